```python
import math
import jax, jax.numpy as jnp
from jax import lax
import numpy as np

D_MODEL = 1024
BATCH = 8
SEQ = 4096
DEPTH = 1

HEAD_DIM = 64
BLOCK = 128
WINDOW = 128
A_Q_HEADS = 8
A_KV_HEADS = 2
A_GROUP = A_Q_HEADS // A_KV_HEADS
B_HEADS = 8
N_BRANCHES = 2
ROPE_THETA = 10000.0
RMS_EPS = 1e-6
_ffn_raw = (8 * D_MODEL + 2) // 3
D_FF = ((_ffn_raw + 255) // 256) * 256

A_Q_W = A_Q_HEADS * HEAD_DIM
A_KV_W = A_KV_HEADS * HEAD_DIM
B_W = B_HEADS * HEAD_DIM
GATE_W = N_BRANCHES * D_MODEL
IN_SPLITS = [A_Q_W, A_KV_W, A_KV_W, B_W, B_W, B_W, B_HEADS, GATE_W]
IN_OFFSETS = list(np.cumsum(IN_SPLITS)[:-1])
IN_W = int(sum(IN_SPLITS))
N_ADA = 6

kernel_name = "hybrid_swa_sink_fox_gated_block"


def rms_norm(x, g):
    xf = x.astype(jnp.float32)
    y = xf * lax.rsqrt(jnp.mean(xf * xf, axis=-1, keepdims=True) + RMS_EPS)
    return (y * g.astype(jnp.float32)).astype(x.dtype)


def rope(x, positions):
    dh = x.shape[-1]
    inv_freq = 1.0 / (ROPE_THETA ** (jnp.arange(0, dh, 2, dtype=jnp.float32) / dh))
    ang = positions.astype(jnp.float32)[..., None] * inv_freq
    cos = jnp.cos(ang)[:, :, None, :]
    sin = jnp.sin(ang)[:, :, None, :]
    xf = x.astype(jnp.float32)
    x1, x2 = xf[..., : dh // 2], xf[..., dh // 2:]
    out = jnp.concatenate([x1 * cos - x2 * sin, x2 * cos + x1 * sin], axis=-1)
    return out.astype(x.dtype)


def sliding_window_gqa_sinks(q, k, v, sinks):
    B, S, _, dh = q.shape
    nb = S // BLOCK
    scale = 1.0 / math.sqrt(dh)
    qb = q.reshape(B, nb, BLOCK, A_KV_HEADS, A_GROUP, dh)
    pad = ((0, 0), (BLOCK, 0), (0, 0), (0, 0))
    kp = jnp.pad(k, pad).reshape(B, nb + 1, BLOCK, A_KV_HEADS, dh)
    vp = jnp.pad(v, pad).reshape(B, nb + 1, BLOCK, A_KV_HEADS, dh)
    kb = jnp.concatenate([kp[:, :-1], kp[:, 1:]], axis=2)
    vb = jnp.concatenate([vp[:, :-1], vp[:, 1:]], axis=2)
    s = jnp.einsum('bnqhgd,bnkhd->bnhgqk', qb, kb).astype(jnp.float32) * scale
    q_loc = jnp.arange(BLOCK) + BLOCK
    k_loc = jnp.arange(2 * BLOCK)
    rel = q_loc[:, None] - k_loc[None, :]
    band = (rel >= 0) & (rel < WINDOW)
    k_abs = jnp.arange(nb)[:, None] * BLOCK + k_loc[None, :] - BLOCK
    valid = band[None, :, :] & (k_abs >= 0)[:, None, :]
    s = jnp.where(valid[None, :, None, None], s, -jnp.inf)
    sink = sinks.astype(jnp.float32).reshape(1, 1, A_KV_HEADS, A_GROUP, 1, 1)
    m = jnp.maximum(jnp.max(s, axis=-1, keepdims=True), sink)
    p = jnp.exp(s - m)
    denom = jnp.sum(p, axis=-1, keepdims=True) + jnp.exp(sink - m)
    p = p / denom
    o = jnp.einsum('bnhgqk,bnkhd->bnqhgd', p, vb.astype(jnp.float32))
    return o.reshape(B, S, A_Q_HEADS * dh)


def forgetting_attention(q, k, v, log_f):
    B, S, H, dh = q.shape
    nb = S // BLOCK
    scale = 1.0 / math.sqrt(dh)
    cum = lax.cumsum(log_f, axis=1)
    cum_k = jnp.transpose(cum, (0, 2, 1))
    k_pos = jnp.arange(S)
    vf = v.astype(jnp.float32)

    def one_block(i):
        start = i * BLOCK
        qi = lax.dynamic_slice_in_dim(q, start, BLOCK, axis=1)
        ci = lax.dynamic_slice_in_dim(cum, start, BLOCK, axis=1)
        s = jnp.einsum('bqhd,bkhd->bhqk', qi, k).astype(jnp.float32) * scale
        s = s + jnp.transpose(ci, (0, 2, 1))[..., None] - cum_k[:, :, None, :]
        q_pos = start + jnp.arange(BLOCK)
        s = jnp.where((k_pos[None, :] <= q_pos[:, None])[None, None], s, -jnp.inf)
        p = jax.nn.softmax(s, axis=-1)
        return jnp.einsum('bhqk,bkhd->bqhd', p, vf)

    o = lax.map(one_block, jnp.arange(nb))
    return jnp.transpose(o, (1, 0, 2, 3, 4)).reshape(B, S, H * dh)


def setup_inputs(seed: int = 0) -> dict:
    key = jax.random.key(seed)
    ks = jax.random.split(key, 20)
    f32 = jnp.float32
    nrm = lambda k, shape, s: jax.random.normal(k, shape, f32) * s
    x = nrm(ks[0], (BATCH, SEQ, D_MODEL), 1.0)
    c = nrm(ks[1], (BATCH, D_MODEL), 1.0)
    positions = jnp.broadcast_to(jnp.arange(SEQ, dtype=jnp.int32), (BATCH, SEQ))
    w_ada = nrm(ks[2], (DEPTH, D_MODEL, N_ADA * D_MODEL), 0.5 * D_MODEL ** -0.5)
    b_ada = nrm(ks[3], (DEPTH, N_ADA * D_MODEL), 0.01)
    g_pre_mix = 1.0 + nrm(ks[4], (DEPTH, D_MODEL), 0.05)
    g_post_mix = 1.0 + nrm(ks[5], (DEPTH, D_MODEL), 0.05)
    w_in = nrm(ks[6], (DEPTH, D_MODEL, IN_W), D_MODEL ** -0.5)
    b_f = 1.0 + nrm(ks[7], (DEPTH, B_HEADS), 0.1)
    sinks = nrm(ks[8], (DEPTH, A_Q_HEADS), 0.5)
    w_branch_a = nrm(ks[9], (DEPTH, A_Q_W, D_MODEL), A_Q_W ** -0.5)
    w_branch_b = nrm(ks[10], (DEPTH, B_W, D_MODEL), B_W ** -0.5)
    w_out = nrm(ks[11], (DEPTH, D_MODEL, D_MODEL), D_MODEL ** -0.5)
    g_pre_ffn = 1.0 + nrm(ks[12], (DEPTH, D_MODEL), 0.05)
    g_post_ffn = 1.0 + nrm(ks[13], (DEPTH, D_MODEL), 0.05)
    w_ffn_in = nrm(ks[14], (DEPTH, D_MODEL, 2 * D_FF), D_MODEL ** -0.5)
    w_ffn_out = nrm(ks[15], (DEPTH, D_FF, D_MODEL), D_FF ** -0.5)
    return {"x": x, "c": c, "positions": positions, "w_ada": w_ada, "b_ada": b_ada,
            "g_pre_mix": g_pre_mix, "g_post_mix": g_post_mix, "w_in": w_in, "b_f": b_f,
            "sinks": sinks, "w_branch_a": w_branch_a, "w_branch_b": w_branch_b,
            "w_out": w_out, "g_pre_ffn": g_pre_ffn, "g_post_ffn": g_post_ffn,
            "w_ffn_in": w_ffn_in, "w_ffn_out": w_ffn_out}


def reference(x, c, positions, w_ada, b_ada, g_pre_mix, g_post_mix, w_in, b_f, sinks,
              w_branch_a, w_branch_b, w_out, g_pre_ffn, g_post_ffn, w_ffn_in, w_ffn_out):
    B, S, D = x.shape
    for l in range(DEPTH):
        ada = (c @ w_ada[l] + b_ada[l]).reshape(B, N_ADA, D)[:, :, None, :]
        shift_m, scale_m, gate_m = ada[:, 0], ada[:, 1], ada[:, 2]
        shift_f, scale_f, gate_f = ada[:, 3], ada[:, 4], ada[:, 5]

        h = rms_norm(x, g_pre_mix[l]) * (1.0 + scale_m) + shift_m
        proj = h @ w_in[l]
        qa, ka, va, qb, kb, vb, f_logit, gate_logit = jnp.split(proj, IN_OFFSETS, axis=-1)
        qa = rope(qa.reshape(B, S, A_Q_HEADS, HEAD_DIM), positions)
        ka = rope(ka.reshape(B, S, A_KV_HEADS, HEAD_DIM), positions)
        va = va.reshape(B, S, A_KV_HEADS, HEAD_DIM)
        o_a = sliding_window_gqa_sinks(qa, ka, va, sinks[l]).astype(x.dtype)

        log_f = jax.nn.log_sigmoid((f_logit + b_f[l]).astype(jnp.float32))
        o_b = forgetting_attention(qb.reshape(B, S, B_HEADS, HEAD_DIM),
                                   kb.reshape(B, S, B_HEADS, HEAD_DIM),
                                   vb.reshape(B, S, B_HEADS, HEAD_DIM), log_f).astype(x.dtype)

        gates = jax.nn.sigmoid(gate_logit).reshape(B, S, N_BRANCHES, D)
        merged = gates[:, :, 0] * (o_a @ w_branch_a[l]) + gates[:, :, 1] * (o_b @ w_branch_b[l])
        y = merged @ w_out[l]
        x = x + gate_m * rms_norm(y, g_post_mix[l])

        h = rms_norm(x, g_pre_ffn[l]) * (1.0 + scale_f) + shift_f
        gu = h @ w_ffn_in[l]
        g_part, u_part = gu[..., :D_FF], gu[..., D_FF:]
        y = (jax.nn.silu(g_part) * u_part) @ w_ffn_out[l]
        x = x + gate_f * rms_norm(y, g_post_ffn[l])
    return x
```

```python
import functools
import math

import jax
import jax.numpy as jnp
from jax import lax
from jax.experimental import pallas as pl
from jax.experimental.pallas import tpu as pltpu

HEAD_DIM = 64
WINDOW = 128
A_Q_HEADS = 8
A_KV_HEADS = 2
A_GROUP = A_Q_HEADS // A_KV_HEADS
B_HEADS = 8
N_ADA = 6
ROPE_THETA = 10000.0
RMS_EPS = 1e-6
LANES = 128
VMEM_LIMIT = 56 * 1024 * 1024

A_Q_W = A_Q_HEADS * HEAD_DIM
A_KV_W = A_KV_HEADS * HEAD_DIM
B_W = B_HEADS * HEAD_DIM
QKV_W = A_Q_W + 2 * A_KV_W + 3 * B_W
BF16 = jnp.bfloat16
F32 = jnp.float32
NEG_INF = float("-inf")


def _params(*sem):
    return pltpu.CompilerParams(dimension_semantics=sem, vmem_limit_bytes=VMEM_LIMIT)


def _rms_mod(x, g, scale, shift):
    ms = jnp.mean(x * x, axis=-1, keepdims=True)
    return (x * lax.rsqrt(ms + RMS_EPS) * g) * (1.0 + scale) + shift


def _rms(y, g):
    ms = jnp.mean(y * y, axis=-1, keepdims=True)
    return y * lax.rsqrt(ms + RMS_EPS) * g


def _ada_kernel(c_ref, w_ref, b_ref, o_ref):
    o_ref[...] = jnp.dot(c_ref[...], w_ref[...], preferred_element_type=F32) + b_ref[...]


def _ada(c, w, b):
    bsz, d = c.shape
    n = w.shape[1]
    tn = n // 4
    return pl.pallas_call(
        _ada_kernel,
        grid=(n // tn,),
        in_specs=[pl.BlockSpec((bsz, d), lambda j: (0, 0)),
                  pl.BlockSpec((d, tn), lambda j: (0, j)),
                  pl.BlockSpec((1, tn), lambda j: (0, j))],
        out_specs=pl.BlockSpec((bsz, tn), lambda j: (0, j)),
        out_shape=jax.ShapeDtypeStruct((bsz, n), F32),
        compiler_params=_params("arbitrary"),
        name="ada",
    )(c, w, b.reshape(1, n))


def _inproj_kernel(x_ref, pos_ref, ada_ref, g_ref, w_ref, wf_ref, bf_ref, invf_ref,
                   qa_ref, ka_ref, va_ref, qb_ref, kb_ref, vb_ref, lf_ref):
    x = x_ref[0]
    tm = x.shape[0]
    h = _rms_mod(x, g_ref[...], ada_ref[0, 1:2, :], ada_ref[0, 0:1, :]).astype(BF16)

    def proj(lo, width):
        return jnp.dot(h, w_ref[:, lo:lo + width], preferred_element_type=F32)

    ang = pos_ref[0] * invf_ref[...]
    cos = jnp.cos(ang)
    sin = jnp.sin(ang)
    lane = lax.broadcasted_iota(jnp.int32, (tm, LANES), 1)
    first_half = (lane % HEAD_DIM) < (HEAD_DIM // 2)
    sin_signed = jnp.where(first_half, -sin, sin)

    def rope(t):
        partner = jnp.where(first_half,
                            pltpu.roll(t, LANES - HEAD_DIM // 2, axis=1),
                            pltpu.roll(t, HEAD_DIM // 2, axis=1))
        return t * cos + partner * sin_signed

    sm_scale = 1.0 / math.sqrt(HEAD_DIM)
    qa = proj(0, A_Q_W)
    for j in range(A_Q_W // LANES):
        qa_ref[0, :, j * LANES:(j + 1) * LANES] = (
            rope(qa[:, j * LANES:(j + 1) * LANES]) * sm_scale).astype(BF16)
    kva = proj(A_Q_W, 2 * A_KV_W)
    ka_ref[0] = rope(kva[:, :A_KV_W]).astype(BF16)
    va_ref[0] = kva[:, A_KV_W:].astype(BF16)
    off = A_Q_W + 2 * A_KV_W
    qb_ref[0] = (proj(off, B_W) * sm_scale).astype(BF16)
    kb_ref[0] = proj(off + B_W, B_W).astype(BF16)
    vb_ref[0] = proj(off + 2 * B_W, B_W).astype(BF16)
    fl = jnp.dot(h, wf_ref[...], preferred_element_type=F32) + bf_ref[...]
    lf_ref[0] = jnp.minimum(fl, 0.0) - jnp.log1p(jnp.exp(-jnp.abs(fl)))


def _in_proj(x, pos, ada, g_pre, w_qkv, w_f, b_f, inv_freq, tm):
    bsz, s, d = x.shape
    grid = (bsz, s // tm)
    tok = lambda w: pl.BlockSpec((1, tm, w), lambda b, i: (b, i, 0))
    const = lambda shape: pl.BlockSpec(shape, lambda b, i: (0,) * len(shape))
    out_w = [A_Q_W, A_KV_W, A_KV_W, B_W, B_W, B_W]
    return pl.pallas_call(
        _inproj_kernel,
        grid=grid,
        in_specs=[tok(d), tok(1),
                  pl.BlockSpec((1, N_ADA, d), lambda b, i: (b, 0, 0)),
                  const((1, d)), const(w_qkv.shape), const(w_f.shape),
                  const((1, LANES)), const((1, LANES))],
        out_specs=[tok(w) for w in out_w] + [tok(LANES)],
        out_shape=[jax.ShapeDtypeStruct((bsz, s, w), BF16) for w in out_w]
        + [jax.ShapeDtypeStruct((bsz, s, LANES), F32)],
        compiler_params=_params("parallel", "parallel"),
        name="in_proj",
    )(x, pos, ada, g_pre, w_qkv, w_f, b_f, inv_freq)


def _cumsum_kernel(lf_ref, o_ref):
    x = lf_ref[0]
    s = x.shape[0]
    row = lax.broadcasted_iota(jnp.int32, x.shape, 0)
    d = 1
    while d < s:
        x = x + jnp.where(row >= d, pltpu.roll(x, d, axis=0), 0.0)
        d *= 2
    o_ref[0] = jnp.transpose(x)[:B_HEADS, :]


def _cumsum(lf):
    bsz, s, _ = lf.shape
    return pl.pallas_call(
        _cumsum_kernel,
        grid=(bsz,),
        in_specs=[pl.BlockSpec((1, s, LANES), lambda b: (b, 0, 0))],
        out_specs=pl.BlockSpec((1, B_HEADS, s), lambda b: (b, 0, 0)),
        out_shape=jax.ShapeDtypeStruct((bsz, B_HEADS, s), F32),
        compiler_params=_params("parallel"),
        name="cumsum",
    )(lf)


def _swa_kernel(q_ref, kp_ref, kc_ref, vp_ref, vc_ref, sink_ref, o_ref):
    i = pl.program_id(1)
    tq = q_ref.shape[1]
    nsub = tq // WINDOW
    ncol = A_Q_W // LANES
    lane_q = lax.broadcasted_iota(jnp.int32, (WINDOW, LANES), 1)
    low = lane_q < HEAD_DIM
    q_loc = lax.broadcasted_iota(jnp.int32, (WINDOW, 2 * WINDOW), 0) + WINDOW
    k_loc = lax.broadcasted_iota(jnp.int32, (WINDOW, 2 * WINDOW), 1)
    rel = q_loc - k_loc
    band = (rel >= 0) & (rel < WINDOW)
    for j in range(nsub):
        if j == 0:
            kk = jnp.concatenate([kp_ref[0], kc_ref[0, :WINDOW, :]], axis=0)
            vv = jnp.concatenate([vp_ref[0], vc_ref[0, :WINDOW, :]], axis=0)
            valid = band & ((k_loc >= WINDOW) | (i > 0))
        else:
            kk = kc_ref[0, (j - 1) * WINDOW:(j + 1) * WINDOW, :]
            vv = vc_ref[0, (j - 1) * WINDOW:(j + 1) * WINDOW, :]
            valid = band
        q = q_ref[0, j * WINDOW:(j + 1) * WINDOW, :]
        zero = jnp.zeros((WINDOW, LANES), BF16)
        parts = [jnp.where(low, q[:, c * LANES:(c + 1) * LANES], zero) for c in range(ncol)]
        parts += [jnp.where(low, zero, q[:, c * LANES:(c + 1) * LANES]) for c in range(ncol)]
        qs = jnp.concatenate(parts, axis=0)
        s = lax.dot_general(qs, kk, (((1,), (1,)), ((), ())), preferred_element_type=F32)
        outs = []
        for hh in range(A_Q_HEADS):
            sh = jnp.where(valid, s[hh * WINDOW:(hh + 1) * WINDOW, :], NEG_INF)
            sink = sink_ref[:, hh:hh + 1]
            m = jnp.maximum(jnp.max(sh, axis=-1, keepdims=True), sink)
            p = jnp.exp(sh - m)
            denom = jnp.sum(p, axis=-1, keepdims=True) + jnp.exp(sink - m)
            o = jnp.dot(p.astype(BF16), vv, preferred_element_type=F32)
            outs.append(o / denom)
        for c in range(ncol):
            o_ref[0, j * WINDOW:(j + 1) * WINDOW, c * LANES:(c + 1) * LANES] = (
                jnp.where(low, outs[c], outs[ncol + c]).astype(BF16))


def _swa(qa, ka, va, sinks, tq):
    bsz, s, _ = qa.shape
    r = tq // WINDOW
    prev = lambda b, i: (b, jnp.maximum(i * r - 1, 0), 0)
    cur = lambda b, i: (b, i, 0)
    return pl.pallas_call(
        _swa_kernel,
        grid=(bsz, s // tq),
        in_specs=[pl.BlockSpec((1, tq, A_Q_W), cur),
                  pl.BlockSpec((1, WINDOW, A_KV_W), prev),
                  pl.BlockSpec((1, tq, A_KV_W), cur),
                  pl.BlockSpec((1, WINDOW, A_KV_W), prev),
                  pl.BlockSpec((1, tq, A_KV_W), cur),
                  pl.BlockSpec((1, LANES), lambda b, i: (0, 0))],
        out_specs=pl.BlockSpec((1, tq, A_Q_W), cur),
        out_shape=jax.ShapeDtypeStruct((bsz, s, A_Q_W), BF16),
        compiler_params=_params("parallel", "parallel"),
        name="swa",
    )(qa, ka, ka, va, va, sinks)


def _fox_kernel(q_ref, k_ref, v_ref, cum_ref, o_ref, m_ref, l_ref, acc_ref):
    qi = pl.program_id(2)
    tq = q_ref.shape[1]
    tk = tq
    q = q_ref[0]
    lane = lax.broadcasted_iota(jnp.int32, (tq, LANES), 1)
    low = lane < HEAD_DIM
    zero = jnp.zeros_like(q)
    qs = (jnp.where(low, q, zero), jnp.where(low, zero, q))
    ct = tuple(jnp.transpose(jnp.broadcast_to(cum_ref[0, e, qi], (LANES, tq))) for e in range(2))
    m_ref[...] = jnp.full(m_ref.shape, NEG_INF, F32)
    l_ref[...] = jnp.zeros(l_ref.shape, F32)
    acc_ref[...] = jnp.zeros(acc_ref.shape, F32)
    row = lax.broadcasted_iota(jnp.int32, (tq, tk), 0)
    col = lax.broadcasted_iota(jnp.int32, (tq, tk), 1)
    causal = col <= row

    def block(kj, diagonal):
        start = pl.multiple_of(kj * tk, tk)
        k = k_ref[0, pl.ds(start, tk), :]
        v = v_ref[0, pl.ds(start, tk), :]
        for e in range(2):
            s = lax.dot_general(qs[e], k, (((1,), (1,)), ((), ())), preferred_element_type=F32)
            z = s + ct[e][:, :1] - cum_ref[0, e, kj]
            if diagonal:
                z = jnp.where(causal, z, NEG_INF)
            m_old = m_ref[e]
            m_new = jnp.maximum(m_old, jnp.max(z, axis=-1, keepdims=True))
            alpha = jnp.exp(m_old - m_new)
            p = jnp.exp(z - m_new[:, :1])
            l_ref[e] = alpha * l_ref[e] + jnp.sum(p, axis=-1, keepdims=True)
            acc_ref[e] = alpha * acc_ref[e] + jnp.dot(p.astype(BF16), v, preferred_element_type=F32)
            m_ref[e] = m_new

    def body(kj, carry):
        block(kj, False)
        return carry

    lax.fori_loop(0, qi, body, 0)
    block(qi, True)
    o_ref[0] = jnp.where(low, acc_ref[0] / l_ref[0], acc_ref[1] / l_ref[1]).astype(BF16)


def _fox(qb, kb, vb, cum, tq):
    bsz, s, _ = qb.shape
    nq = s // tq
    npair = B_W // LANES
    cum5 = cum.reshape(bsz, B_HEADS, nq, 1, tq)
    return pl.pallas_call(
        _fox_kernel,
        grid=(bsz, npair, nq),
        in_specs=[pl.BlockSpec((1, tq, LANES), lambda b, hp, i: (b, i, hp)),
                  pl.BlockSpec((1, s, LANES), lambda b, hp, i: (b, 0, hp)),
                  pl.BlockSpec((1, s, LANES), lambda b, hp, i: (b, 0, hp)),
                  pl.BlockSpec((1, 2, nq, 1, tq), lambda b, hp, i: (b, hp, 0, 0, 0))],
        out_specs=pl.BlockSpec((1, tq, LANES), lambda b, hp, i: (b, i, hp)),
        out_shape=jax.ShapeDtypeStruct((bsz, s, B_W), BF16),
        scratch_shapes=[pltpu.VMEM((2, tq, LANES), F32),
                        pltpu.VMEM((2, tq, LANES), F32),
                        pltpu.VMEM((2, tq, LANES), F32)],
        compiler_params=_params("parallel", "parallel", "arbitrary"),
        name="fox",
    )(qb, kb, vb, cum5)


def _post_kernel(x_ref, oa_ref, ob_ref, ada_ref, gpre_ref, gpost_ref, wg_ref, wa_ref, wb_ref,
                 wo_ref, o_ref):
    x = x_ref[0]
    d = x.shape[1]
    h = _rms_mod(x, gpre_ref[...], ada_ref[0, 1:2, :], ada_ref[0, 0:1, :]).astype(BF16)
    a = jnp.dot(oa_ref[0], wa_ref[...], preferred_element_type=F32)
    b = jnp.dot(ob_ref[0], wb_ref[...], preferred_element_type=F32)
    ga = jax.nn.sigmoid(jnp.dot(h, wg_ref[:, :d], preferred_element_type=F32))
    gb = jax.nn.sigmoid(jnp.dot(h, wg_ref[:, d:], preferred_element_type=F32))
    merged = (ga * a + gb * b).astype(BF16)
    y = jnp.dot(merged, wo_ref[...], preferred_element_type=F32)
    o_ref[0] = x + ada_ref[0, 2:3, :] * _rms(y, gpost_ref[...])


def _post(x, oa, ob, ada, g_pre, g_post, wg, wa, wb, wo, tm):
    bsz, s, d = x.shape
    tok = lambda w: pl.BlockSpec((1, tm, w), lambda b, i: (b, i, 0))
    const = lambda shape: pl.BlockSpec(shape, lambda b, i: (0,) * len(shape))
    return pl.pallas_call(
        _post_kernel,
        grid=(bsz, s // tm),
        in_specs=[tok(d), tok(A_Q_W), tok(B_W),
                  pl.BlockSpec((1, N_ADA, d), lambda b, i: (b, 0, 0)),
                  const((1, d)), const((1, d)),
                  const(wg.shape), const(wa.shape), const(wb.shape), const(wo.shape)],
        out_specs=tok(d),
        out_shape=jax.ShapeDtypeStruct((bsz, s, d), F32),
        compiler_params=_params("parallel", "parallel"),
        name="post",
    )(x, oa, ob, ada, g_pre, g_post, wg, wa, wb, wo)


def _ffn_kernel(x_ref, ada_ref, gpre_ref, gpost_ref, wi_ref, wo_ref, o_ref, *, chunk):
    x = x_ref[0]
    d_ff = wo_ref.shape[0]
    h = _rms_mod(x, gpre_ref[...], ada_ref[0, 4:5, :], ada_ref[0, 3:4, :]).astype(BF16)
    y = jnp.zeros(x.shape, F32)
    for c in range(d_ff // chunk):
        g = jnp.dot(h, wi_ref[:, c * chunk:(c + 1) * chunk], preferred_element_type=F32)
        u = jnp.dot(h, wi_ref[:, d_ff + c * chunk:d_ff + (c + 1) * chunk], preferred_element_type=F32)
        act = (g * jax.nn.sigmoid(g) * u).astype(BF16)
        y = y + jnp.dot(act, wo_ref[c * chunk:(c + 1) * chunk, :], preferred_element_type=F32)
    o_ref[0] = x + ada_ref[0, 5:6, :] * _rms(y, gpost_ref[...])


def _ffn(x, ada, g_pre, g_post, wi, wo, tm, chunk):
    bsz, s, d = x.shape
    tok = pl.BlockSpec((1, tm, d), lambda b, i: (b, i, 0))
    const = lambda shape: pl.BlockSpec(shape, lambda b, i: (0,) * len(shape))
    return pl.pallas_call(
        functools.partial(_ffn_kernel, chunk=chunk),
        grid=(bsz, s // tm),
        in_specs=[tok, pl.BlockSpec((1, N_ADA, d), lambda b, i: (b, 0, 0)),
                  const((1, d)), const((1, d)), const(wi.shape), const(wo.shape)],
        out_specs=tok,
        out_shape=jax.ShapeDtypeStruct((bsz, s, d), F32),
        compiler_params=_params("parallel", "parallel"),
        name="ffn",
    )(x, ada, g_pre, g_post, wi, wo)


def _pair_heads(w, axis):
    shape = w.shape
    pre, post = shape[:axis], shape[axis + 1:]
    w = w.reshape(pre + (A_KV_HEADS, A_GROUP, HEAD_DIM) + post)
    w = jnp.swapaxes(w, axis, axis + 1)
    return w.reshape(shape)


def kernel(x, c, positions, w_ada, b_ada, g_pre_mix, g_post_mix, w_in, b_f, sinks, w_branch_a,
           w_branch_b, w_out, g_pre_ffn, g_post_ffn, w_ffn_in, w_ffn_out):
    bsz, s, d = x.shape
    depth = w_ada.shape[0]
    tm = min(512, s)
    tq_fox = min(512, s)
    pos = positions.astype(F32).reshape(bsz, s, 1)
    half = jnp.arange(0, HEAD_DIM, 2, dtype=F32) / HEAD_DIM
    inv_freq = jnp.tile(1.0 / (ROPE_THETA ** half), LANES // (HEAD_DIM // 2)).reshape(1, LANES)
    for l in range(depth):
        w_l = w_in[l]
        w_qkv = jnp.concatenate([_pair_heads(w_l[:, :A_Q_W], 1), w_l[:, A_Q_W:QKV_W]], axis=1).astype(BF16)
        w_f = jnp.pad(w_l[:, QKV_W:QKV_W + B_HEADS], ((0, 0), (0, LANES - B_HEADS))).astype(BF16)
        w_g = w_l[:, QKV_W + B_HEADS:].astype(BF16)
        bf_row = jnp.pad(b_f[l], (0, LANES - B_HEADS)).reshape(1, LANES)
        sink_row = jnp.pad(sinks[l], (0, LANES - A_Q_HEADS)).reshape(1, LANES)
        w_a = _pair_heads(w_branch_a[l], 0).astype(BF16)
        w_b = w_branch_b[l].astype(BF16)
        w_o = w_out[l].astype(BF16)
        w_fi = w_ffn_in[l].astype(BF16)
        w_fo = w_ffn_out[l].astype(BF16)
        row = lambda g: g[l].reshape(1, d)

        ada = _ada(c, w_ada[l], b_ada[l]).reshape(bsz, N_ADA, d)
        qa, ka, va, qb, kb, vb, lf = _in_proj(x, pos, ada, row(g_pre_mix), w_qkv, w_f, bf_row,
                                              inv_freq, tm)
        cum = _cumsum(lf)
        o_a = _swa(qa, ka, va, sink_row, tm)
        o_b = _fox(qb, kb, vb, cum, tq_fox)
        x = _post(x, o_a, o_b, ada, row(g_pre_mix), row(g_post_mix), w_g, w_a, w_b, w_o, min(256, s))
        x = _ffn(x, ada, row(g_pre_ffn), row(g_post_ffn), w_fi, w_fo, min(256, s),
                 w_ffn_out.shape[1] // 2)
    return x
```

```python
import functools
import math

import jax
import jax.numpy as jnp
from jax import lax
from jax.experimental import pallas as pl
from jax.experimental.pallas import tpu as pltpu

HEAD_DIM = 64
WINDOW = 128
A_Q_HEADS = 8
A_KV_HEADS = 2
A_GROUP = A_Q_HEADS // A_KV_HEADS
B_HEADS = 8
N_ADA = 6
ROPE_THETA = 10000.0
RMS_EPS = 1e-6
LANES = 128
VMEM_LIMIT = 56 * 1024 * 1024

A_Q_W = A_Q_HEADS * HEAD_DIM
A_KV_W = A_KV_HEADS * HEAD_DIM
B_W = B_HEADS * HEAD_DIM
QKV_W = A_Q_W + 2 * A_KV_W + 3 * B_W
BF16 = jnp.bfloat16
F32 = jnp.float32
NEG_INF = float("-inf")
LOG2E = math.log2(math.e)
ONES_ROWS = 16


def _params(*sem):
    return pltpu.CompilerParams(dimension_semantics=sem, vmem_limit_bytes=VMEM_LIMIT)


def _rms_mod(x, g, scale, shift):
    ms = jnp.mean(x * x, axis=-1, keepdims=True)
    return (x * lax.rsqrt(ms + RMS_EPS) * g) * (1.0 + scale) + shift


def _rms(y, g):
    ms = jnp.mean(y * y, axis=-1, keepdims=True)
    return y * lax.rsqrt(ms + RMS_EPS) * g


def _ada_kernel(c_ref, w_ref, b_ref, o_ref):
    o_ref[...] = jnp.dot(c_ref[...], w_ref[...], preferred_element_type=F32) + b_ref[...]


def _ada(c, w, b):
    bsz, d = c.shape
    n = w.shape[1]
    tn = n // 4
    return pl.pallas_call(
        _ada_kernel,
        grid=(n // tn,),
        in_specs=[pl.BlockSpec((bsz, d), lambda j: (0, 0)),
                  pl.BlockSpec((d, tn), lambda j: (0, j)),
                  pl.BlockSpec((1, tn), lambda j: (0, j))],
        out_specs=pl.BlockSpec((bsz, tn), lambda j: (0, j)),
        out_shape=jax.ShapeDtypeStruct((bsz, n), F32),
        compiler_params=_params("arbitrary"),
        name="ada",
    )(c, w, b.reshape(1, n))


def _inproj_kernel(x_ref, pos_ref, ada_ref, g_ref, w_ref, wvt_ref, wf_ref, bf_ref, invf_ref,
                   qa_ref, ka_ref, vat_ref, qb_ref, kb_ref, vbt_ref, lf_ref):
    x = x_ref[0]
    tm = x.shape[0]
    h = _rms_mod(x, g_ref[...], ada_ref[0, 1:2, :], ada_ref[0, 0:1, :]).astype(BF16)

    def proj(lo, width):
        return jnp.dot(h, w_ref[:, lo:lo + width], preferred_element_type=F32)

    def proj_t(lo, width):
        return lax.dot_general(wvt_ref[lo:lo + width, :], h, (((1,), (1,)), ((), ())),
                               preferred_element_type=F32)

    ang = pos_ref[0] * invf_ref[...]
    cos = jnp.cos(ang)
    sin = jnp.sin(ang)
    lane = lax.broadcasted_iota(jnp.int32, (tm, LANES), 1)
    first_half = (lane % HEAD_DIM) < (HEAD_DIM // 2)
    sin_signed = jnp.where(first_half, -sin, sin)

    def rope(t):
        partner = jnp.where(first_half,
                            pltpu.roll(t, LANES - HEAD_DIM // 2, axis=1),
                            pltpu.roll(t, HEAD_DIM // 2, axis=1))
        return t * cos + partner * sin_signed

    q_scale = LOG2E / math.sqrt(HEAD_DIM)
    qa = proj(0, A_Q_W)
    for j in range(A_Q_W // LANES):
        qa_ref[0, :, j * LANES:(j + 1) * LANES] = (
            rope(qa[:, j * LANES:(j + 1) * LANES]) * q_scale).astype(BF16)
    ka_ref[0] = rope(proj(A_Q_W, A_KV_W)).astype(BF16)
    off = A_Q_W + A_KV_W
    qb_ref[0] = (proj(off, B_W) * q_scale).astype(BF16)
    kb_ref[0] = proj(off + B_W, B_W).astype(BF16)
    vat_ref[0] = proj_t(0, A_KV_W).astype(BF16)
    vbt_ref[0, 0] = proj_t(A_KV_W, B_W).astype(BF16)
    fl = jnp.dot(h, wf_ref[...], preferred_element_type=F32) + bf_ref[...]
    lf_ref[0] = jnp.minimum(fl, 0.0) - jnp.log1p(jnp.exp(-jnp.abs(fl)))


def _in_proj(x, pos, ada, g_pre, w_qk, w_vt, w_f, b_f, inv_freq, tm):
    bsz, s, d = x.shape
    grid = (bsz, s // tm)
    tok = lambda w: pl.BlockSpec((1, tm, w), lambda b, i: (b, i, 0))
    const = lambda shape: pl.BlockSpec(shape, lambda b, i: (0,) * len(shape))
    act = lambda w: jax.ShapeDtypeStruct((bsz, s, w), BF16)
    return pl.pallas_call(
        _inproj_kernel,
        grid=grid,
        in_specs=[tok(d), tok(1),
                  pl.BlockSpec((1, N_ADA, d), lambda b, i: (b, 0, 0)),
                  const((1, d)), const(w_qk.shape), const(w_vt.shape), const(w_f.shape),
                  const((1, LANES)), const((1, LANES))],
        out_specs=[tok(A_Q_W), tok(A_KV_W),
                   pl.BlockSpec((1, A_KV_W, tm), lambda b, i: (b, 0, i)),
                   tok(B_W), tok(B_W),
                   pl.BlockSpec((1, 1, B_W, tm), lambda b, i: (b, i, 0, 0)),
                   tok(LANES)],
        out_shape=[act(A_Q_W), act(A_KV_W),
                   jax.ShapeDtypeStruct((bsz, A_KV_W, s), BF16),
                   act(B_W), act(B_W),
                   jax.ShapeDtypeStruct((bsz, s // tm, B_W, tm), BF16),
                   jax.ShapeDtypeStruct((bsz, s, LANES), F32)],
        compiler_params=_params("parallel", "parallel"),
        name="in_proj",
    )(x, pos, ada, g_pre, w_qk, w_vt, w_f, b_f, inv_freq)


def _cumsum_kernel(lf_ref, o_ref):
    x = lf_ref[0]
    s = x.shape[0]
    row = lax.broadcasted_iota(jnp.int32, x.shape, 0)
    d = 1
    while d < s:
        x = x + jnp.where(row >= d, pltpu.roll(x, d, axis=0), 0.0)
        d *= 2
    o_ref[0] = jnp.transpose(x)[:B_HEADS, :] * LOG2E


def _cumsum(lf):
    bsz, s, _ = lf.shape
    return pl.pallas_call(
        _cumsum_kernel,
        grid=(bsz,),
        in_specs=[pl.BlockSpec((1, s, LANES), lambda b: (b, 0, 0))],
        out_specs=pl.BlockSpec((1, B_HEADS, s), lambda b: (b, 0, 0)),
        out_shape=jax.ShapeDtypeStruct((bsz, B_HEADS, s), F32),
        compiler_params=_params("parallel"),
        name="cumsum",
    )(lf)


def _swa_kernel(q_ref, kp_ref, kc_ref, vtp_ref, vtc_ref, sink_ref, o_ref):
    i = pl.program_id(1)
    tq = q_ref.shape[1]
    nsub = tq // WINDOW
    ncol = A_Q_W // LANES
    lane_q = lax.broadcasted_iota(jnp.int32, (WINDOW, LANES), 1)
    low = lane_q < HEAD_DIM
    low_dim = lax.broadcasted_iota(jnp.int32, (LANES, WINDOW), 0) < HEAD_DIM
    k_loc = lax.broadcasted_iota(jnp.int32, (2 * WINDOW, WINDOW), 0)
    q_loc = lax.broadcasted_iota(jnp.int32, (2 * WINDOW, WINDOW), 1) + WINDOW
    rel = q_loc - k_loc
    band = (rel >= 0) & (rel < WINDOW)
    sink = sink_ref[...] * LOG2E
    ones = jnp.ones((ONES_ROWS, 2 * WINDOW), BF16)
    zero = jnp.zeros((WINDOW, LANES), BF16)
    for j in range(nsub):
        if j == 0:
            kk = jnp.concatenate([kp_ref[0], kc_ref[0, :WINDOW, :]], axis=0)
            vt = jnp.concatenate([vtp_ref[0], vtc_ref[0, :, :WINDOW]], axis=1)
            valid = band & ((k_loc >= WINDOW) | (i > 0))
        else:
            kk = kc_ref[0, (j - 1) * WINDOW:(j + 1) * WINDOW, :]
            vt = vtc_ref[0, :, (j - 1) * WINDOW:(j + 1) * WINDOW]
            valid = band
        q = q_ref[0, j * WINDOW:(j + 1) * WINDOW, :]
        parts = [jnp.where(low, q[:, c * LANES:(c + 1) * LANES], zero) for c in range(ncol)]
        parts += [jnp.where(low, zero, q[:, c * LANES:(c + 1) * LANES]) for c in range(ncol)]
        qs = jnp.concatenate(parts, axis=0)
        st = lax.dot_general(kk, qs, (((1,), (1,)), ((), ())), preferred_element_type=F32)
        ps, sink_terms = [], []
        for hh in range(A_Q_HEADS):
            cols = slice(hh * WINDOW, (hh + 1) * WINDOW)
            w = jnp.where(valid, st[:, cols], NEG_INF)
            m = jnp.maximum(jnp.max(w, axis=0, keepdims=True), sink[:, cols])
            ps.append(jnp.exp2(w - m).astype(BF16))
            sink_terms.append(jnp.exp2(sink[:, cols] - m))
        pt = jnp.concatenate(ps, axis=1)
        ot = jnp.dot(jnp.concatenate([vt, ones], axis=0), pt, preferred_element_type=F32)
        for c in range(ncol):
            halves = []
            for hh in (c, ncol + c):
                cols = slice(hh * WINDOW, (hh + 1) * WINDOW)
                denom = ot[LANES:LANES + 1, cols] + sink_terms[hh]
                halves.append(ot[:LANES, cols] * (1.0 / denom))
            o_ref[0, j * WINDOW:(j + 1) * WINDOW, c * LANES:(c + 1) * LANES] = (
                jnp.transpose(jnp.where(low_dim, halves[0], halves[1])).astype(BF16))


def _swa(qa, ka, vat, sinks, tq):
    bsz, s, _ = qa.shape
    r = tq // WINDOW
    cur = lambda b, i: (b, i, 0)
    return pl.pallas_call(
        _swa_kernel,
        grid=(bsz, s // tq),
        in_specs=[pl.BlockSpec((1, tq, A_Q_W), cur),
                  pl.BlockSpec((1, WINDOW, A_KV_W), lambda b, i: (b, jnp.maximum(i * r - 1, 0), 0)),
                  pl.BlockSpec((1, tq, A_KV_W), cur),
                  pl.BlockSpec((1, A_KV_W, WINDOW), lambda b, i: (b, 0, jnp.maximum(i * r - 1, 0))),
                  pl.BlockSpec((1, A_KV_W, tq), lambda b, i: (b, 0, i)),
                  pl.BlockSpec((1, A_Q_HEADS * WINDOW), lambda b, i: (0, 0))],
        out_specs=pl.BlockSpec((1, tq, A_Q_W), cur),
        out_shape=jax.ShapeDtypeStruct((bsz, s, A_Q_W), BF16),
        compiler_params=_params("parallel", "parallel"),
        name="swa",
    )(qa, ka, ka, vat, vat, sinks)


def _fox_kernel(q_ref, k_ref, vt_ref, cum_ref, o_ref, m_ref, acc_ref, csl_ref, qs_ref, sta_ref, stb_ref):
    qi = pl.program_id(2)
    tq = q_ref.shape[1]
    tk = vt_ref.shape[3]
    nk = vt_ref.shape[1]
    nchunk = tq // LANES
    q = q_ref[0]
    lane = lax.broadcasted_iota(jnp.int32, (tq, LANES), 1)
    low = lane < HEAD_DIM
    zero = jnp.zeros_like(q)
    qs_ref[0] = jnp.where(low, q, zero)
    qs_ref[1] = jnp.where(low, zero, q)

    @pl.when(qi == 0)
    def _():
        for e in range(2):
            for j in range(nk):
                csl_ref[e, j * tk:(j + 1) * tk, :] = jnp.transpose(
                    jnp.broadcast_to(cum_ref[0, e, j], (LANES, tk)))

    m_ref[...] = jnp.full(m_ref.shape, NEG_INF, F32)
    acc_ref[...] = jnp.zeros(acc_ref.shape, F32)
    key = lax.broadcasted_iota(jnp.int32, (tk, LANES), 0)
    qry = lax.broadcasted_iota(jnp.int32, (tk, LANES), 1)
    ones = jnp.ones((ONES_ROWS, tk), BF16)

    def scores(kj, st_ref):
        start = pl.multiple_of(kj * tk, tk)
        k = k_ref[0, pl.ds(start, tk), :]
        for e in range(2):
            st_ref[e] = lax.dot_general(k, qs_ref[e], (((1,), (1,)), ((), ())),
                                        preferred_element_type=F32)

    def softmax_pv(kj, st_ref, diagonal):
        start = pl.multiple_of(kj * tk, tk)
        vta = jnp.concatenate([vt_ref[0, kj], ones], axis=0)
        for e in range(2):
            ct = cum_ref[0, e, qi]
            csl = csl_ref[e, pl.ds(start, tk), :]
            m_old = m_ref[e]
            ps, alphas, ms = [], [], []
            for c in range(nchunk):
                cols = slice(c * LANES, (c + 1) * LANES)
                w = st_ref[e, :, cols] - csl
                if diagonal:
                    w = jnp.where(key <= qry + c * LANES, w, NEG_INF)
                m_new = jnp.maximum(m_old[:, cols], jnp.max(w, axis=0, keepdims=True) + ct[:, cols])
                ps.append(jnp.exp2(w - (m_new - ct[:, cols])).astype(BF16))
                alphas.append(jnp.exp2(m_old[:, cols] - m_new))
                ms.append(m_new)
            pt = jnp.concatenate(ps, axis=1)
            alpha = jnp.concatenate(alphas, axis=1)
            acc_ref[e] = alpha * acc_ref[e] + jnp.dot(vta, pt, preferred_element_type=F32)
            m_ref[e] = jnp.concatenate(ms, axis=1)

    scores(0, sta_ref)

    def pair(t, carry):
        u = 2 * t
        scores(u + 1, stb_ref)
        softmax_pv(u, sta_ref, False)
        scores(u + 2, sta_ref)
        softmax_pv(u + 1, stb_ref, False)
        return carry

    lax.fori_loop(0, qi // 2, pair, 0)
    odd = qi % 2 == 1

    @pl.when(odd)
    def _():
        scores(qi, stb_ref)
        softmax_pv(qi - 1, sta_ref, False)
        softmax_pv(qi, stb_ref, True)

    @pl.when(jnp.logical_not(odd))
    def _():
        softmax_pv(qi, sta_ref, True)

    dim = lax.broadcasted_iota(jnp.int32, (LANES, tq), 0)
    inv_l = [1.0 / acc_ref[e, LANES:LANES + 1, :] for e in range(2)]
    out_t = jnp.where(dim < HEAD_DIM, acc_ref[0, :LANES, :] * inv_l[0], acc_ref[1, :LANES, :] * inv_l[1])
    o_ref[0] = jnp.transpose(out_t).astype(BF16)


def _fox(qb, kb, vbt, cum, tq):
    bsz, s, _ = qb.shape
    nq = s // tq
    npair = B_W // LANES
    cum5 = cum.reshape(bsz, B_HEADS, nq, 1, tq)
    return pl.pallas_call(
        _fox_kernel,
        grid=(bsz, npair, nq),
        in_specs=[pl.BlockSpec((1, tq, LANES), lambda b, hp, i: (b, i, hp)),
                  pl.BlockSpec((1, s, LANES), lambda b, hp, i: (b, 0, hp)),
                  pl.BlockSpec((1, nq, LANES, tq), lambda b, hp, i: (b, 0, hp, 0)),
                  pl.BlockSpec((1, 2, nq, 1, tq), lambda b, hp, i: (b, hp, 0, 0, 0))],
        out_specs=pl.BlockSpec((1, tq, LANES), lambda b, hp, i: (b, i, hp)),
        out_shape=jax.ShapeDtypeStruct((bsz, s, B_W), BF16),
        scratch_shapes=[pltpu.VMEM((2, 1, tq), F32),
                        pltpu.VMEM((2, LANES + ONES_ROWS, tq), F32),
                        pltpu.VMEM((2, s, LANES), F32),
                        pltpu.VMEM((2, tq, LANES), BF16),
                        pltpu.VMEM((2, tq, tq), F32),
                        pltpu.VMEM((2, tq, tq), F32)],
        compiler_params=_params("parallel", "arbitrary", "arbitrary"),
        name="fox",
    )(qb, kb, vbt, cum5)


def _post_kernel(x_ref, oa_ref, ob_ref, ada_ref, gpre_ref, gpost_ref, wg_ref, wa_ref, wb_ref,
                 wo_ref, o_ref):
    x = x_ref[0]
    d = x.shape[1]
    h = _rms_mod(x, gpre_ref[...], ada_ref[0, 1:2, :], ada_ref[0, 0:1, :]).astype(BF16)
    a = jnp.dot(oa_ref[0], wa_ref[...], preferred_element_type=F32)
    b = jnp.dot(ob_ref[0], wb_ref[...], preferred_element_type=F32)
    ga = jax.nn.sigmoid(jnp.dot(h, wg_ref[:, :d], preferred_element_type=F32))
    gb = jax.nn.sigmoid(jnp.dot(h, wg_ref[:, d:], preferred_element_type=F32))
    merged = (ga * a + gb * b).astype(BF16)
    y = jnp.dot(merged, wo_ref[...], preferred_element_type=F32)
    o_ref[0] = x + ada_ref[0, 2:3, :] * _rms(y, gpost_ref[...])


def _post(x, oa, ob, ada, g_pre, g_post, wg, wa, wb, wo, tm):
    bsz, s, d = x.shape
    tok = lambda w: pl.BlockSpec((1, tm, w), lambda b, i: (b, i, 0))
    const = lambda shape: pl.BlockSpec(shape, lambda b, i: (0,) * len(shape))
    return pl.pallas_call(
        _post_kernel,
        grid=(bsz, s // tm),
        in_specs=[tok(d), tok(A_Q_W), tok(B_W),
                  pl.BlockSpec((1, N_ADA, d), lambda b, i: (b, 0, 0)),
                  const((1, d)), const((1, d)),
                  const(wg.shape), const(wa.shape), const(wb.shape), const(wo.shape)],
        out_specs=tok(d),
        out_shape=jax.ShapeDtypeStruct((bsz, s, d), F32),
        compiler_params=_params("parallel", "parallel"),
        name="post",
    )(x, oa, ob, ada, g_pre, g_post, wg, wa, wb, wo)


def _ffn_kernel(x_ref, ada_ref, gpre_ref, gpost_ref, wi_ref, wo_ref, o_ref, *, chunk):
    x = x_ref[0]
    d_ff = wo_ref.shape[0]
    h = _rms_mod(x, gpre_ref[...], ada_ref[0, 4:5, :], ada_ref[0, 3:4, :]).astype(BF16)
    y = jnp.zeros(x.shape, F32)
    for c in range(d_ff // chunk):
        g = jnp.dot(h, wi_ref[:, c * chunk:(c + 1) * chunk], preferred_element_type=F32)
        u = jnp.dot(h, wi_ref[:, d_ff + c * chunk:d_ff + (c + 1) * chunk], preferred_element_type=F32)
        act = (g * jax.nn.sigmoid(g) * u).astype(BF16)
        y = y + jnp.dot(act, wo_ref[c * chunk:(c + 1) * chunk, :], preferred_element_type=F32)
    o_ref[0] = x + ada_ref[0, 5:6, :] * _rms(y, gpost_ref[...])


def _ffn(x, ada, g_pre, g_post, wi, wo, tm, chunk):
    bsz, s, d = x.shape
    tok = pl.BlockSpec((1, tm, d), lambda b, i: (b, i, 0))
    const = lambda shape: pl.BlockSpec(shape, lambda b, i: (0,) * len(shape))
    return pl.pallas_call(
        functools.partial(_ffn_kernel, chunk=chunk),
        grid=(bsz, s // tm),
        in_specs=[tok, pl.BlockSpec((1, N_ADA, d), lambda b, i: (b, 0, 0)),
                  const((1, d)), const((1, d)), const(wi.shape), const(wo.shape)],
        out_specs=tok,
        out_shape=jax.ShapeDtypeStruct((bsz, s, d), F32),
        compiler_params=_params("parallel", "parallel"),
        name="ffn",
    )(x, ada, g_pre, g_post, wi, wo)


def _pair_heads(w, axis):
    shape = w.shape
    pre, post = shape[:axis], shape[axis + 1:]
    w = w.reshape(pre + (A_KV_HEADS, A_GROUP, HEAD_DIM) + post)
    w = jnp.swapaxes(w, axis, axis + 1)
    return w.reshape(shape)


def kernel(x, c, positions, w_ada, b_ada, g_pre_mix, g_post_mix, w_in, b_f, sinks, w_branch_a,
           w_branch_b, w_out, g_pre_ffn, g_post_ffn, w_ffn_in, w_ffn_out):
    bsz, s, d = x.shape
    depth = w_ada.shape[0]
    tm = min(512, s)
    pos = positions.astype(F32).reshape(bsz, s, 1)
    half = jnp.arange(0, HEAD_DIM, 2, dtype=F32) / HEAD_DIM
    inv_freq = jnp.tile(1.0 / (ROPE_THETA ** half), LANES // (HEAD_DIM // 2)).reshape(1, LANES)
    for l in range(depth):
        w_l = w_in[l]
        o_va, o_qb, o_vb = A_Q_W + A_KV_W, A_Q_W + 2 * A_KV_W, A_Q_W + 2 * A_KV_W + 2 * B_W
        w_qk = jnp.concatenate([_pair_heads(w_l[:, :A_Q_W], 1), w_l[:, A_Q_W:o_va], w_l[:, o_qb:o_vb]],
                               axis=1).astype(BF16)
        w_vt = jnp.concatenate([w_l[:, o_va:o_qb], w_l[:, o_vb:QKV_W]], axis=1).T.astype(BF16)
        w_f = jnp.pad(w_l[:, QKV_W:QKV_W + B_HEADS], ((0, 0), (0, LANES - B_HEADS))).astype(BF16)
        w_g = w_l[:, QKV_W + B_HEADS:].astype(BF16)
        bf_row = jnp.pad(b_f[l], (0, LANES - B_HEADS)).reshape(1, LANES)
        sink_row = jnp.repeat(sinks[l], WINDOW).reshape(1, A_Q_HEADS * WINDOW)
        w_a = _pair_heads(w_branch_a[l], 0).astype(BF16)
        w_b = w_branch_b[l].astype(BF16)
        w_o = w_out[l].astype(BF16)
        w_fi = w_ffn_in[l].astype(BF16)
        w_fo = w_ffn_out[l].astype(BF16)
        row = lambda g: g[l].reshape(1, d)

        ada = _ada(c, w_ada[l], b_ada[l]).reshape(bsz, N_ADA, d)
        qa, ka, vat, qb, kb, vbt, lf = _in_proj(x, pos, ada, row(g_pre_mix), w_qk, w_vt, w_f, bf_row,
                                                inv_freq, tm)
        cum = _cumsum(lf)
        o_a = _swa(qa, ka, vat, sink_row, tm)
        o_b = _fox(qb, kb, vbt, cum, tm)
        x = _post(x, o_a, o_b, ada, row(g_pre_mix), row(g_post_mix), w_g, w_a, w_b, w_o, min(256, s))
        x = _ffn(x, ada, row(g_pre_ffn), row(g_post_ffn), w_fi, w_fo, min(256, s),
                 w_ffn_out.shape[1] // 2)
    return x
```

```python
import functools
import math

import jax
import jax.numpy as jnp
import numpy as np
from jax import lax
from jax.experimental import pallas as pl
from jax.experimental.pallas import tpu as pltpu

HEAD_DIM = 64
WINDOW = 128
A_Q_HEADS = 8
A_KV_HEADS = 2
A_GROUP = A_Q_HEADS // A_KV_HEADS
B_HEADS = 8
N_ADA = 6
ROPE_THETA = 10000.0
RMS_EPS = 1e-6
LANES = 128
MXU_DEPTH = 256
VMEM_LIMIT = 56 * 1024 * 1024

A_Q_W = A_Q_HEADS * HEAD_DIM
A_KV_W = A_KV_HEADS * HEAD_DIM
B_W = B_HEADS * HEAD_DIM
QKV_W = A_Q_W + 2 * A_KV_W + 3 * B_W
BF16 = jnp.bfloat16
F32 = jnp.float32
NEG_INF = float("-inf")
LOG2E = math.log2(math.e)
ONES_ROWS = 16


def _params(*sem):
    return pltpu.CompilerParams(dimension_semantics=sem, vmem_limit_bytes=VMEM_LIMIT)


def _rms_mod(x, g, scale, shift):
    ms = jnp.mean(x * x, axis=-1, keepdims=True)
    return (x * lax.rsqrt(ms + RMS_EPS) * g) * (1.0 + scale) + shift


def _rms(y, g):
    ms = jnp.mean(y * y, axis=-1, keepdims=True)
    return y * lax.rsqrt(ms + RMS_EPS) * g


def _ada_kernel(c_ref, w_ref, b_ref, o_ref):
    o_ref[...] = jnp.dot(c_ref[...], w_ref[...], preferred_element_type=F32) + b_ref[...]


def _ada(c, w, b):
    bsz, d = c.shape
    n = w.shape[1]
    tn = n // 4
    return pl.pallas_call(
        _ada_kernel,
        grid=(n // tn,),
        in_specs=[pl.BlockSpec((bsz, d), lambda j: (0, 0)),
                  pl.BlockSpec((d, tn), lambda j: (0, j)),
                  pl.BlockSpec((1, tn), lambda j: (0, j))],
        out_specs=pl.BlockSpec((bsz, tn), lambda j: (0, j)),
        out_shape=jax.ShapeDtypeStruct((bsz, n), F32),
        compiler_params=_params("arbitrary"),
        name="ada",
    )(c, w, b.reshape(1, n))


def _rope_table_kernel(pos_ref, invf_ref, cos_ref, sin_ref):
    ang = pos_ref[0] * invf_ref[...]
    cos_ref[0] = jnp.cos(ang)
    sin_ref[0] = jnp.sin(ang)


def _rope_tables(pos_c, inv_freq):
    bsz, rows, _ = pos_c.shape
    blk = pl.BlockSpec((1, rows, LANES), lambda b: (b, 0, 0))
    shape = jax.ShapeDtypeStruct((bsz, rows, LANES), F32)
    return pl.pallas_call(
        _rope_table_kernel,
        grid=(bsz,),
        in_specs=[blk, pl.BlockSpec((1, LANES), lambda b: (0, 0))],
        out_specs=[blk, blk],
        out_shape=[shape, shape],
        compiler_params=_params("parallel"),
        name="rope_tables",
    )(pos_c, inv_freq)


AUG_PARTS = 3


def _aug_base(head):
    return head * LANES + (HEAD_DIM if head % 2 == 0 else 0)


def _aug_placement():
    p = np.zeros((LANES, B_HEADS * LANES), np.float32)
    for head in range(B_HEADS):
        for part in range(AUG_PARTS):
            p[part * B_HEADS + head, _aug_base(head) + part] = 1.0
            p[part * B_HEADS + head, _aug_base(head) + AUG_PARTS + part] = -1.0
    return jnp.asarray(p, BF16)


def _inproj_kernel(x_ref, cos_ref, sin_ref, ada_ref, g_ref, w_ref, wvt_ref, bf_ref, place_ref,
                   qa_ref, ka_ref, vat_ref, qb_ref, kb_ref, vbt_ref, carry_ref):
    i = pl.program_id(1)
    sub = vbt_ref.shape[3]
    lane = lax.broadcasted_iota(jnp.int32, (sub, LANES), 1)
    row = lax.broadcasted_iota(jnp.int32, (sub, LANES), 0)
    first_half = (lane % HEAD_DIM) < (HEAD_DIM // 2)
    low = lane < HEAD_DIM
    q_scale = LOG2E / math.sqrt(HEAD_DIM)

    @pl.when(i == 0)
    def _():
        carry_ref[...] = jnp.zeros(carry_ref.shape, F32)

    carry = carry_ref[0:1, :]
    for r in range(x_ref.shape[1] // sub):
        rows = slice(r * sub, (r + 1) * sub)
        h = _rms_mod(x_ref[0, rows, :], g_ref[...], ada_ref[0, 1:2, :], ada_ref[0, 0:1, :]).astype(BF16)

        def proj(lo, width):
            return jnp.dot(h, w_ref[:, lo:lo + width], preferred_element_type=F32)

        def proj_t(lo, width):
            return lax.dot_general(wvt_ref[lo:lo + width, :], h, (((1,), (1,)), ((), ())),
                                   preferred_element_type=F32)

        cos = cos_ref[0, rows, :]
        sin_signed = jnp.where(first_half, -sin_ref[0, rows, :], sin_ref[0, rows, :])

        def rope(t):
            partner = jnp.where(first_half,
                                pltpu.roll(t, LANES - HEAD_DIM // 2, axis=1),
                                pltpu.roll(t, HEAD_DIM // 2, axis=1))
            return t * cos + partner * sin_signed

        kf = proj(A_Q_W + 2 * B_W, A_KV_W + LANES)
        ka_ref[0, rows, :] = rope(kf[:, :A_KV_W]).astype(BF16)
        fl = kf[:, A_KV_W:] + bf_ref[...]
        c = jnp.minimum(fl, 0.0) - jnp.log1p(jnp.exp(-jnp.abs(fl)))
        d = 1
        while d < sub:
            c = c + jnp.where(row >= d, pltpu.roll(c, d, axis=0), 0.0)
            d *= 2
        c = c + carry
        carry = c[sub - 1:sub, :]
        c = c * LOG2E
        hi = c.astype(BF16).astype(F32)
        r1 = c - hi
        mid = r1.astype(BF16).astype(F32)
        lo = r1 - mid
        packed = jnp.where(lane < B_HEADS, hi,
                           jnp.where(lane < 2 * B_HEADS, pltpu.roll(mid, B_HEADS, axis=1),
                                     jnp.where(lane < 3 * B_HEADS, pltpu.roll(lo, 2 * B_HEADS, axis=1), 0.0)))

        qa = proj(0, A_Q_W)
        for j in range(A_Q_W // LANES):
            qa_ref[0, rows, j * LANES:(j + 1) * LANES] = (
                rope(qa[:, j * LANES:(j + 1) * LANES]) * q_scale).astype(BF16)
        vat_ref[0, :, rows] = proj_t(0, A_KV_W).astype(BF16)
        vbt_ref[0, r] = proj_t(A_KV_W, B_W).astype(BF16)
        qb = proj(A_Q_W, B_W) * q_scale
        kb = proj(A_Q_W + B_W, B_W)
        aug = jnp.dot(packed.astype(BF16), place_ref[...], preferred_element_type=F32)
        for head in range(B_HEADS):
            base = _aug_base(head) % LANES
            own = low if head % 2 == 0 else jnp.logical_not(low)
            plus = (lane >= base) & (lane < base + AUG_PARTS)
            minus = (lane >= base + AUG_PARTS) & (lane < base + 2 * AUG_PARTS)
            a = aug[:, head * LANES:(head + 1) * LANES]
            src = slice(head // 2 * LANES, (head // 2 + 1) * LANES)
            group = slice(head * LANES, (head + 1) * LANES)
            qb_ref[0, rows, group] = jnp.where(own, qb[:, src],
                                               jnp.where(plus, a, jnp.where(minus, 1.0, 0.0))).astype(BF16)
            kb_ref[0, rows, group] = jnp.where(own, kb[:, src],
                                               jnp.where(minus, a, jnp.where(plus, 1.0, 0.0))).astype(BF16)
    carry_ref[0:1, :] = carry


def _in_proj(x, cos, sin, ada, g_pre, w_qk, w_vt, b_f, tm, sub):
    bsz, s, d = x.shape
    grid = (bsz, s // tm)
    tok = lambda w: pl.BlockSpec((1, tm, w), lambda b, i: (b, i, 0))
    const = lambda shape: pl.BlockSpec(shape, lambda b, i: (0,) * len(shape))
    act = lambda w: jax.ShapeDtypeStruct((bsz, s, w), BF16)
    place = _aug_placement()
    return pl.pallas_call(
        _inproj_kernel,
        grid=grid,
        in_specs=[tok(d), tok(LANES), tok(LANES),
                  pl.BlockSpec((1, N_ADA, d), lambda b, i: (b, 0, 0)),
                  const((1, d)), _resident(w_qk.shape), _resident(w_vt.shape),
                  const((1, LANES)), const(place.shape)],
        out_specs=[tok(A_Q_W), tok(A_KV_W),
                   pl.BlockSpec((1, A_KV_W, tm), lambda b, i: (b, 0, i)),
                   tok(B_HEADS * LANES), tok(B_HEADS * LANES),
                   pl.BlockSpec((1, tm // sub, B_W, sub), lambda b, i: (b, i, 0, 0))],
        out_shape=[act(A_Q_W), act(A_KV_W),
                   jax.ShapeDtypeStruct((bsz, A_KV_W, s), BF16),
                   act(B_HEADS * LANES), act(B_HEADS * LANES),
                   jax.ShapeDtypeStruct((bsz, s // sub, B_W, sub), BF16)],
        scratch_shapes=[pltpu.VMEM((8, LANES), F32)],
        compiler_params=_params("parallel", "arbitrary"),
        name="in_proj",
    )(x, cos, sin, ada, g_pre, w_qk, w_vt, b_f, place)


def _swa_kernel(q_ref, kp_ref, kc_ref, vtp_ref, vtc_ref, sink_ref, o_ref):
    i = pl.program_id(1)
    tq = q_ref.shape[1]
    nsub = tq // WINDOW
    ncol = A_Q_W // LANES
    lane_q = lax.broadcasted_iota(jnp.int32, (WINDOW, LANES), 1)
    low = lane_q < HEAD_DIM
    low_dim = lax.broadcasted_iota(jnp.int32, (LANES, WINDOW), 0) < HEAD_DIM
    k_loc = lax.broadcasted_iota(jnp.int32, (2 * WINDOW, WINDOW), 0)
    q_loc = lax.broadcasted_iota(jnp.int32, (2 * WINDOW, WINDOW), 1) + WINDOW
    rel = q_loc - k_loc
    band = (rel >= 0) & (rel < WINDOW)
    sink = sink_ref[...] * LOG2E
    ones = jnp.ones((ONES_ROWS, 2 * WINDOW), BF16)
    zero = jnp.zeros((WINDOW, LANES), BF16)
    def keys(j):
        if j == 0:
            return jnp.concatenate([kp_ref[0], kc_ref[0, :WINDOW, :]], axis=0)
        return kc_ref[0, (j - 1) * WINDOW:(j + 1) * WINDOW, :]

    def values_t(j):
        if j == 0:
            return jnp.concatenate([vtp_ref[0], vtc_ref[0, :, :WINDOW]], axis=1)
        return vtc_ref[0, :, (j - 1) * WINDOW:(j + 1) * WINDOW]

    def scores(j):
        q = q_ref[0, j * WINDOW:(j + 1) * WINDOW, :]
        parts = [jnp.where(low, q[:, c * LANES:(c + 1) * LANES], zero) for c in range(ncol)]
        parts += [jnp.where(low, zero, q[:, c * LANES:(c + 1) * LANES]) for c in range(ncol)]
        qs = jnp.concatenate(parts, axis=0)
        return lax.dot_general(keys(j), qs, (((1,), (1,)), ((), ())), preferred_element_type=F32)

    def softmax_pv(j, st):
        valid = band & ((k_loc >= WINDOW) | (i > 0)) if j == 0 else band
        ps, sink_terms = [], []
        for hh in range(A_Q_HEADS):
            cols = slice(hh * WINDOW, (hh + 1) * WINDOW)
            w = jnp.where(valid, st[:, cols], NEG_INF)
            m = jnp.maximum(jnp.max(w, axis=0, keepdims=True), sink[:, cols])
            ps.append(jnp.exp2(w - m).astype(BF16))
            sink_terms.append(jnp.exp2(sink[:, cols] - m))
        pt = jnp.concatenate(ps, axis=1)
        ot = jnp.dot(jnp.concatenate([values_t(j), ones], axis=0), pt, preferred_element_type=F32)
        for c in range(ncol):
            halves = []
            for hh in (c, ncol + c):
                cols = slice(hh * WINDOW, (hh + 1) * WINDOW)
                denom = ot[LANES:LANES + 1, cols] + sink_terms[hh]
                halves.append(ot[:LANES, cols] * (1.0 / denom))
            o_ref[0, j * WINDOW:(j + 1) * WINDOW, c * LANES:(c + 1) * LANES] = (
                jnp.transpose(jnp.where(low_dim, halves[0], halves[1])).astype(BF16))

    st = scores(0)
    for j in range(nsub):
        st_next = scores(j + 1) if j + 1 < nsub else None
        softmax_pv(j, st)
        st = st_next


def _swa(qa, ka, vat, sinks, tq):
    bsz, s, _ = qa.shape
    r = tq // WINDOW
    cur = lambda b, i: (b, i, 0)
    return pl.pallas_call(
        _swa_kernel,
        grid=(bsz, s // tq),
        in_specs=[pl.BlockSpec((1, tq, A_Q_W), cur),
                  pl.BlockSpec((1, WINDOW, A_KV_W), lambda b, i: (b, jnp.maximum(i * r - 1, 0), 0)),
                  pl.BlockSpec((1, tq, A_KV_W), cur),
                  pl.BlockSpec((1, A_KV_W, WINDOW), lambda b, i: (b, 0, jnp.maximum(i * r - 1, 0))),
                  pl.BlockSpec((1, A_KV_W, tq), lambda b, i: (b, 0, i)),
                  pl.BlockSpec((1, A_Q_HEADS * WINDOW), lambda b, i: (0, 0))],
        out_specs=pl.BlockSpec((1, tq, A_Q_W), cur),
        out_shape=jax.ShapeDtypeStruct((bsz, s, A_Q_W), BF16),
        compiler_params=_params("parallel", "parallel"),
        name="swa",
    )(qa, ka, ka, vat, vat, sinks)


def _fox_kernel(q_ref, k_ref, vt_ref, o_ref, m_ref, acc_ref, sta_ref, stb_ref):
    qi = pl.program_id(2)
    tq = q_ref.shape[1]
    tk = vt_ref.shape[3]
    nchunk = tq // LANES
    m_ref[...] = jnp.full(m_ref.shape, NEG_INF, F32)
    acc_ref[...] = jnp.zeros(acc_ref.shape, F32)
    key = lax.broadcasted_iota(jnp.int32, (tk, LANES), 0)
    qry = lax.broadcasted_iota(jnp.int32, (tk, LANES), 1)
    ones = jnp.ones((ONES_ROWS, tk), BF16)

    def scores(kj, st_ref):
        start = pl.multiple_of(kj * tk, tk)
        for e in range(2):
            group = slice(e * LANES, (e + 1) * LANES)
            st_ref[e] = lax.dot_general(k_ref[0, pl.ds(start, tk), group], q_ref[0, :, group],
                                        (((1,), (1,)), ((), ())), preferred_element_type=F32)

    def softmax_pv(kj, st_ref, diagonal):
        for e in range(2):
            vta = jnp.concatenate([vt_ref[0, kj, e * HEAD_DIM:(e + 1) * HEAD_DIM, :], ones], axis=0)
            m_old = m_ref[e]
            ps, alphas, ms = [], [], []
            for c in range(nchunk):
                cols = slice(c * LANES, (c + 1) * LANES)
                z = st_ref[e, :, cols]
                if diagonal:
                    z = jnp.where(key <= qry + c * LANES, z, NEG_INF)
                m_new = jnp.maximum(m_old[:, cols], jnp.max(z, axis=0, keepdims=True))
                ps.append(jnp.exp2(z - m_new).astype(BF16))
                alphas.append(jnp.exp2(m_old[:, cols] - m_new))
                ms.append(m_new)
            pt = jnp.concatenate(ps, axis=1)
            alpha = jnp.concatenate(alphas, axis=1)
            acc_ref[e] = alpha * acc_ref[e] + jnp.dot(vta, pt, preferred_element_type=F32)
            m_ref[e] = jnp.concatenate(ms, axis=1)

    scores(0, sta_ref)

    def pair(t, carry):
        u = 2 * t
        scores(u + 1, stb_ref)
        softmax_pv(u, sta_ref, False)
        scores(u + 2, sta_ref)
        softmax_pv(u + 1, stb_ref, False)
        return carry

    lax.fori_loop(0, qi // 2, pair, 0)
    odd = qi % 2 == 1

    @pl.when(odd)
    def _():
        scores(qi, stb_ref)
        softmax_pv(qi - 1, sta_ref, False)
        softmax_pv(qi, stb_ref, True)

    @pl.when(jnp.logical_not(odd))
    def _():
        softmax_pv(qi, sta_ref, True)

    out_t = jnp.concatenate(
        [acc_ref[e, :HEAD_DIM, :] * (1.0 / acc_ref[e, HEAD_DIM:HEAD_DIM + 1, :]) for e in range(2)], axis=0)
    o_ref[0] = jnp.transpose(out_t).astype(BF16)


def _fox(qb, kb, vbt, tq):
    bsz, s, _ = qb.shape
    nq = s // tq
    npair = B_HEADS // 2
    return pl.pallas_call(
        _fox_kernel,
        grid=(bsz, npair, nq),
        in_specs=[pl.BlockSpec((1, tq, 2 * LANES), lambda b, hp, i: (b, i, hp)),
                  pl.BlockSpec((1, s, 2 * LANES), lambda b, hp, i: (b, 0, hp)),
                  pl.BlockSpec((1, nq, 2 * HEAD_DIM, tq), lambda b, hp, i: (b, 0, hp, 0))],
        out_specs=pl.BlockSpec((1, tq, 2 * HEAD_DIM), lambda b, hp, i: (b, i, hp)),
        out_shape=jax.ShapeDtypeStruct((bsz, s, B_W), BF16),
        scratch_shapes=[pltpu.VMEM((2, 1, tq), F32),
                        pltpu.VMEM((2, HEAD_DIM + ONES_ROWS, tq), F32),
                        pltpu.VMEM((2, tq, tq), F32),
                        pltpu.VMEM((2, tq, tq), F32)],
        compiler_params=_params("parallel", "arbitrary", "arbitrary"),
        name="fox",
    )(qb, kb, vbt)


def _resident(shape):
    return pl.BlockSpec(shape, lambda b, i: (0,) * len(shape), pipeline_mode=pl.Buffered(1))


def _post_kernel(x_ref, oa_ref, ob_ref, ada_ref, gpre_ref, gpost_ref, wg_ref, wa_ref, wb_ref,
                 wo_ref, o_ref, *, sub):
    d = x_ref.shape[2]
    for r in range(x_ref.shape[1] // sub):
        rows = slice(r * sub, (r + 1) * sub)
        x = x_ref[0, rows, :]
        h = _rms_mod(x, gpre_ref[...], ada_ref[0, 1:2, :], ada_ref[0, 0:1, :]).astype(BF16)
        a = jnp.dot(oa_ref[0, rows, :], wa_ref[...], preferred_element_type=F32)
        b = jnp.dot(ob_ref[0, rows, :], wb_ref[...], preferred_element_type=F32)
        ga = jax.nn.sigmoid(jnp.dot(h, wg_ref[:, :d], preferred_element_type=F32))
        gb = jax.nn.sigmoid(jnp.dot(h, wg_ref[:, d:], preferred_element_type=F32))
        merged = (ga * a + gb * b).astype(BF16)
        y = jnp.dot(merged, wo_ref[...], preferred_element_type=F32)
        o_ref[0, rows, :] = x + ada_ref[0, 2:3, :] * _rms(y, gpost_ref[...])


def _post(x, oa, ob, ada, g_pre, g_post, wg, wa, wb, wo, tm, sub):
    bsz, s, d = x.shape
    tok = lambda w: pl.BlockSpec((1, tm, w), lambda b, i: (b, i, 0))
    const = lambda shape: pl.BlockSpec(shape, lambda b, i: (0,) * len(shape))
    return pl.pallas_call(
        functools.partial(_post_kernel, sub=sub),
        grid=(bsz, s // tm),
        in_specs=[tok(d), tok(A_Q_W), tok(B_W),
                  pl.BlockSpec((1, N_ADA, d), lambda b, i: (b, 0, 0)),
                  const((1, d)), const((1, d)),
                  _resident(wg.shape), _resident(wa.shape), _resident(wb.shape), _resident(wo.shape)],
        out_specs=tok(d),
        out_shape=jax.ShapeDtypeStruct((bsz, s, d), F32),
        compiler_params=_params("parallel", "parallel"),
        name="post",
    )(x, oa, ob, ada, g_pre, g_post, wg, wa, wb, wo)


def _ffn_kernel(x_ref, ada_ref, gpre_ref, gpost_ref, wi_ref, wo_ref, o_ref, *, sub, chunks):
    d_ff = wo_ref.shape[0]
    for r in range(x_ref.shape[1] // sub):
        rows = slice(r * sub, (r + 1) * sub)
        x = x_ref[0, rows, :]
        h = _rms_mod(x, gpre_ref[...], ada_ref[0, 4:5, :], ada_ref[0, 3:4, :]).astype(BF16)
        y = jnp.zeros(x.shape, F32)
        lo = 0
        for width in chunks:
            g = jnp.dot(h, wi_ref[:, lo:lo + width], preferred_element_type=F32)
            u = jnp.dot(h, wi_ref[:, d_ff + lo:d_ff + lo + width], preferred_element_type=F32)
            act = (g * jax.nn.sigmoid(g) * u).astype(BF16)
            y = y + jnp.dot(act, wo_ref[lo:lo + width, :], preferred_element_type=F32)
            lo += width
        o_ref[0, rows, :] = x + ada_ref[0, 5:6, :] * _rms(y, gpost_ref[...])


def _ffn(x, ada, g_pre, g_post, wi, wo, tm, sub, chunks):
    bsz, s, d = x.shape
    tok = pl.BlockSpec((1, tm, d), lambda b, i: (b, i, 0))
    const = lambda shape: pl.BlockSpec(shape, lambda b, i: (0,) * len(shape))
    return pl.pallas_call(
        functools.partial(_ffn_kernel, sub=sub, chunks=chunks),
        grid=(bsz, s // tm),
        in_specs=[tok, pl.BlockSpec((1, N_ADA, d), lambda b, i: (b, 0, 0)),
                  const((1, d)), const((1, d)), _resident(wi.shape), _resident(wo.shape)],
        out_specs=tok,
        out_shape=jax.ShapeDtypeStruct((bsz, s, d), F32),
        compiler_params=_params("parallel", "parallel"),
        name="ffn",
    )(x, ada, g_pre, g_post, wi, wo)


def _pair_heads(w, axis):
    shape = w.shape
    pre, post = shape[:axis], shape[axis + 1:]
    w = w.reshape(pre + (A_KV_HEADS, A_GROUP, HEAD_DIM) + post)
    w = jnp.swapaxes(w, axis, axis + 1)
    return w.reshape(shape)


def kernel(x, c, positions, w_ada, b_ada, g_pre_mix, g_post_mix, w_in, b_f, sinks, w_branch_a,
           w_branch_b, w_out, g_pre_ffn, g_post_ffn, w_ffn_in, w_ffn_out):
    bsz, s, d = x.shape
    depth = w_ada.shape[0]
    tm = min(1024, s)
    sub = min(512, s)
    n_freq = HEAD_DIM // 2
    rep = LANES // n_freq
    half = jnp.arange(0, HEAD_DIM, 2, dtype=F32) / HEAD_DIM
    inv_freq = jnp.tile(1.0 / (ROPE_THETA ** half), rep).reshape(1, LANES)
    pos_c = jnp.repeat(positions.astype(F32).reshape(bsz, s // rep, rep), n_freq, axis=2)
    cos_c, sin_c = _rope_tables(pos_c, inv_freq)
    cos = jnp.tile(cos_c.reshape(bsz, s, n_freq), (1, 1, rep))
    sin = jnp.tile(sin_c.reshape(bsz, s, n_freq), (1, 1, rep))
    for l in range(depth):
        w_l = w_in[l]
        o_va, o_qb, o_vb = A_Q_W + A_KV_W, A_Q_W + 2 * A_KV_W, A_Q_W + 2 * A_KV_W + 2 * B_W
        w_f = jnp.pad(w_l[:, QKV_W:QKV_W + B_HEADS], ((0, 0), (0, LANES - B_HEADS)))
        w_qk = jnp.concatenate([_pair_heads(w_l[:, :A_Q_W], 1), w_l[:, o_qb:o_vb], w_l[:, A_Q_W:o_va], w_f],
                               axis=1).astype(BF16)
        w_vt = jnp.concatenate([w_l[:, o_va:o_qb], w_l[:, o_vb:QKV_W]], axis=1).T.astype(BF16)
        w_g = w_l[:, QKV_W + B_HEADS:].astype(BF16)
        bf_row = jnp.pad(b_f[l], (0, LANES - B_HEADS)).reshape(1, LANES)
        sink_row = jnp.repeat(sinks[l], WINDOW).reshape(1, A_Q_HEADS * WINDOW)
        w_a = _pair_heads(w_branch_a[l], 0).astype(BF16)
        w_b = w_branch_b[l].astype(BF16)
        w_o = w_out[l].astype(BF16)
        w_fi = w_ffn_in[l].astype(BF16)
        w_fo = w_ffn_out[l].astype(BF16)
        row = lambda g: g[l].reshape(1, d)

        ada = _ada(c, w_ada[l], b_ada[l]).reshape(bsz, N_ADA, d)
        qa, ka, vat, qb, kb, vbt = _in_proj(x, cos, sin, ada, row(g_pre_mix), w_qk, w_vt, bf_row, tm, sub)
        o_a = _swa(qa, ka, vat, sink_row, sub)
        o_b = _fox(qb, kb, vbt, sub)
        x = _post(x, o_a, o_b, ada, row(g_pre_mix), row(g_post_mix), w_g, w_a, w_b, w_o, tm, sub)
        n_tiles = w_fo.shape[0] // MXU_DEPTH
        chunks = ((n_tiles + 1) // 2 * MXU_DEPTH, n_tiles // 2 * MXU_DEPTH)
        x = _ffn(x, ada, row(g_pre_ffn), row(g_post_ffn), w_fi, w_fo, tm, sub, chunks)
    return x
```

```python
import functools
import math

import jax
import jax.numpy as jnp
import numpy as np
from jax import lax
from jax.experimental import pallas as pl
from jax.experimental.pallas import tpu as pltpu

HEAD_DIM = 64
WINDOW = 128
A_Q_HEADS = 8
A_KV_HEADS = 2
A_GROUP = A_Q_HEADS // A_KV_HEADS
B_HEADS = 8
N_ADA = 6
ROPE_THETA = 10000.0
RMS_EPS = 1e-6
LANES = 128
MXU_DEPTH = 256
ROPE_FREQS = HEAD_DIM // 2
VMEM_LIMIT = 56 * 1024 * 1024

A_Q_W = A_Q_HEADS * HEAD_DIM
A_KV_W = A_KV_HEADS * HEAD_DIM
B_W = B_HEADS * HEAD_DIM
QKV_W = A_Q_W + 2 * A_KV_W + 3 * B_W
BF16 = jnp.bfloat16
F32 = jnp.float32
NEG_INF = float("-inf")
LOG2E = math.log2(math.e)
ONES_ROWS = 16


def _params(*sem, flags=None):
    return pltpu.CompilerParams(dimension_semantics=sem, vmem_limit_bytes=VMEM_LIMIT, flags=flags)


def _rms_mod(x, g, scale, shift):
    ms = jnp.mean(x * x, axis=-1, keepdims=True)
    return (x * lax.rsqrt(ms + RMS_EPS)) * (g * (1.0 + scale)) + shift


def _rms(y, g):
    ms = jnp.mean(y * y, axis=-1, keepdims=True)
    return y * lax.rsqrt(ms + RMS_EPS) * g


def _ada_kernel(c_ref, w_ref, b_ref, o_ref):
    o_ref[...] = jnp.dot(c_ref[...], w_ref[...], preferred_element_type=F32) + b_ref[...]


def _ada(c, w, b):
    bsz, d = c.shape
    n = w.shape[1]
    tn = n // 4
    return pl.pallas_call(
        _ada_kernel,
        grid=(n // tn,),
        in_specs=[pl.BlockSpec((bsz, d), lambda j: (0, 0)),
                  pl.BlockSpec((d, tn), lambda j: (0, j)),
                  pl.BlockSpec((1, tn), lambda j: (0, j))],
        out_specs=pl.BlockSpec((bsz, tn), lambda j: (0, j)),
        out_shape=jax.ShapeDtypeStruct((bsz, n), F32),
        compiler_params=_params("arbitrary"),
        name="ada",
    )(c, w, b.reshape(1, n))


def _rope_table_kernel(pos_ref, invf_ref, cos_ref, sin_ref):
    ang = pos_ref[0] * invf_ref[...]
    cos_ref[0] = jnp.cos(ang)
    sin_ref[0] = jnp.sin(ang)


def _rope_tables(pos_c, inv_freq):
    bsz, rows, _ = pos_c.shape
    blk = pl.BlockSpec((1, rows, LANES), lambda b: (b, 0, 0))
    shape = jax.ShapeDtypeStruct((bsz, rows, LANES), F32)
    return pl.pallas_call(
        _rope_table_kernel,
        grid=(bsz,),
        in_specs=[blk, pl.BlockSpec((1, LANES), lambda b: (0, 0))],
        out_specs=[blk, blk],
        out_shape=[shape, shape],
        compiler_params=_params("parallel"),
        name="rope_tables",
    )(pos_c, inv_freq)


AUG_PARTS = 3


def _aug_base(head):
    return head * LANES + (HEAD_DIM if head % 2 == 0 else 0)


def _aug_placement():
    p = np.zeros((LANES, B_HEADS * LANES), np.float32)
    for head in range(B_HEADS):
        for part in range(AUG_PARTS):
            p[part * B_HEADS + head, _aug_base(head) + part] = 1.0
            p[part * B_HEADS + head, _aug_base(head) + AUG_PARTS + part] = -1.0
    return jnp.asarray(p, BF16)


def _inproj_kernel(x_ref, cos_ref, sin_ref, ada_ref, g_ref, w_ref, wvt_ref, bf_ref, place_ref,
                   qa_ref, ka_ref, vat_ref, qb_ref, kb_ref, vbt_ref, carry_ref):
    i = pl.program_id(1)
    sub = vbt_ref.shape[3]
    lane = lax.broadcasted_iota(jnp.int32, (sub, LANES), 1)
    row = lax.broadcasted_iota(jnp.int32, (sub, LANES), 0)
    first_half = (lane % HEAD_DIM) < (HEAD_DIM // 2)
    low = lane < HEAD_DIM
    q_scale = LOG2E / math.sqrt(HEAD_DIM)

    @pl.when(i == 0)
    def _():
        carry_ref[...] = jnp.zeros(carry_ref.shape, F32)

    carry = carry_ref[0:1, :]
    for r in range(x_ref.shape[1] // sub):
        rows = slice(r * sub, (r + 1) * sub)
        h = _rms_mod(x_ref[0, rows, :], g_ref[...], ada_ref[0, 1:2, :], ada_ref[0, 0:1, :]).astype(BF16)

        def proj(lo, width):
            return jnp.dot(h, w_ref[:, lo:lo + width], preferred_element_type=F32)

        def proj_t(lo, width):
            return lax.dot_general(wvt_ref[lo:lo + width, :], h, (((1,), (1,)), ((), ())),
                                   preferred_element_type=F32)

        t0 = i * x_ref.shape[1] + r * sub
        quarter = t0 // cos_ref.shape[1]
        t_rows = pl.ds(pl.multiple_of(t0 - quarter * cos_ref.shape[1], sub), sub)
        shift = (LANES - quarter * ROPE_FREQS) % LANES

        def spread(tab):
            y = jnp.where(lane < ROPE_FREQS, pltpu.roll(tab, shift, axis=1), 0.0)
            y = y + pltpu.roll(y, ROPE_FREQS, axis=1)
            return y + pltpu.roll(y, 2 * ROPE_FREQS, axis=1)

        cos = spread(cos_ref[0, t_rows, :])
        sin = spread(sin_ref[0, t_rows, :])
        sin_signed = jnp.where(first_half, -sin, sin)

        def rope(t):
            partner = jnp.where(first_half,
                                pltpu.roll(t, LANES - HEAD_DIM // 2, axis=1),
                                pltpu.roll(t, HEAD_DIM // 2, axis=1))
            return t * cos + partner * sin_signed

        kf = proj(A_Q_W + 2 * B_W, A_KV_W + LANES)
        ka_ref[0, rows, :] = rope(kf[:, :A_KV_W]).astype(BF16)
        fl = kf[:, A_KV_W:] + bf_ref[...]
        c = jnp.minimum(fl, 0.0) - jnp.log1p(jnp.exp(-jnp.abs(fl)))
        d = 1
        while d < sub:
            c = c + jnp.where(row >= d, pltpu.roll(c, d, axis=0), 0.0)
            d *= 2
        c = c + carry
        carry = c[sub - 1:sub, :]
        c = c * LOG2E
        hi = c.astype(BF16).astype(F32)
        r1 = c - hi
        mid = r1.astype(BF16).astype(F32)
        lo = r1 - mid
        packed = jnp.where(lane < B_HEADS, hi,
                           jnp.where(lane < 2 * B_HEADS, pltpu.roll(mid, B_HEADS, axis=1),
                                     jnp.where(lane < 3 * B_HEADS, pltpu.roll(lo, 2 * B_HEADS, axis=1), 0.0)))

        qa = proj(0, A_Q_W)
        for j in range(A_Q_W // LANES):
            qa_ref[0, rows, j * LANES:(j + 1) * LANES] = (
                rope(qa[:, j * LANES:(j + 1) * LANES]) * q_scale).astype(BF16)
        vat_ref[0, :, rows] = proj_t(0, A_KV_W).astype(BF16)
        vbt_ref[0, r] = proj_t(A_KV_W, B_W).astype(BF16)
        qb = proj(A_Q_W, B_W) * q_scale
        kb = proj(A_Q_W + B_W, B_W)
        aug = jnp.dot(packed.astype(BF16), place_ref[...], preferred_element_type=F32)
        for head in range(B_HEADS):
            base = _aug_base(head) % LANES
            own = low if head % 2 == 0 else jnp.logical_not(low)
            plus = (lane >= base) & (lane < base + AUG_PARTS)
            minus = (lane >= base + AUG_PARTS) & (lane < base + 2 * AUG_PARTS)
            a = aug[:, head * LANES:(head + 1) * LANES]
            src = slice(head // 2 * LANES, (head // 2 + 1) * LANES)
            group = slice(head * LANES, (head + 1) * LANES)
            qb_ref[0, rows, group] = jnp.where(own, qb[:, src],
                                               jnp.where(plus, a, jnp.where(minus, 1.0, 0.0))).astype(BF16)
            kb_ref[0, rows, group] = jnp.where(own, kb[:, src],
                                               jnp.where(minus, a, jnp.where(plus, 1.0, 0.0))).astype(BF16)
    carry_ref[0:1, :] = carry


def _in_proj(x, cos, sin, ada, g_pre, w_qk, w_vt, b_f, tm, sub):
    bsz, s, d = x.shape
    grid = (bsz, s // tm)
    tok = lambda w: pl.BlockSpec((1, tm, w), lambda b, i: (b, i, 0))
    const = lambda shape: pl.BlockSpec(shape, lambda b, i: (0,) * len(shape))
    act = lambda w: jax.ShapeDtypeStruct((bsz, s, w), BF16)
    place = _aug_placement()
    return pl.pallas_call(
        _inproj_kernel,
        grid=grid,
        in_specs=[tok(d), pl.BlockSpec((1,) + cos.shape[1:], lambda b, i: (b, 0, 0)),
                  pl.BlockSpec((1,) + sin.shape[1:], lambda b, i: (b, 0, 0)),
                  pl.BlockSpec((1, N_ADA, d), lambda b, i: (b, 0, 0)),
                  const((1, d)), _resident(w_qk.shape), _resident(w_vt.shape),
                  const((1, LANES)), const(place.shape)],
        out_specs=[tok(A_Q_W), tok(A_KV_W),
                   pl.BlockSpec((1, A_KV_W, tm), lambda b, i: (b, 0, i)),
                   tok(B_HEADS * LANES), tok(B_HEADS * LANES),
                   pl.BlockSpec((1, tm // sub, B_W, sub), lambda b, i: (b, i, 0, 0))],
        out_shape=[act(A_Q_W), act(A_KV_W),
                   jax.ShapeDtypeStruct((bsz, A_KV_W, s), BF16),
                   act(B_HEADS * LANES), act(B_HEADS * LANES),
                   jax.ShapeDtypeStruct((bsz, s // sub, B_W, sub), BF16)],
        scratch_shapes=[pltpu.VMEM((8, LANES), F32)],
        compiler_params=_params("parallel", "arbitrary"),
        name="in_proj",
    )(x, cos, sin, ada, g_pre, w_qk, w_vt, b_f, place)


def _swa_kernel(q_ref, kp_ref, kc_ref, vtp_ref, vtc_ref, sink_ref, o_ref):
    i = pl.program_id(1)
    tq = q_ref.shape[1]
    nsub = tq // WINDOW
    ncol = A_Q_W // LANES
    lane_q = lax.broadcasted_iota(jnp.int32, (WINDOW, LANES), 1)
    low = lane_q < HEAD_DIM
    low_dim = lax.broadcasted_iota(jnp.int32, (LANES, WINDOW), 0) < HEAD_DIM
    k_loc = lax.broadcasted_iota(jnp.int32, (2 * WINDOW, WINDOW), 0)
    q_loc = lax.broadcasted_iota(jnp.int32, (2 * WINDOW, WINDOW), 1) + WINDOW
    rel = q_loc - k_loc
    band = (rel >= 0) & (rel < WINDOW)
    sink = sink_ref[...] * LOG2E
    ones = jnp.ones((ONES_ROWS, 2 * WINDOW), BF16)
    zero = jnp.zeros((WINDOW, LANES), BF16)

    def keys(j):
        if j == 0:
            return jnp.concatenate([kp_ref[0], kc_ref[0, :WINDOW, :]], axis=0)
        return kc_ref[0, (j - 1) * WINDOW:(j + 1) * WINDOW, :]

    def values_t(j):
        if j == 0:
            return jnp.concatenate([vtp_ref[0], vtc_ref[0, :, :WINDOW]], axis=1)
        return vtc_ref[0, :, (j - 1) * WINDOW:(j + 1) * WINDOW]

    def scores(j):
        q = q_ref[0, j * WINDOW:(j + 1) * WINDOW, :]
        parts = [jnp.where(low, q[:, c * LANES:(c + 1) * LANES], zero) for c in range(ncol)]
        parts += [jnp.where(low, zero, q[:, c * LANES:(c + 1) * LANES]) for c in range(ncol)]
        qs = jnp.concatenate(parts, axis=0)
        return lax.dot_general(keys(j), qs, (((1,), (1,)), ((), ())), preferred_element_type=F32)

    def softmax_pv(j, st):
        valid = band & ((k_loc >= WINDOW) | (i > 0)) if j == 0 else band
        ps, sink_terms = [], []
        for hh in range(A_Q_HEADS):
            cols = slice(hh * WINDOW, (hh + 1) * WINDOW)
            w = jnp.where(valid, st[:, cols], NEG_INF)
            m = jnp.maximum(jnp.max(w, axis=0, keepdims=True), sink[:, cols])
            ps.append(jnp.exp2(w - m).astype(BF16))
            sink_terms.append(jnp.exp2(sink[:, cols] - m))
        pt = jnp.concatenate(ps, axis=1)
        ot = jnp.dot(jnp.concatenate([values_t(j), ones], axis=0), pt, preferred_element_type=F32)
        for c in range(ncol):
            halves = []
            for hh in (c, ncol + c):
                cols = slice(hh * WINDOW, (hh + 1) * WINDOW)
                denom = ot[LANES:LANES + 1, cols] + sink_terms[hh]
                halves.append(ot[:LANES, cols] * (1.0 / denom))
            o_ref[0, j * WINDOW:(j + 1) * WINDOW, c * LANES:(c + 1) * LANES] = (
                jnp.transpose(jnp.where(low_dim, halves[0], halves[1])).astype(BF16))

    st = scores(0)
    for j in range(nsub):
        st_next = scores(j + 1) if j + 1 < nsub else None
        softmax_pv(j, st)
        st = st_next


def _swa(qa, ka, vat, sinks, tq):
    bsz, s, _ = qa.shape
    r = tq // WINDOW
    cur = lambda b, i: (b, i, 0)
    return pl.pallas_call(
        _swa_kernel,
        grid=(bsz, s // tq),
        in_specs=[pl.BlockSpec((1, tq, A_Q_W), cur),
                  pl.BlockSpec((1, WINDOW, A_KV_W), lambda b, i: (b, jnp.maximum(i * r - 1, 0), 0)),
                  pl.BlockSpec((1, tq, A_KV_W), cur),
                  pl.BlockSpec((1, A_KV_W, WINDOW), lambda b, i: (b, 0, jnp.maximum(i * r - 1, 0))),
                  pl.BlockSpec((1, A_KV_W, tq), lambda b, i: (b, 0, i)),
                  pl.BlockSpec((1, A_Q_HEADS * WINDOW), lambda b, i: (0, 0))],
        out_specs=pl.BlockSpec((1, tq, A_Q_W), cur),
        out_shape=jax.ShapeDtypeStruct((bsz, s, A_Q_W), BF16),
        compiler_params=_params("parallel", "parallel"),
        name="swa",
    )(qa, ka, ka, vat, vat, sinks)


def _fox_kernel(q_ref, k_ref, vt_ref, o_ref, m_ref, acc_ref, sta_ref, stb_ref):
    qi = pl.program_id(2)
    tq = q_ref.shape[1]
    tk = vt_ref.shape[3]
    nchunk = tq // LANES
    m_ref[...] = jnp.full(m_ref.shape, NEG_INF, F32)
    acc_ref[...] = jnp.zeros(acc_ref.shape, F32)
    ones = jnp.ones((ONES_ROWS, tk), BF16)

    def scores(kj, st_ref):
        start = pl.multiple_of(kj * tk, tk)
        for e in range(2):
            group = slice(e * LANES, (e + 1) * LANES)
            st_ref[e] = lax.dot_general(k_ref[0, pl.ds(start, tk), group], q_ref[0, :, group],
                                        (((1,), (1,)), ((), ())), preferred_element_type=F32)

    def softmax_pv(kj, st_ref, diagonal):
        for e in range(2):
            vta = jnp.concatenate([vt_ref[0, kj, e * HEAD_DIM:(e + 1) * HEAD_DIM, :], ones], axis=0)
            m_old = m_ref[e]
            ps, alphas, ms = [], [], []
            for c in range(nchunk):
                cols = slice(c * LANES, (c + 1) * LANES)
                live = (c + 1) * LANES if diagonal else tk
                z = st_ref[e, :live, cols]
                if diagonal:
                    causal = (lax.broadcasted_iota(jnp.int32, (live, LANES), 0)
                              <= lax.broadcasted_iota(jnp.int32, (live, LANES), 1) + c * LANES)
                    z = jnp.where(causal, z, NEG_INF)
                m_new = jnp.maximum(m_old[:, cols], jnp.max(z, axis=0, keepdims=True))
                p = jnp.exp2(z - m_new).astype(BF16)
                if live < tk:
                    p = jnp.concatenate([p, jnp.zeros((tk - live, LANES), BF16)], axis=0)
                ps.append(p)
                alphas.append(jnp.exp2(m_old[:, cols] - m_new))
                ms.append(m_new)
            pt = jnp.concatenate(ps, axis=1)
            alpha = jnp.concatenate(alphas, axis=1)
            acc_ref[e] = alpha * acc_ref[e] + jnp.dot(vta, pt, preferred_element_type=F32)
            m_ref[e] = jnp.concatenate(ms, axis=1)

    scores(0, sta_ref)

    def pair(t, carry):
        u = 2 * t
        scores(u + 1, stb_ref)
        softmax_pv(u, sta_ref, False)
        scores(u + 2, sta_ref)
        softmax_pv(u + 1, stb_ref, False)
        return carry

    lax.fori_loop(0, qi // 2, pair, 0)
    odd = qi % 2 == 1

    @pl.when(odd)
    def _():
        scores(qi, stb_ref)
        softmax_pv(qi - 1, sta_ref, False)
        softmax_pv(qi, stb_ref, True)

    @pl.when(jnp.logical_not(odd))
    def _():
        softmax_pv(qi, sta_ref, True)

    out_t = jnp.concatenate(
        [acc_ref[e, :HEAD_DIM, :] * (1.0 / acc_ref[e, HEAD_DIM:HEAD_DIM + 1, :]) for e in range(2)], axis=0)
    o_ref[0] = jnp.transpose(out_t).astype(BF16)


def _fox(qb, kb, vbt, tq):
    bsz, s, _ = qb.shape
    nq = s // tq
    npair = B_HEADS // 2
    return pl.pallas_call(
        _fox_kernel,
        grid=(bsz, npair, nq),
        in_specs=[pl.BlockSpec((1, tq, 2 * LANES), lambda b, hp, i: (b, i, hp)),
                  pl.BlockSpec((1, s, 2 * LANES), lambda b, hp, i: (b, 0, hp)),
                  pl.BlockSpec((1, nq, 2 * HEAD_DIM, tq), lambda b, hp, i: (b, 0, hp, 0))],
        out_specs=pl.BlockSpec((1, tq, 2 * HEAD_DIM), lambda b, hp, i: (b, i, hp)),
        out_shape=jax.ShapeDtypeStruct((bsz, s, B_W), BF16),
        scratch_shapes=[pltpu.VMEM((2, 1, tq), F32),
                        pltpu.VMEM((2, HEAD_DIM + ONES_ROWS, tq), F32),
                        pltpu.VMEM((2, tq, tq), F32),
                        pltpu.VMEM((2, tq, tq), F32)],
        compiler_params=_params("parallel", "arbitrary", "arbitrary"),
        name="fox",
    )(qb, kb, vbt)


def _resident(shape):
    return pl.BlockSpec(shape, lambda b, i: (0,) * len(shape), pipeline_mode=pl.Buffered(1))


def _post_kernel(x_ref, oa_ref, ob_ref, ada_ref, gpre_ref, gpost_ref, wg_ref, wa_ref, wb_ref,
                 wo_ref, o_ref, *, sub):
    d = x_ref.shape[2]
    for r in range(x_ref.shape[1] // sub):
        rows = slice(r * sub, (r + 1) * sub)
        x = x_ref[0, rows, :]
        h = _rms_mod(x, gpre_ref[...], ada_ref[0, 1:2, :], ada_ref[0, 0:1, :]).astype(BF16)
        a = jnp.dot(oa_ref[0, rows, :], wa_ref[...], preferred_element_type=F32)
        b = jnp.dot(ob_ref[0, rows, :], wb_ref[...], preferred_element_type=F32)
        ga = jax.nn.sigmoid(jnp.dot(h, wg_ref[:, :d], preferred_element_type=F32))
        gb = jax.nn.sigmoid(jnp.dot(h, wg_ref[:, d:], preferred_element_type=F32))
        merged = (ga * a + gb * b).astype(BF16)
        y = jnp.dot(merged, wo_ref[...], preferred_element_type=F32)
        o_ref[0, rows, :] = x + ada_ref[0, 2:3, :] * _rms(y, gpost_ref[...])


def _post(x, oa, ob, ada, g_pre, g_post, wg, wa, wb, wo, tm, sub):
    bsz, s, d = x.shape
    tok = lambda w: pl.BlockSpec((1, tm, w), lambda b, i: (b, i, 0))
    const = lambda shape: pl.BlockSpec(shape, lambda b, i: (0,) * len(shape))
    return pl.pallas_call(
        functools.partial(_post_kernel, sub=sub),
        grid=(bsz, s // tm),
        in_specs=[tok(d), tok(A_Q_W), tok(B_W),
                  pl.BlockSpec((1, N_ADA, d), lambda b, i: (b, 0, 0)),
                  const((1, d)), const((1, d)),
                  _resident(wg.shape), _resident(wa.shape), _resident(wb.shape), _resident(wo.shape)],
        out_specs=tok(d),
        out_shape=jax.ShapeDtypeStruct((bsz, s, d), F32),
        compiler_params=_params("parallel", "parallel"),
        name="post",
    )(x, oa, ob, ada, g_pre, g_post, wg, wa, wb, wo)


def _ffn_kernel(x_ref, ada_ref, gpre_ref, gpost_ref, wi_ref, wo_ref, o_ref, *, sub, chunks):
    d_ff = wo_ref.shape[0]
    for r in range(x_ref.shape[1] // sub):
        rows = slice(r * sub, (r + 1) * sub)
        x = x_ref[0, rows, :]
        h = _rms_mod(x, gpre_ref[...], ada_ref[0, 4:5, :], ada_ref[0, 3:4, :]).astype(BF16)
        y = jnp.zeros(x.shape, F32)
        lo = 0
        for width in chunks:
            g = jnp.dot(h, wi_ref[:, lo:lo + width], preferred_element_type=F32)
            u = jnp.dot(h, wi_ref[:, d_ff + lo:d_ff + lo + width], preferred_element_type=F32)
            act = (g * jax.nn.sigmoid(g) * u).astype(BF16)
            y = y + jnp.dot(act, wo_ref[lo:lo + width, :], preferred_element_type=F32)
            lo += width
        o_ref[0, rows, :] = x + ada_ref[0, 5:6, :] * _rms(y, gpost_ref[...])


def _ffn(x, ada, g_pre, g_post, wi, wo, tm, sub, chunks):
    bsz, s, d = x.shape
    tok = pl.BlockSpec((1, tm, d), lambda b, i: (b, i, 0))
    const = lambda shape: pl.BlockSpec(shape, lambda b, i: (0,) * len(shape))
    return pl.pallas_call(
        functools.partial(_ffn_kernel, sub=sub, chunks=chunks),
        grid=(bsz, s // tm),
        in_specs=[tok, pl.BlockSpec((1, N_ADA, d), lambda b, i: (b, 0, 0)),
                  const((1, d)), const((1, d)), _resident(wi.shape), _resident(wo.shape)],
        out_specs=tok,
        out_shape=jax.ShapeDtypeStruct((bsz, s, d), F32),
        compiler_params=_params("parallel", "parallel"),
        name="ffn",
    )(x, ada, g_pre, g_post, wi, wo)


def _pair_heads(w, axis):
    shape = w.shape
    pre, post = shape[:axis], shape[axis + 1:]
    w = w.reshape(pre + (A_KV_HEADS, A_GROUP, HEAD_DIM) + post)
    w = jnp.swapaxes(w, axis, axis + 1)
    return w.reshape(shape)


def kernel(x, c, positions, w_ada, b_ada, g_pre_mix, g_post_mix, w_in, b_f, sinks, w_branch_a,
           w_branch_b, w_out, g_pre_ffn, g_post_ffn, w_ffn_in, w_ffn_out):
    bsz, s, d = x.shape
    depth = w_ada.shape[0]
    tm = min(1024, s)
    sub = min(512, s)
    rep = LANES // ROPE_FREQS
    assert (s // rep) % sub == 0
    half = jnp.arange(0, HEAD_DIM, 2, dtype=F32) / HEAD_DIM
    inv_freq = jnp.tile(1.0 / (ROPE_THETA ** half), rep).reshape(1, LANES)
    pos_c = jnp.swapaxes(positions.astype(F32).reshape(bsz, rep, s // rep), 1, 2)
    cos, sin = _rope_tables(jnp.repeat(pos_c, ROPE_FREQS, axis=2), inv_freq)
    for l in range(depth):
        w_l = w_in[l]
        o_va, o_qb, o_vb = A_Q_W + A_KV_W, A_Q_W + 2 * A_KV_W, A_Q_W + 2 * A_KV_W + 2 * B_W
        w_f = jnp.pad(w_l[:, QKV_W:QKV_W + B_HEADS], ((0, 0), (0, LANES - B_HEADS)))
        w_qk = jnp.concatenate([_pair_heads(w_l[:, :A_Q_W], 1), w_l[:, o_qb:o_vb], w_l[:, A_Q_W:o_va], w_f],
                               axis=1).astype(BF16)
        w_vt = jnp.concatenate([w_l[:, o_va:o_qb], w_l[:, o_vb:QKV_W]], axis=1).T.astype(BF16)
        w_g = w_l[:, QKV_W + B_HEADS:].astype(BF16)
        bf_row = jnp.pad(b_f[l], (0, LANES - B_HEADS)).reshape(1, LANES)
        sink_row = jnp.repeat(sinks[l], WINDOW).reshape(1, A_Q_HEADS * WINDOW)
        w_a = _pair_heads(w_branch_a[l], 0).astype(BF16)
        w_b = w_branch_b[l].astype(BF16)
        w_o = w_out[l].astype(BF16)
        w_fi = w_ffn_in[l].astype(BF16)
        w_fo = w_ffn_out[l].astype(BF16)
        row = lambda g: g[l].reshape(1, d)

        ada = _ada(c, w_ada[l], b_ada[l]).reshape(bsz, N_ADA, d)
        qa, ka, vat, qb, kb, vbt = _in_proj(x, cos, sin, ada, row(g_pre_mix), w_qk, w_vt, bf_row, tm, sub)
        o_a = _swa(qa, ka, vat, sink_row, sub)
        o_b = _fox(qb, kb, vbt, sub)
        x = _post(x, o_a, o_b, ada, row(g_pre_mix), row(g_post_mix), w_g, w_a, w_b, w_o, tm, sub)
        n_tiles = w_fo.shape[0] // MXU_DEPTH
        chunks = ((n_tiles + 1) // 2 * MXU_DEPTH, n_tiles // 2 * MXU_DEPTH)
        x = _ffn(x, ada, row(g_pre_ffn), row(g_post_ffn), w_fi, w_fo, tm, sub, chunks)
    return x
```

```python
import functools
import math

import jax
import jax.numpy as jnp
import numpy as np
from jax import lax
from jax.experimental import pallas as pl
from jax.experimental.pallas import tpu as pltpu

HEAD_DIM = 64
WINDOW = 128
A_Q_HEADS = 8
A_KV_HEADS = 2
A_GROUP = A_Q_HEADS // A_KV_HEADS
B_HEADS = 8
N_ADA = 6
ROPE_THETA = 10000.0
RMS_EPS = 1e-6
LANES = 128
MXU_DEPTH = 256
ROPE_FREQS = HEAD_DIM // 2
VMEM_LIMIT = 56 * 1024 * 1024

A_Q_W = A_Q_HEADS * HEAD_DIM
A_KV_W = A_KV_HEADS * HEAD_DIM
B_W = B_HEADS * HEAD_DIM
QKV_W = A_Q_W + 2 * A_KV_W + 3 * B_W
BF16 = jnp.bfloat16
F32 = jnp.float32
NEG_INF = float("-inf")
LOG2E = math.log2(math.e)
ONES_ROWS = 16
SKIP_LOG2 = 64.0
NORM_MARGIN = 1.02


def _params(*sem, flags=None):
    return pltpu.CompilerParams(dimension_semantics=sem, vmem_limit_bytes=VMEM_LIMIT, flags=flags)


def _rms_mod(x, g, scale, shift):
    ms = jnp.mean(x * x, axis=-1, keepdims=True)
    return (x * lax.rsqrt(ms + RMS_EPS)) * (g * (1.0 + scale)) + shift


def _rms(y, g):
    ms = jnp.mean(y * y, axis=-1, keepdims=True)
    return y * lax.rsqrt(ms + RMS_EPS) * g


def _ada_kernel(c_ref, w_ref, b_ref, o_ref):
    o_ref[...] = jnp.dot(c_ref[...], w_ref[...], preferred_element_type=F32) + b_ref[...]


def _ada(c, w, b):
    bsz, d = c.shape
    n = w.shape[1]
    tn = n // 4
    return pl.pallas_call(
        _ada_kernel,
        grid=(n // tn,),
        in_specs=[pl.BlockSpec((bsz, d), lambda j: (0, 0)),
                  pl.BlockSpec((d, tn), lambda j: (0, j)),
                  pl.BlockSpec((1, tn), lambda j: (0, j))],
        out_specs=pl.BlockSpec((bsz, tn), lambda j: (0, j)),
        out_shape=jax.ShapeDtypeStruct((bsz, n), F32),
        compiler_params=_params("arbitrary"),
        name="ada",
    )(c, w, b.reshape(1, n))


def _rope_table_kernel(pos_ref, invf_ref, cos_ref, sin_ref):
    ang = pos_ref[0] * invf_ref[...]
    cos_ref[0] = jnp.cos(ang)
    sin_ref[0] = jnp.sin(ang)


def _rope_tables(pos_c, inv_freq):
    bsz, rows, _ = pos_c.shape
    blk = pl.BlockSpec((1, rows, LANES), lambda b: (b, 0, 0))
    shape = jax.ShapeDtypeStruct((bsz, rows, LANES), F32)
    return pl.pallas_call(
        _rope_table_kernel,
        grid=(bsz,),
        in_specs=[blk, pl.BlockSpec((1, LANES), lambda b: (0, 0))],
        out_specs=[blk, blk],
        out_shape=[shape, shape],
        compiler_params=_params("parallel"),
        name="rope_tables",
    )(pos_c, inv_freq)


AUG_PARTS = 3


def _aug_base(head):
    return head * LANES + (HEAD_DIM if head % 2 == 0 else 0)


def _aug_placement():
    p = np.zeros((LANES, B_HEADS * LANES), np.float32)
    for head in range(B_HEADS):
        for part in range(AUG_PARTS):
            p[part * B_HEADS + head, _aug_base(head) + part] = 1.0
            p[part * B_HEADS + head, _aug_base(head) + AUG_PARTS + part] = -1.0
    return jnp.asarray(p, BF16)


def _inproj_kernel(x_ref, cos_ref, sin_ref, ada_ref, g_ref, w_ref, wvt_ref, bf_ref, place_ref, hsum_ref,
                   qa_ref, ka_ref, vat_ref, qb_ref, kb_ref, vbt_ref, stat_ref, carry_ref):
    i = pl.program_id(1)
    sub = vbt_ref.shape[3]
    lane = lax.broadcasted_iota(jnp.int32, (sub, LANES), 1)
    row = lax.broadcasted_iota(jnp.int32, (sub, LANES), 0)
    first_half = (lane % HEAD_DIM) < (HEAD_DIM // 2)
    low = lane < HEAD_DIM
    q_scale = LOG2E / math.sqrt(HEAD_DIM)

    @pl.when(i == 0)
    def _():
        carry_ref[...] = jnp.zeros(carry_ref.shape, F32)

    carry = carry_ref[0:1, :]
    for r in range(x_ref.shape[1] // sub):
        rows = slice(r * sub, (r + 1) * sub)
        h = _rms_mod(x_ref[0, rows, :], g_ref[...], ada_ref[0, 1:2, :], ada_ref[0, 0:1, :]).astype(BF16)

        def proj(lo, width):
            return jnp.dot(h, w_ref[:, lo:lo + width], preferred_element_type=F32)

        def proj_t(lo, width):
            return lax.dot_general(wvt_ref[lo:lo + width, :], h, (((1,), (1,)), ((), ())),
                                   preferred_element_type=F32)

        t0 = i * x_ref.shape[1] + r * sub
        quarter = t0 // cos_ref.shape[1]
        t_rows = pl.ds(pl.multiple_of(t0 - quarter * cos_ref.shape[1], sub), sub)
        shift = (LANES - quarter * ROPE_FREQS) % LANES

        def spread(tab):
            y = jnp.where(lane < ROPE_FREQS, pltpu.roll(tab, shift, axis=1), 0.0)
            y = y + pltpu.roll(y, ROPE_FREQS, axis=1)
            return y + pltpu.roll(y, 2 * ROPE_FREQS, axis=1)

        cos = spread(cos_ref[0, t_rows, :])
        sin = spread(sin_ref[0, t_rows, :])
        sin_signed = jnp.where(first_half, -sin, sin)

        def rope(t):
            partner = jnp.where(first_half,
                                pltpu.roll(t, LANES - HEAD_DIM // 2, axis=1),
                                pltpu.roll(t, HEAD_DIM // 2, axis=1))
            return t * cos + partner * sin_signed

        kf = proj(A_Q_W + 2 * B_W, A_KV_W + LANES)
        ka_ref[0, rows, :] = rope(kf[:, :A_KV_W]).astype(BF16)
        fl = kf[:, A_KV_W:] + bf_ref[...]
        c = jnp.minimum(fl, 0.0) - jnp.log1p(jnp.exp(-jnp.abs(fl)))
        d = 1
        while d < sub:
            c = c + jnp.where(row >= d, pltpu.roll(c, d, axis=0), 0.0)
            d *= 2
        c = c + carry
        carry = c[sub - 1:sub, :]
        c = c * LOG2E
        hi = c.astype(BF16).astype(F32)
        r1 = c - hi
        mid = r1.astype(BF16).astype(F32)
        lo = r1 - mid
        packed = jnp.where(lane < B_HEADS, hi,
                           jnp.where(lane < 2 * B_HEADS, pltpu.roll(mid, B_HEADS, axis=1),
                                     jnp.where(lane < 3 * B_HEADS, pltpu.roll(lo, 2 * B_HEADS, axis=1), 0.0)))

        qa = proj(0, A_Q_W)
        for j in range(A_Q_W // LANES):
            qa_ref[0, rows, j * LANES:(j + 1) * LANES] = (
                rope(qa[:, j * LANES:(j + 1) * LANES]) * q_scale).astype(BF16)
        vat_ref[0, :, rows] = proj_t(0, A_KV_W).astype(BF16)
        vbt_ref[0, r] = proj_t(A_KV_W, B_W).astype(BF16)
        qb = proj(A_Q_W, B_W) * q_scale
        kb = proj(A_Q_W + B_W, B_W)
        aug = jnp.dot(packed.astype(BF16), place_ref[...], preferred_element_type=F32)
        stats = [jnp.max(jnp.dot((t * t).astype(BF16), hsum_ref[...], preferred_element_type=F32),
                         axis=0, keepdims=True) * NORM_MARGIN for t in (qb, kb)]
        stats += [c[0:1, :], c[sub - 1:sub, :], jnp.zeros((4, LANES), F32)]
        stat_ref[0, r] = jnp.concatenate(stats, axis=0)
        for head in range(B_HEADS):
            base = _aug_base(head) % LANES
            own = low if head % 2 == 0 else jnp.logical_not(low)
            plus = (lane >= base) & (lane < base + AUG_PARTS)
            minus = (lane >= base + AUG_PARTS) & (lane < base + 2 * AUG_PARTS)
            a = aug[:, head * LANES:(head + 1) * LANES]
            src = slice(head // 2 * LANES, (head // 2 + 1) * LANES)
            group = slice(head * LANES, (head + 1) * LANES)
            qb_ref[0, rows, group] = jnp.where(own, qb[:, src],
                                               jnp.where(plus, a, jnp.where(minus, 1.0, 0.0))).astype(BF16)
            kb_ref[0, rows, group] = jnp.where(own, kb[:, src],
                                               jnp.where(minus, a, jnp.where(plus, 1.0, 0.0))).astype(BF16)
    carry_ref[0:1, :] = carry


def _in_proj(x, cos, sin, ada, g_pre, w_qk, w_vt, b_f, tm, sub):
    bsz, s, d = x.shape
    grid = (bsz, s // tm)
    tok = lambda w: pl.BlockSpec((1, tm, w), lambda b, i: (b, i, 0))
    const = lambda shape: pl.BlockSpec(shape, lambda b, i: (0,) * len(shape))
    act = lambda w: jax.ShapeDtypeStruct((bsz, s, w), BF16)
    place = _aug_placement()
    head_sum = jnp.asarray(np.kron(np.eye(B_HEADS, LANES), np.ones((HEAD_DIM, 1))), BF16)
    return pl.pallas_call(
        _inproj_kernel,
        grid=grid,
        in_specs=[tok(d), pl.BlockSpec((1,) + cos.shape[1:], lambda b, i: (b, 0, 0)),
                  pl.BlockSpec((1,) + sin.shape[1:], lambda b, i: (b, 0, 0)),
                  pl.BlockSpec((1, N_ADA, d), lambda b, i: (b, 0, 0)),
                  const((1, d)), _resident(w_qk.shape), _resident(w_vt.shape),
                  const((1, LANES)), const(place.shape), const(head_sum.shape)],
        out_specs=[tok(A_Q_W), tok(A_KV_W),
                   pl.BlockSpec((1, A_KV_W, tm), lambda b, i: (b, 0, i)),
                   tok(B_HEADS * LANES), tok(B_HEADS * LANES),
                   pl.BlockSpec((1, tm // sub, B_W, sub), lambda b, i: (b, i, 0, 0)),
                   pl.BlockSpec((1, tm // sub, 8, LANES), lambda b, i: (b, i, 0, 0))],
        out_shape=[act(A_Q_W), act(A_KV_W),
                   jax.ShapeDtypeStruct((bsz, A_KV_W, s), BF16),
                   act(B_HEADS * LANES), act(B_HEADS * LANES),
                   jax.ShapeDtypeStruct((bsz, s // sub, B_W, sub), BF16),
                   jax.ShapeDtypeStruct((bsz, s // sub, 8, LANES), F32)],
        scratch_shapes=[pltpu.VMEM((8, LANES), F32)],
        compiler_params=_params("parallel", "arbitrary"),
        name="in_proj",
    )(x, cos, sin, ada, g_pre, w_qk, w_vt, b_f, place, head_sum)


def _plan_kernel(st_ref, o_ref, *, ratio):
    nt = st_ref.shape[1]
    q_scale = LOG2E / math.sqrt(HEAD_DIM)
    qn = st_ref[0, 0, 0:1, :]
    kn = st_ref[0, 0, 1:2, :]
    for j in range(1, nt):
        qn = jnp.maximum(qn, st_ref[0, j, 0:1, :])
        kn = jnp.maximum(kn, st_ref[0, j, 1:2, :])
    spread = qn * (1.0 / q_scale) + kn * q_scale
    rows = []
    for blk in range(nt // ratio):
        limit = st_ref[0, blk * ratio, 2:3, :] + spread + SKIP_LOG2
        count = jnp.zeros((1, LANES), jnp.int32)
        prefix = jnp.ones((1, LANES), jnp.int32)
        for j in range(blk * ratio):
            prefix = prefix * (st_ref[0, j, 3:4, :] > limit).astype(jnp.int32)
            count = count + prefix
        rows.append(count)
    o_ref[0] = jnp.concatenate(rows, axis=0)


def _fox_plan(stats, ratio):
    bsz, nt, _, _ = stats.shape
    return pl.pallas_call(
        functools.partial(_plan_kernel, ratio=ratio),
        grid=(bsz,),
        in_specs=[pl.BlockSpec((1, nt, 8, LANES), lambda b: (b, 0, 0, 0))],
        out_specs=pl.BlockSpec((1, nt // ratio, LANES), lambda b: (b, 0, 0)),
        out_shape=jax.ShapeDtypeStruct((bsz, nt // ratio, LANES), jnp.int32),
        compiler_params=_params("parallel"),
        name="fox_plan",
    )(stats)


def _swa_kernel(q_ref, kp_ref, kc_ref, vtp_ref, vtc_ref, sink_ref, o_ref):
    i = pl.program_id(1)
    tq = q_ref.shape[1]
    nsub = tq // WINDOW
    ncol = A_Q_W // LANES
    lane_q = lax.broadcasted_iota(jnp.int32, (WINDOW, LANES), 1)
    low = lane_q < HEAD_DIM
    low_dim = lax.broadcasted_iota(jnp.int32, (LANES, WINDOW), 0) < HEAD_DIM
    k_loc = lax.broadcasted_iota(jnp.int32, (2 * WINDOW, WINDOW), 0)
    q_loc = lax.broadcasted_iota(jnp.int32, (2 * WINDOW, WINDOW), 1) + WINDOW
    rel = q_loc - k_loc
    band = (rel >= 0) & (rel < WINDOW)
    sink = sink_ref[...] * LOG2E
    ones = jnp.ones((ONES_ROWS, 2 * WINDOW), BF16)
    zero = jnp.zeros((WINDOW, LANES), BF16)

    def keys(j):
        if j == 0:
            return jnp.concatenate([kp_ref[0], kc_ref[0, :WINDOW, :]], axis=0)
        return kc_ref[0, (j - 1) * WINDOW:(j + 1) * WINDOW, :]

    def values_t(j):
        if j == 0:
            return jnp.concatenate([vtp_ref[0], vtc_ref[0, :, :WINDOW]], axis=1)
        return vtc_ref[0, :, (j - 1) * WINDOW:(j + 1) * WINDOW]

    def scores(j):
        q = q_ref[0, j * WINDOW:(j + 1) * WINDOW, :]
        parts = [jnp.where(low, q[:, c * LANES:(c + 1) * LANES], zero) for c in range(ncol)]
        parts += [jnp.where(low, zero, q[:, c * LANES:(c + 1) * LANES]) for c in range(ncol)]
        qs = jnp.concatenate(parts, axis=0)
        return lax.dot_general(keys(j), qs, (((1,), (1,)), ((), ())), preferred_element_type=F32)

    def softmax_pv(j, st):
        valid = band & ((k_loc >= WINDOW) | (i > 0)) if j == 0 else band
        ps, sink_terms = [], []
        for hh in range(A_Q_HEADS):
            cols = slice(hh * WINDOW, (hh + 1) * WINDOW)
            w = jnp.where(valid, st[:, cols], NEG_INF)
            m = jnp.maximum(jnp.max(w, axis=0, keepdims=True), sink[:, cols])
            ps.append(jnp.exp2(w - m).astype(BF16))
            sink_terms.append(jnp.exp2(sink[:, cols] - m))
        pt = jnp.concatenate(ps, axis=1)
        ot = jnp.dot(jnp.concatenate([values_t(j), ones], axis=0), pt, preferred_element_type=F32)
        for c in range(ncol):
            halves = []
            for hh in (c, ncol + c):
                cols = slice(hh * WINDOW, (hh + 1) * WINDOW)
                denom = ot[LANES:LANES + 1, cols] + sink_terms[hh]
                halves.append(ot[:LANES, cols] * (1.0 / denom))
            o_ref[0, j * WINDOW:(j + 1) * WINDOW, c * LANES:(c + 1) * LANES] = (
                jnp.transpose(jnp.where(low_dim, halves[0], halves[1])).astype(BF16))

    st = scores(0)
    for j in range(nsub):
        st_next = scores(j + 1) if j + 1 < nsub else None
        softmax_pv(j, st)
        st = st_next


def _swa(qa, ka, vat, sinks, tq):
    bsz, s, _ = qa.shape
    r = tq // WINDOW
    cur = lambda b, i: (b, i, 0)
    return pl.pallas_call(
        _swa_kernel,
        grid=(bsz, s // tq),
        in_specs=[pl.BlockSpec((1, tq, A_Q_W), cur),
                  pl.BlockSpec((1, WINDOW, A_KV_W), lambda b, i: (b, jnp.maximum(i * r - 1, 0), 0)),
                  pl.BlockSpec((1, tq, A_KV_W), cur),
                  pl.BlockSpec((1, A_KV_W, WINDOW), lambda b, i: (b, 0, jnp.maximum(i * r - 1, 0))),
                  pl.BlockSpec((1, A_KV_W, tq), lambda b, i: (b, 0, i)),
                  pl.BlockSpec((1, A_Q_HEADS * WINDOW), lambda b, i: (0, 0))],
        out_specs=pl.BlockSpec((1, tq, A_Q_W), cur),
        out_shape=jax.ShapeDtypeStruct((bsz, s, A_Q_W), BF16),
        compiler_params=_params("parallel", "parallel"),
        name="swa",
    )(qa, ka, ka, vat, vat, sinks)


def _fox_kernel(skip_ref, q_ref, k_ref, vt_ref, o_ref, m_ref, acc_ref, sta_ref, stb_ref):
    qi = pl.program_id(2)
    tq = q_ref.shape[1]
    tk = vt_ref.shape[3]
    nchunk = tq // LANES
    assert tq == 2 * tk
    m_ref[...] = jnp.full(m_ref.shape, NEG_INF, F32)
    acc_ref[...] = jnp.zeros(acc_ref.shape, F32)
    ones = jnp.ones((ONES_ROWS, tk), BF16)

    def scores(kj, st_ref, c0=0):
        start = pl.multiple_of(kj * tk, tk)
        for e in range(2):
            group = slice(e * LANES, (e + 1) * LANES)
            st_ref[e, :, c0 * LANES:] = lax.dot_general(
                k_ref[0, pl.ds(start, tk), group], q_ref[0, c0 * LANES:, group],
                (((1,), (1,)), ((), ())), preferred_element_type=F32)

    def softmax_pv(kj, st_ref, first_query=None, c0=0):
        for e in range(2):
            vta = jnp.concatenate([vt_ref[0, kj, e * HEAD_DIM:(e + 1) * HEAD_DIM, :], ones], axis=0)
            ps, alphas = [], []
            for c in range(c0, nchunk):
                cols = slice(c * LANES, (c + 1) * LANES)
                off = None if first_query is None else first_query + c * LANES
                masked = off is not None and off < tk
                live = min(off + LANES, tk) if masked else tk
                z = st_ref[e, :live, cols]
                if masked:
                    causal = (lax.broadcasted_iota(jnp.int32, (live, LANES), 0)
                              <= lax.broadcasted_iota(jnp.int32, (live, LANES), 1) + off)
                    z = jnp.where(causal, z, NEG_INF)
                m_old = m_ref[e, :, cols]
                m_new = jnp.maximum(m_old, jnp.max(z, axis=0, keepdims=True))
                p = jnp.exp2(z - m_new).astype(BF16)
                if live < tk:
                    p = jnp.concatenate([p, jnp.zeros((tk - live, LANES), BF16)], axis=0)
                ps.append(p)
                alphas.append(jnp.exp2(m_old - m_new))
                m_ref[e, :, cols] = m_new
            pt = jnp.concatenate(ps, axis=1)
            alpha = jnp.concatenate(alphas, axis=1)
            acc_ref[e, :, c0 * LANES:] = alpha * acc_ref[e, :, c0 * LANES:] + jnp.dot(
                vta, pt, preferred_element_type=F32)

    base = ((pl.program_id(0) * pl.num_programs(2) + qi) * B_HEADS + 2 * pl.program_id(1))
    first = jnp.minimum(skip_ref[base], skip_ref[base + 1])
    first_pair = (first + 1) // 2
    odd_start = first % 2 == 1

    @pl.when(odd_start)
    def _():
        scores(first, stb_ref)
        scores(first + 1, sta_ref)
        softmax_pv(first, stb_ref)

    @pl.when(jnp.logical_not(odd_start))
    def _():
        scores(first, sta_ref)

    def pair(t, carry):
        u = 2 * t
        scores(u + 1, stb_ref)
        softmax_pv(u, sta_ref)
        scores(u + 2, sta_ref)
        softmax_pv(u + 1, stb_ref)
        return carry

    lax.fori_loop(first_pair, qi, pair, 0)
    second = tk // LANES
    scores(2 * qi + 1, stb_ref, c0=second)
    softmax_pv(2 * qi, sta_ref, first_query=0)
    softmax_pv(2 * qi + 1, stb_ref, first_query=-tk, c0=second)

    out_t = jnp.concatenate(
        [acc_ref[e, :HEAD_DIM, :] * (1.0 / acc_ref[e, HEAD_DIM:HEAD_DIM + 1, :]) for e in range(2)], axis=0)
    o_ref[0] = jnp.transpose(out_t).astype(BF16)


def _fox(skip, qb, kb, vbt, tq):
    bsz, s, _ = qb.shape
    _, nk, _, tk = vbt.shape
    npair = B_HEADS // 2
    return pl.pallas_call(
        _fox_kernel,
        grid_spec=pltpu.PrefetchScalarGridSpec(
            num_scalar_prefetch=1,
            grid=(bsz, npair, s // tq),
            in_specs=[pl.BlockSpec((1, tq, 2 * LANES), lambda b, hp, i, skip: (b, i, hp)),
                      pl.BlockSpec((1, s, 2 * LANES), lambda b, hp, i, skip: (b, 0, hp)),
                      pl.BlockSpec((1, nk, 2 * HEAD_DIM, tk), lambda b, hp, i, skip: (b, 0, hp, 0))],
            out_specs=pl.BlockSpec((1, tq, 2 * HEAD_DIM), lambda b, hp, i, skip: (b, i, hp)),
            scratch_shapes=[pltpu.VMEM((2, 1, tq), F32),
                            pltpu.VMEM((2, HEAD_DIM + ONES_ROWS, tq), F32),
                            pltpu.VMEM((2, tk, tq), F32),
                            pltpu.VMEM((2, tk, tq), F32)]),
        out_shape=jax.ShapeDtypeStruct((bsz, s, B_W), BF16),
        compiler_params=_params("parallel", "arbitrary", "arbitrary"),
        name="fox",
    )(skip, qb, kb, vbt)


def _resident(shape):
    return pl.BlockSpec(shape, lambda b, i: (0,) * len(shape), pipeline_mode=pl.Buffered(1))


def _post_kernel(x_ref, oa_ref, ob_ref, ada_ref, gpre_ref, gpost_ref, wg_ref, wa_ref, wb_ref,
                 wo_ref, o_ref, *, sub):
    d = x_ref.shape[2]
    for r in range(x_ref.shape[1] // sub):
        rows = slice(r * sub, (r + 1) * sub)
        x = x_ref[0, rows, :]
        h = _rms_mod(x, gpre_ref[...], ada_ref[0, 1:2, :], ada_ref[0, 0:1, :]).astype(BF16)
        a = jnp.dot(oa_ref[0, rows, :], wa_ref[...], preferred_element_type=F32)
        b = jnp.dot(ob_ref[0, rows, :], wb_ref[...], preferred_element_type=F32)
        ga = jax.nn.sigmoid(jnp.dot(h, wg_ref[:, :d], preferred_element_type=F32))
        gb = jax.nn.sigmoid(jnp.dot(h, wg_ref[:, d:], preferred_element_type=F32))
        merged = (ga * a + gb * b).astype(BF16)
        y = jnp.dot(merged, wo_ref[...], preferred_element_type=F32)
        o_ref[0, rows, :] = x + ada_ref[0, 2:3, :] * _rms(y, gpost_ref[...])


def _post(x, oa, ob, ada, g_pre, g_post, wg, wa, wb, wo, tm, sub):
    bsz, s, d = x.shape
    tok = lambda w: pl.BlockSpec((1, tm, w), lambda b, i: (b, i, 0))
    const = lambda shape: pl.BlockSpec(shape, lambda b, i: (0,) * len(shape))
    return pl.pallas_call(
        functools.partial(_post_kernel, sub=sub),
        grid=(bsz, s // tm),
        in_specs=[tok(d), tok(A_Q_W), tok(B_W),
                  pl.BlockSpec((1, N_ADA, d), lambda b, i: (b, 0, 0)),
                  const((1, d)), const((1, d)),
                  _resident(wg.shape), _resident(wa.shape), _resident(wb.shape), _resident(wo.shape)],
        out_specs=tok(d),
        out_shape=jax.ShapeDtypeStruct((bsz, s, d), F32),
        compiler_params=_params("parallel", "parallel"),
        name="post",
    )(x, oa, ob, ada, g_pre, g_post, wg, wa, wb, wo)


def _ffn_kernel(x_ref, ada_ref, gpre_ref, gpost_ref, wi_ref, wo_ref, o_ref, *, sub, chunks):
    d_ff = wo_ref.shape[0]
    for r in range(x_ref.shape[1] // sub):
        rows = slice(r * sub, (r + 1) * sub)
        x = x_ref[0, rows, :]
        h = _rms_mod(x, gpre_ref[...], ada_ref[0, 4:5, :], ada_ref[0, 3:4, :]).astype(BF16)
        y = jnp.zeros(x.shape, F32)
        lo = 0
        for width in chunks:
            g = jnp.dot(h, wi_ref[:, lo:lo + width], preferred_element_type=F32)
            u = jnp.dot(h, wi_ref[:, d_ff + lo:d_ff + lo + width], preferred_element_type=F32)
            act = (g * jax.nn.sigmoid(g) * u).astype(BF16)
            y = y + jnp.dot(act, wo_ref[lo:lo + width, :], preferred_element_type=F32)
            lo += width
        o_ref[0, rows, :] = x + ada_ref[0, 5:6, :] * _rms(y, gpost_ref[...])


def _ffn(x, ada, g_pre, g_post, wi, wo, tm, sub, chunks):
    bsz, s, d = x.shape
    tok = pl.BlockSpec((1, tm, d), lambda b, i: (b, i, 0))
    const = lambda shape: pl.BlockSpec(shape, lambda b, i: (0,) * len(shape))
    return pl.pallas_call(
        functools.partial(_ffn_kernel, sub=sub, chunks=chunks),
        grid=(bsz, s // tm),
        in_specs=[tok, pl.BlockSpec((1, N_ADA, d), lambda b, i: (b, 0, 0)),
                  const((1, d)), const((1, d)), _resident(wi.shape), _resident(wo.shape)],
        out_specs=tok,
        out_shape=jax.ShapeDtypeStruct((bsz, s, d), F32),
        compiler_params=_params("parallel", "parallel"),
        name="ffn",
    )(x, ada, g_pre, g_post, wi, wo)


def _pair_heads(w, axis):
    shape = w.shape
    pre, post = shape[:axis], shape[axis + 1:]
    w = w.reshape(pre + (A_KV_HEADS, A_GROUP, HEAD_DIM) + post)
    w = jnp.swapaxes(w, axis, axis + 1)
    return w.reshape(shape)


def kernel(x, c, positions, w_ada, b_ada, g_pre_mix, g_post_mix, w_in, b_f, sinks, w_branch_a,
           w_branch_b, w_out, g_pre_ffn, g_post_ffn, w_ffn_in, w_ffn_out):
    bsz, s, d = x.shape
    depth = w_ada.shape[0]
    tm = min(1024, s)
    sub = min(512, s)
    rep = LANES // ROPE_FREQS
    assert (s // rep) % sub == 0
    half = jnp.arange(0, HEAD_DIM, 2, dtype=F32) / HEAD_DIM
    inv_freq = jnp.tile(1.0 / (ROPE_THETA ** half), rep).reshape(1, LANES)
    pos_c = jnp.swapaxes(positions.astype(F32).reshape(bsz, rep, s // rep), 1, 2)
    cos, sin = _rope_tables(jnp.repeat(pos_c, ROPE_FREQS, axis=2), inv_freq)
    for l in range(depth):
        w_l = w_in[l]
        o_va, o_qb, o_vb = A_Q_W + A_KV_W, A_Q_W + 2 * A_KV_W, A_Q_W + 2 * A_KV_W + 2 * B_W
        w_f = jnp.pad(w_l[:, QKV_W:QKV_W + B_HEADS], ((0, 0), (0, LANES - B_HEADS)))
        w_qk = jnp.concatenate([_pair_heads(w_l[:, :A_Q_W], 1), w_l[:, o_qb:o_vb], w_l[:, A_Q_W:o_va], w_f],
                               axis=1).astype(BF16)
        w_vt = jnp.concatenate([w_l[:, o_va:o_qb], w_l[:, o_vb:QKV_W]], axis=1).T.astype(BF16)
        w_g = w_l[:, QKV_W + B_HEADS:].astype(BF16)
        bf_row = jnp.pad(b_f[l], (0, LANES - B_HEADS)).reshape(1, LANES)
        sink_row = jnp.repeat(sinks[l], WINDOW).reshape(1, A_Q_HEADS * WINDOW)
        w_a = _pair_heads(w_branch_a[l], 0).astype(BF16)
        w_b = w_branch_b[l].astype(BF16)
        w_o = w_out[l].astype(BF16)
        w_fi = w_ffn_in[l].astype(BF16)
        w_fo = w_ffn_out[l].astype(BF16)
        row = lambda g: g[l].reshape(1, d)

        ada = _ada(c, w_ada[l], b_ada[l]).reshape(bsz, N_ADA, d)
        qa, ka, vat, qb, kb, vbt, stats = _in_proj(x, cos, sin, ada, row(g_pre_mix), w_qk, w_vt, bf_row,
                                                   tm, sub)
        o_a = _swa(qa, ka, vat, sink_row, sub)
        fox_ratio = 2
        skip = _fox_plan(stats, fox_ratio)[:, :, :B_HEADS].reshape(-1)
        o_b = _fox(skip, qb, kb, vbt, fox_ratio * sub)
        x = _post(x, o_a, o_b, ada, row(g_pre_mix), row(g_post_mix), w_g, w_a, w_b, w_o, tm, sub)
        n_tiles = w_fo.shape[0] // MXU_DEPTH
        chunks = ((n_tiles + 1) // 2 * MXU_DEPTH, n_tiles // 2 * MXU_DEPTH)
        x = _ffn(x, ada, row(g_pre_ffn), row(g_post_ffn), w_fi, w_fo, tm, sub, chunks)
    return x
```

```python
import functools
import math

import jax
import jax.numpy as jnp
import numpy as np
from jax import lax
from jax.experimental import pallas as pl
from jax.experimental.pallas import tpu as pltpu

HEAD_DIM = 64
WINDOW = 128
A_Q_HEADS = 8
A_KV_HEADS = 2
A_GROUP = A_Q_HEADS // A_KV_HEADS
B_HEADS = 8
N_ADA = 6
ROPE_THETA = 10000.0
RMS_EPS = 1e-6
LANES = 128
MXU_DEPTH = 256
ROPE_FREQS = HEAD_DIM // 2
VMEM_LIMIT = 56 * 1024 * 1024

A_Q_W = A_Q_HEADS * HEAD_DIM
A_KV_W = A_KV_HEADS * HEAD_DIM
B_W = B_HEADS * HEAD_DIM
QKV_W = A_Q_W + 2 * A_KV_W + 3 * B_W
BF16 = jnp.bfloat16
F32 = jnp.float32
NEG_INF = float("-inf")
LOG2E = math.log2(math.e)
ONES_ROWS = 16
SKIP_LOG2 = 48.0
NORM_MARGIN = 1.02
NORM_ROWS = 256


def _params(*sem, flags=None):
    return pltpu.CompilerParams(dimension_semantics=sem, vmem_limit_bytes=VMEM_LIMIT, flags=flags)


def _rms_mod(x, g, scale, shift):
    ms = jnp.mean(x * x, axis=-1, keepdims=True)
    return (x * lax.rsqrt(ms + RMS_EPS)) * (g * (1.0 + scale)) + shift


def _rms(y, g):
    ms = jnp.mean(y * y, axis=-1, keepdims=True)
    return y * lax.rsqrt(ms + RMS_EPS) * g


def _ada_kernel(c_ref, w_ref, b_ref, o_ref):
    o_ref[...] = jnp.dot(c_ref[...], w_ref[...], preferred_element_type=F32) + b_ref[...]


def _ada(c, w, b):
    bsz, d = c.shape
    n = w.shape[1]
    tn = n // 4
    return pl.pallas_call(
        _ada_kernel,
        grid=(n // tn,),
        in_specs=[pl.BlockSpec((bsz, d), lambda j: (0, 0)),
                  pl.BlockSpec((d, tn), lambda j: (0, j)),
                  pl.BlockSpec((1, tn), lambda j: (0, j))],
        out_specs=pl.BlockSpec((bsz, tn), lambda j: (0, j)),
        out_shape=jax.ShapeDtypeStruct((bsz, n), F32),
        compiler_params=_params("arbitrary"),
        name="ada",
    )(c, w, b.reshape(1, n))


def _rope_table_kernel(pos_ref, invf_ref, cos_ref, sin_ref):
    ang = pos_ref[0] * invf_ref[...]
    cos_ref[0] = jnp.cos(ang)
    sin_ref[0] = jnp.sin(ang)


def _rope_tables(pos_c, inv_freq):
    bsz, rows, _ = pos_c.shape
    blk = pl.BlockSpec((1, rows, LANES), lambda b: (b, 0, 0))
    shape = jax.ShapeDtypeStruct((bsz, rows, LANES), F32)
    return pl.pallas_call(
        _rope_table_kernel,
        grid=(bsz,),
        in_specs=[blk, pl.BlockSpec((1, LANES), lambda b: (0, 0))],
        out_specs=[blk, blk],
        out_shape=[shape, shape],
        compiler_params=_params("parallel"),
        name="rope_tables",
    )(pos_c, inv_freq)


AUG_PARTS = 3


def _aug_base(head):
    return head * LANES + (HEAD_DIM if head % 2 == 0 else 0)


def _aug_placement():
    p = np.zeros((LANES, B_HEADS * LANES), np.float32)
    for head in range(B_HEADS):
        for part in range(AUG_PARTS):
            p[part * B_HEADS + head, _aug_base(head) + part] = 1.0
            p[part * B_HEADS + head, _aug_base(head) + AUG_PARTS + part] = -1.0
    return jnp.asarray(p, BF16)


def _inproj_kernel(x_ref, cos_ref, sin_ref, ada_ref, g_ref, w_ref, wvt_ref, bf_ref, place_ref, hsum_ref,
                   qa_ref, ka_ref, vat_ref, qb_ref, kb_ref, vbt_ref, stat_ref, carry_ref):
    i = pl.program_id(1)
    sub = vbt_ref.shape[3]
    lane = lax.broadcasted_iota(jnp.int32, (sub, LANES), 1)
    row = lax.broadcasted_iota(jnp.int32, (sub, LANES), 0)
    first_half = (lane % HEAD_DIM) < (HEAD_DIM // 2)
    low = lane < HEAD_DIM
    q_scale = LOG2E / math.sqrt(HEAD_DIM)

    @pl.when(i == 0)
    def _():
        carry_ref[...] = jnp.zeros(carry_ref.shape, F32)

    carry = carry_ref[0:1, :]
    for r in range(x_ref.shape[1] // sub):
        rows = slice(r * sub, (r + 1) * sub)
        h = _rms_mod(x_ref[0, rows, :], g_ref[...], ada_ref[0, 1:2, :], ada_ref[0, 0:1, :]).astype(BF16)

        def proj(lo, width):
            return jnp.dot(h, w_ref[:, lo:lo + width], preferred_element_type=F32)

        def proj_t(lo, width):
            return lax.dot_general(wvt_ref[lo:lo + width, :], h, (((1,), (1,)), ((), ())),
                                   preferred_element_type=F32)

        t0 = i * x_ref.shape[1] + r * sub
        quarter = t0 // cos_ref.shape[1]
        t_rows = pl.ds(pl.multiple_of(t0 - quarter * cos_ref.shape[1], sub), sub)
        shift = (LANES - quarter * ROPE_FREQS) % LANES

        def spread(tab):
            y = jnp.where(lane < ROPE_FREQS, pltpu.roll(tab, shift, axis=1), 0.0)
            y = y + pltpu.roll(y, ROPE_FREQS, axis=1)
            return y + pltpu.roll(y, 2 * ROPE_FREQS, axis=1)

        cos = spread(cos_ref[0, t_rows, :])
        sin = spread(sin_ref[0, t_rows, :])
        sin_signed = jnp.where(first_half, -sin, sin)

        def rope(t):
            partner = jnp.where(first_half,
                                pltpu.roll(t, LANES - HEAD_DIM // 2, axis=1),
                                pltpu.roll(t, HEAD_DIM // 2, axis=1))
            return t * cos + partner * sin_signed

        kf = proj(A_Q_W + 2 * B_W, A_KV_W + LANES)
        ka_ref[0, rows, :] = rope(kf[:, :A_KV_W]).astype(BF16)
        fl = kf[:, A_KV_W:] + bf_ref[...]
        c = jnp.minimum(fl, 0.0) - jnp.log1p(jnp.exp(-jnp.abs(fl)))
        d = 1
        while d < sub:
            c = c + jnp.where(row >= d, pltpu.roll(c, d, axis=0), 0.0)
            d *= 2
        c = c + carry
        carry = c[sub - 1:sub, :]
        c = c * LOG2E
        hi = c.astype(BF16).astype(F32)
        r1 = c - hi
        mid = r1.astype(BF16).astype(F32)
        lo = r1 - mid
        packed = jnp.where(lane < B_HEADS, hi,
                           jnp.where(lane < 2 * B_HEADS, pltpu.roll(mid, B_HEADS, axis=1),
                                     jnp.where(lane < 3 * B_HEADS, pltpu.roll(lo, 2 * B_HEADS, axis=1), 0.0)))

        qa = proj(0, A_Q_W)
        for j in range(A_Q_W // LANES):
            qa_ref[0, rows, j * LANES:(j + 1) * LANES] = (
                rope(qa[:, j * LANES:(j + 1) * LANES]) * q_scale).astype(BF16)
        vat_ref[0, :, rows] = proj_t(0, A_KV_W).astype(BF16)
        vbt_ref[0, r] = proj_t(A_KV_W, B_W).astype(BF16)
        qb = proj(A_Q_W, B_W) * q_scale
        kb = proj(A_Q_W + B_W, B_W)
        aug = jnp.dot(packed.astype(BF16), place_ref[...], preferred_element_type=F32)
        def norm_bound(t):
            sq = t * t
            fold = sq[:NORM_ROWS]
            for g in range(1, sub // NORM_ROWS):
                fold = jnp.maximum(fold, sq[g * NORM_ROWS:(g + 1) * NORM_ROWS])
            per_head = jnp.dot(fold.astype(BF16), hsum_ref[...], preferred_element_type=F32)
            return jnp.max(per_head, axis=0, keepdims=True) * NORM_MARGIN

        stats = [norm_bound(qb), norm_bound(kb)]
        stats += [c[0:1, :], c[sub - 1:sub, :], jnp.zeros((4, LANES), F32)]
        stat_ref[0, r] = jnp.concatenate(stats, axis=0)
        for head in range(B_HEADS):
            base = _aug_base(head) % LANES
            own = low if head % 2 == 0 else jnp.logical_not(low)
            plus = (lane >= base) & (lane < base + AUG_PARTS)
            minus = (lane >= base + AUG_PARTS) & (lane < base + 2 * AUG_PARTS)
            a = aug[:, head * LANES:(head + 1) * LANES]
            src = slice(head // 2 * LANES, (head // 2 + 1) * LANES)
            group = slice(head * LANES, (head + 1) * LANES)
            qb_ref[0, rows, group] = jnp.where(own, qb[:, src],
                                               jnp.where(plus, a, jnp.where(minus, 1.0, 0.0))).astype(BF16)
            kb_ref[0, rows, group] = jnp.where(own, kb[:, src],
                                               jnp.where(minus, a, jnp.where(plus, 1.0, 0.0))).astype(BF16)
    carry_ref[0:1, :] = carry


def _in_proj(x, cos, sin, ada, g_pre, w_qk, w_vt, b_f, tm, sub):
    bsz, s, d = x.shape
    grid = (bsz, s // tm)
    tok = lambda w: pl.BlockSpec((1, tm, w), lambda b, i: (b, i, 0))
    const = lambda shape: pl.BlockSpec(shape, lambda b, i: (0,) * len(shape))
    act = lambda w: jax.ShapeDtypeStruct((bsz, s, w), BF16)
    place = _aug_placement()
    head_sum = jnp.asarray(np.kron(np.eye(B_HEADS, LANES), np.ones((HEAD_DIM, 1))), BF16)
    return pl.pallas_call(
        _inproj_kernel,
        grid=grid,
        in_specs=[tok(d), pl.BlockSpec((1,) + cos.shape[1:], lambda b, i: (b, 0, 0)),
                  pl.BlockSpec((1,) + sin.shape[1:], lambda b, i: (b, 0, 0)),
                  pl.BlockSpec((1, N_ADA, d), lambda b, i: (b, 0, 0)),
                  const((1, d)), _resident(w_qk.shape), _resident(w_vt.shape),
                  const((1, LANES)), const(place.shape), const(head_sum.shape)],
        out_specs=[tok(A_Q_W), tok(A_KV_W),
                   pl.BlockSpec((1, A_KV_W, tm), lambda b, i: (b, 0, i)),
                   tok(B_HEADS * LANES), tok(B_HEADS * LANES),
                   pl.BlockSpec((1, tm // sub, B_W, sub), lambda b, i: (b, i, 0, 0)),
                   pl.BlockSpec((1, tm // sub, 8, LANES), lambda b, i: (b, i, 0, 0))],
        out_shape=[act(A_Q_W), act(A_KV_W),
                   jax.ShapeDtypeStruct((bsz, A_KV_W, s), BF16),
                   act(B_HEADS * LANES), act(B_HEADS * LANES),
                   jax.ShapeDtypeStruct((bsz, s // sub, B_W, sub), BF16),
                   jax.ShapeDtypeStruct((bsz, s // sub, 8, LANES), F32)],
        scratch_shapes=[pltpu.VMEM((8, LANES), F32)],
        compiler_params=_params("parallel", "arbitrary"),
        name="in_proj",
    )(x, cos, sin, ada, g_pre, w_qk, w_vt, b_f, place, head_sum)


def _plan_kernel(st_ref, o_ref, *, ratio):
    nt = st_ref.shape[1]
    q_scale = LOG2E / math.sqrt(HEAD_DIM)
    qn = st_ref[0, 0, 0:1, :]
    kn = st_ref[0, 0, 1:2, :]
    for j in range(1, nt):
        qn = jnp.maximum(qn, st_ref[0, j, 0:1, :])
        kn = jnp.maximum(kn, st_ref[0, j, 1:2, :])
    spread = qn * (1.0 / q_scale) + kn * q_scale
    rows = []
    for blk in range(nt // ratio):
        limit = st_ref[0, blk * ratio, 2:3, :] + spread + SKIP_LOG2
        count = jnp.zeros((1, LANES), jnp.int32)
        prefix = jnp.ones((1, LANES), jnp.int32)
        for j in range(blk * ratio):
            prefix = prefix * (st_ref[0, j, 3:4, :] > limit).astype(jnp.int32)
            count = count + prefix
        rows.append(count)
    o_ref[0] = jnp.concatenate(rows, axis=0)


def _fox_plan(stats, ratio):
    bsz, nt, _, _ = stats.shape
    return pl.pallas_call(
        functools.partial(_plan_kernel, ratio=ratio),
        grid=(bsz,),
        in_specs=[pl.BlockSpec((1, nt, 8, LANES), lambda b: (b, 0, 0, 0))],
        out_specs=pl.BlockSpec((1, nt // ratio, LANES), lambda b: (b, 0, 0)),
        out_shape=jax.ShapeDtypeStruct((bsz, nt // ratio, LANES), jnp.int32),
        compiler_params=_params("parallel"),
        name="fox_plan",
    )(stats)


def _swa_kernel(q_ref, kp_ref, kc_ref, vtp_ref, vtc_ref, sink_ref, o_ref):
    i = pl.program_id(1)
    tq = q_ref.shape[1]
    nsub = tq // WINDOW
    ncol = A_Q_W // LANES
    lane_q = lax.broadcasted_iota(jnp.int32, (WINDOW, LANES), 1)
    low = lane_q < HEAD_DIM
    low_dim = lax.broadcasted_iota(jnp.int32, (LANES, WINDOW), 0) < HEAD_DIM
    k_loc = lax.broadcasted_iota(jnp.int32, (2 * WINDOW, WINDOW), 0)
    q_loc = lax.broadcasted_iota(jnp.int32, (2 * WINDOW, WINDOW), 1) + WINDOW
    rel = q_loc - k_loc
    band = (rel >= 0) & (rel < WINDOW)
    sink = sink_ref[...] * LOG2E
    ones = jnp.ones((ONES_ROWS, 2 * WINDOW), BF16)
    zero = jnp.zeros((WINDOW, LANES), BF16)

    def keys(j):
        if j == 0:
            return jnp.concatenate([kp_ref[0], kc_ref[0, :WINDOW, :]], axis=0)
        return kc_ref[0, (j - 1) * WINDOW:(j + 1) * WINDOW, :]

    def values_t(j):
        if j == 0:
            return jnp.concatenate([vtp_ref[0], vtc_ref[0, :, :WINDOW]], axis=1)
        return vtc_ref[0, :, (j - 1) * WINDOW:(j + 1) * WINDOW]

    def scores(j):
        q = q_ref[0, j * WINDOW:(j + 1) * WINDOW, :]
        parts = [jnp.where(low, q[:, c * LANES:(c + 1) * LANES], zero) for c in range(ncol)]
        parts += [jnp.where(low, zero, q[:, c * LANES:(c + 1) * LANES]) for c in range(ncol)]
        qs = jnp.concatenate(parts, axis=0)
        return lax.dot_general(keys(j), qs, (((1,), (1,)), ((), ())), preferred_element_type=F32)

    def softmax_pv(j, st):
        valid = band & ((k_loc >= WINDOW) | (i > 0)) if j == 0 else band
        ps, sink_terms = [], []
        for hh in range(A_Q_HEADS):
            cols = slice(hh * WINDOW, (hh + 1) * WINDOW)
            w = jnp.where(valid, st[:, cols], NEG_INF)
            m = jnp.maximum(jnp.max(w, axis=0, keepdims=True), sink[:, cols])
            ps.append(jnp.exp2(w - m).astype(BF16))
            sink_terms.append(jnp.exp2(sink[:, cols] - m))
        pt = jnp.concatenate(ps, axis=1)
        ot = jnp.dot(jnp.concatenate([values_t(j), ones], axis=0), pt, preferred_element_type=F32)
        for c in range(ncol):
            halves = []
            for hh in (c, ncol + c):
                cols = slice(hh * WINDOW, (hh + 1) * WINDOW)
                denom = ot[LANES:LANES + 1, cols] + sink_terms[hh]
                halves.append(ot[:LANES, cols] * (1.0 / denom))
            o_ref[0, j * WINDOW:(j + 1) * WINDOW, c * LANES:(c + 1) * LANES] = (
                jnp.transpose(jnp.where(low_dim, halves[0], halves[1])).astype(BF16))

    st = scores(0)
    for j in range(nsub):
        st_next = scores(j + 1) if j + 1 < nsub else None
        softmax_pv(j, st)
        st = st_next


def _swa(qa, ka, vat, sinks, tq):
    bsz, s, _ = qa.shape
    r = tq // WINDOW
    cur = lambda b, i: (b, i, 0)
    return pl.pallas_call(
        _swa_kernel,
        grid=(bsz, s // tq),
        in_specs=[pl.BlockSpec((1, tq, A_Q_W), cur),
                  pl.BlockSpec((1, WINDOW, A_KV_W), lambda b, i: (b, jnp.maximum(i * r - 1, 0), 0)),
                  pl.BlockSpec((1, tq, A_KV_W), cur),
                  pl.BlockSpec((1, A_KV_W, WINDOW), lambda b, i: (b, 0, jnp.maximum(i * r - 1, 0))),
                  pl.BlockSpec((1, A_KV_W, tq), lambda b, i: (b, 0, i)),
                  pl.BlockSpec((1, A_Q_HEADS * WINDOW), lambda b, i: (0, 0))],
        out_specs=pl.BlockSpec((1, tq, A_Q_W), cur),
        out_shape=jax.ShapeDtypeStruct((bsz, s, A_Q_W), BF16),
        compiler_params=_params("parallel", "parallel"),
        name="swa",
    )(qa, ka, ka, vat, vat, sinks)


def _fox_kernel(skip_ref, q_ref, k_ref, vt_ref, o_ref, m_ref, acc_ref, sta_ref, stb_ref, stc_ref):
    qi = pl.program_id(2)
    tq = q_ref.shape[1]
    tk = vt_ref.shape[3]
    nchunk = tq // LANES
    assert tq == 2 * tk
    m_ref[...] = jnp.full(m_ref.shape, NEG_INF, F32)
    acc_ref[...] = jnp.zeros(acc_ref.shape, F32)
    ones = jnp.ones((ONES_ROWS, tk), BF16)

    def scores(kj, st_ref, c0=0):
        start = pl.multiple_of(kj * tk, tk)
        for e in range(2):
            group = slice(e * LANES, (e + 1) * LANES)
            st_ref[e, :, c0 * LANES:] = lax.dot_general(
                k_ref[0, pl.ds(start, tk), group], q_ref[0, c0 * LANES:, group],
                (((1,), (1,)), ((), ())), preferred_element_type=F32)

    def softmax_pv(kj, st_ref, first_query=None, c0=0):
        for e in range(2):
            vta = jnp.concatenate([vt_ref[0, kj, e * HEAD_DIM:(e + 1) * HEAD_DIM, :], ones], axis=0)
            ps, alphas = [], []
            for c in range(c0, nchunk):
                cols = slice(c * LANES, (c + 1) * LANES)
                off = None if first_query is None else first_query + c * LANES
                masked = off is not None and off < tk
                live = min(off + LANES, tk) if masked else tk
                z = st_ref[e, :live, cols]
                if masked:
                    causal = (lax.broadcasted_iota(jnp.int32, (live, LANES), 0)
                              <= lax.broadcasted_iota(jnp.int32, (live, LANES), 1) + off)
                    z = jnp.where(causal, z, NEG_INF)
                m_old = m_ref[e, :, cols]
                m_new = jnp.maximum(m_old, jnp.max(z, axis=0, keepdims=True))
                p = jnp.exp2(z - m_new).astype(BF16)
                if live < tk:
                    p = jnp.concatenate([p, jnp.zeros((tk - live, LANES), BF16)], axis=0)
                ps.append(p)
                alphas.append(jnp.exp2(m_old - m_new))
                m_ref[e, :, cols] = m_new
            pt = jnp.concatenate(ps, axis=1)
            alpha = jnp.concatenate(alphas, axis=1)
            acc_ref[e, :, c0 * LANES:] = alpha * acc_ref[e, :, c0 * LANES:] + jnp.dot(
                vta, pt, preferred_element_type=F32)

    base = ((pl.program_id(0) * pl.num_programs(2) + qi) * B_HEADS + 2 * pl.program_id(1))
    first = jnp.minimum(skip_ref[base], skip_ref[base + 1])
    first_pair = (first + 1) // 2
    odd_start = first % 2 == 1
    second = tk // LANES

    @pl.when(odd_start)
    def _():
        scores(first, stb_ref)
        scores(first + 1, sta_ref)
        scores(2 * qi + 1, stc_ref, c0=second)
        softmax_pv(first, stb_ref)

    @pl.when(jnp.logical_not(odd_start))
    def _():
        scores(first, sta_ref)
        scores(2 * qi + 1, stc_ref, c0=second)

    def pair(t, carry):
        u = 2 * t
        scores(u + 1, stb_ref)
        softmax_pv(u, sta_ref)
        scores(u + 2, sta_ref)
        softmax_pv(u + 1, stb_ref)
        return carry

    lax.fori_loop(first_pair, qi, pair, 0)
    softmax_pv(2 * qi, sta_ref, first_query=0)
    softmax_pv(2 * qi + 1, stc_ref, first_query=-tk, c0=second)

    out_t = jnp.concatenate(
        [acc_ref[e, :HEAD_DIM, :] * (1.0 / acc_ref[e, HEAD_DIM:HEAD_DIM + 1, :]) for e in range(2)], axis=0)
    o_ref[0] = jnp.transpose(out_t).astype(BF16)


def _fox(skip, qb, kb, vbt, tq):
    bsz, s, _ = qb.shape
    _, nk, _, tk = vbt.shape
    npair = B_HEADS // 2
    return pl.pallas_call(
        _fox_kernel,
        grid_spec=pltpu.PrefetchScalarGridSpec(
            num_scalar_prefetch=1,
            grid=(bsz, npair, s // tq),
            in_specs=[pl.BlockSpec((1, tq, 2 * LANES), lambda b, hp, i, skip: (b, i, hp)),
                      pl.BlockSpec((1, s, 2 * LANES), lambda b, hp, i, skip: (b, 0, hp)),
                      pl.BlockSpec((1, nk, 2 * HEAD_DIM, tk), lambda b, hp, i, skip: (b, 0, hp, 0))],
            out_specs=pl.BlockSpec((1, tq, 2 * HEAD_DIM), lambda b, hp, i, skip: (b, i, hp)),
            scratch_shapes=[pltpu.VMEM((2, 1, tq), F32),
                            pltpu.VMEM((2, HEAD_DIM + ONES_ROWS, tq), F32),
                            pltpu.VMEM((2, tk, tq), F32),
                            pltpu.VMEM((2, tk, tq), F32),
                            pltpu.VMEM((2, tk, tq), F32)]),
        out_shape=jax.ShapeDtypeStruct((bsz, s, B_W), BF16),
        compiler_params=_params("parallel", "arbitrary", "arbitrary"),
        name="fox",
    )(skip, qb, kb, vbt)


def _resident(shape):
    return pl.BlockSpec(shape, lambda b, i: (0,) * len(shape), pipeline_mode=pl.Buffered(1))


def _post_kernel(x_ref, oa_ref, ob_ref, ada_ref, gpre_ref, gpost_ref, wg_ref, wa_ref, wb_ref,
                 wo_ref, o_ref, *, sub):
    d = x_ref.shape[2]
    for r in range(x_ref.shape[1] // sub):
        rows = slice(r * sub, (r + 1) * sub)
        x = x_ref[0, rows, :]
        h = _rms_mod(x, gpre_ref[...], ada_ref[0, 1:2, :], ada_ref[0, 0:1, :]).astype(BF16)
        a = jnp.dot(oa_ref[0, rows, :], wa_ref[...], preferred_element_type=F32)
        b = jnp.dot(ob_ref[0, rows, :], wb_ref[...], preferred_element_type=F32)
        ga = jax.nn.sigmoid(jnp.dot(h, wg_ref[:, :d], preferred_element_type=F32))
        gb = jax.nn.sigmoid(jnp.dot(h, wg_ref[:, d:], preferred_element_type=F32))
        merged = (ga * a + gb * b).astype(BF16)
        y = jnp.dot(merged, wo_ref[...], preferred_element_type=F32)
        o_ref[0, rows, :] = x + ada_ref[0, 2:3, :] * _rms(y, gpost_ref[...])


def _post(x, oa, ob, ada, g_pre, g_post, wg, wa, wb, wo, tm, sub):
    bsz, s, d = x.shape
    tok = lambda w: pl.BlockSpec((1, tm, w), lambda b, i: (b, i, 0))
    const = lambda shape: pl.BlockSpec(shape, lambda b, i: (0,) * len(shape))
    return pl.pallas_call(
        functools.partial(_post_kernel, sub=sub),
        grid=(bsz, s // tm),
        in_specs=[tok(d), tok(A_Q_W), tok(B_W),
                  pl.BlockSpec((1, N_ADA, d), lambda b, i: (b, 0, 0)),
                  const((1, d)), const((1, d)),
                  _resident(wg.shape), _resident(wa.shape), _resident(wb.shape), _resident(wo.shape)],
        out_specs=tok(d),
        out_shape=jax.ShapeDtypeStruct((bsz, s, d), F32),
        compiler_params=_params("parallel", "parallel"),
        name="post",
    )(x, oa, ob, ada, g_pre, g_post, wg, wa, wb, wo)


def _ffn_kernel(x_ref, ada_ref, gpre_ref, gpost_ref, wi_ref, wo_ref, o_ref, *, sub, chunks):
    d_ff = wo_ref.shape[0]
    for r in range(x_ref.shape[1] // sub):
        rows = slice(r * sub, (r + 1) * sub)
        x = x_ref[0, rows, :]
        h = _rms_mod(x, gpre_ref[...], ada_ref[0, 4:5, :], ada_ref[0, 3:4, :]).astype(BF16)
        y = jnp.zeros(x.shape, F32)
        lo = 0
        for width in chunks:
            g = jnp.dot(h, wi_ref[:, lo:lo + width], preferred_element_type=F32)
            u = jnp.dot(h, wi_ref[:, d_ff + lo:d_ff + lo + width], preferred_element_type=F32)
            act = (g * jax.nn.sigmoid(g) * u).astype(BF16)
            y = y + jnp.dot(act, wo_ref[lo:lo + width, :], preferred_element_type=F32)
            lo += width
        o_ref[0, rows, :] = x + ada_ref[0, 5:6, :] * _rms(y, gpost_ref[...])


def _ffn(x, ada, g_pre, g_post, wi, wo, tm, sub, chunks):
    bsz, s, d = x.shape
    tok = pl.BlockSpec((1, tm, d), lambda b, i: (b, i, 0))
    const = lambda shape: pl.BlockSpec(shape, lambda b, i: (0,) * len(shape))
    return pl.pallas_call(
        functools.partial(_ffn_kernel, sub=sub, chunks=chunks),
        grid=(bsz, s // tm),
        in_specs=[tok, pl.BlockSpec((1, N_ADA, d), lambda b, i: (b, 0, 0)),
                  const((1, d)), const((1, d)), _resident(wi.shape), _resident(wo.shape)],
        out_specs=tok,
        out_shape=jax.ShapeDtypeStruct((bsz, s, d), F32),
        compiler_params=_params("parallel", "parallel"),
        name="ffn",
    )(x, ada, g_pre, g_post, wi, wo)


def _pair_heads(w, axis):
    shape = w.shape
    pre, post = shape[:axis], shape[axis + 1:]
    w = w.reshape(pre + (A_KV_HEADS, A_GROUP, HEAD_DIM) + post)
    w = jnp.swapaxes(w, axis, axis + 1)
    return w.reshape(shape)


def kernel(x, c, positions, w_ada, b_ada, g_pre_mix, g_post_mix, w_in, b_f, sinks, w_branch_a,
           w_branch_b, w_out, g_pre_ffn, g_post_ffn, w_ffn_in, w_ffn_out):
    bsz, s, d = x.shape
    depth = w_ada.shape[0]
    tm = min(1024, s)
    sub = min(512, s)
    rep = LANES // ROPE_FREQS
    assert (s // rep) % sub == 0
    half = jnp.arange(0, HEAD_DIM, 2, dtype=F32) / HEAD_DIM
    inv_freq = jnp.tile(1.0 / (ROPE_THETA ** half), rep).reshape(1, LANES)
    pos_c = jnp.swapaxes(positions.astype(F32).reshape(bsz, rep, s // rep), 1, 2)
    cos, sin = _rope_tables(jnp.repeat(pos_c, ROPE_FREQS, axis=2), inv_freq)
    for l in range(depth):
        w_l = w_in[l]
        o_va, o_qb, o_vb = A_Q_W + A_KV_W, A_Q_W + 2 * A_KV_W, A_Q_W + 2 * A_KV_W + 2 * B_W
        w_f = jnp.pad(w_l[:, QKV_W:QKV_W + B_HEADS], ((0, 0), (0, LANES - B_HEADS)))
        w_qk = jnp.concatenate([_pair_heads(w_l[:, :A_Q_W], 1), w_l[:, o_qb:o_vb], w_l[:, A_Q_W:o_va], w_f],
                               axis=1).astype(BF16)
        w_vt = jnp.concatenate([w_l[:, o_va:o_qb], w_l[:, o_vb:QKV_W]], axis=1).T.astype(BF16)
        w_g = w_l[:, QKV_W + B_HEADS:].astype(BF16)
        bf_row = jnp.pad(b_f[l], (0, LANES - B_HEADS)).reshape(1, LANES)
        sink_row = jnp.repeat(sinks[l], WINDOW).reshape(1, A_Q_HEADS * WINDOW)
        w_a = _pair_heads(w_branch_a[l], 0).astype(BF16)
        w_b = w_branch_b[l].astype(BF16)
        w_o = w_out[l].astype(BF16)
        w_fi = w_ffn_in[l].astype(BF16)
        w_fo = w_ffn_out[l].astype(BF16)
        row = lambda g: g[l].reshape(1, d)

        ada = _ada(c, w_ada[l], b_ada[l]).reshape(bsz, N_ADA, d)
        qa, ka, vat, qb, kb, vbt, stats = _in_proj(x, cos, sin, ada, row(g_pre_mix), w_qk, w_vt, bf_row,
                                                   tm, sub)
        o_a = _swa(qa, ka, vat, sink_row, tm)
        fox_ratio = 2
        skip = _fox_plan(stats, fox_ratio)[:, :, :B_HEADS].reshape(-1)
        o_b = _fox(skip, qb, kb, vbt, fox_ratio * sub)
        x = _post(x, o_a, o_b, ada, row(g_pre_mix), row(g_post_mix), w_g, w_a, w_b, w_o, tm, sub)
        n_tiles = w_fo.shape[0] // MXU_DEPTH
        chunks = ((n_tiles + 1) // 2 * MXU_DEPTH, n_tiles // 2 * MXU_DEPTH)
        x = _ffn(x, ada, row(g_pre_ffn), row(g_post_ffn), w_fi, w_fo, tm, sub, chunks)
    return x
```

```python
import functools
import math

import jax
import jax.numpy as jnp
import numpy as np
from jax import lax
from jax.experimental import pallas as pl
from jax.experimental.pallas import tpu as pltpu

HEAD_DIM = 64
WINDOW = 128
A_Q_HEADS = 8
A_KV_HEADS = 2
A_GROUP = A_Q_HEADS // A_KV_HEADS
B_HEADS = 8
N_ADA = 6
ROPE_THETA = 10000.0
RMS_EPS = 1e-6
LANES = 128
MXU_DEPTH = 256
ROPE_FREQS = HEAD_DIM // 2
VMEM_LIMIT = 56 * 1024 * 1024

A_Q_W = A_Q_HEADS * HEAD_DIM
A_KV_W = A_KV_HEADS * HEAD_DIM
B_W = B_HEADS * HEAD_DIM
QKV_W = A_Q_W + 2 * A_KV_W + 3 * B_W
BF16 = jnp.bfloat16
F32 = jnp.float32
NEG_INF = float("-inf")
LOG2E = math.log2(math.e)
ONES_ROWS = 16
SKIP_LOG2 = 48.0
NORM_MARGIN = 1.02
NORM_ROWS = 256


def _params(*sem, flags=None):
    return pltpu.CompilerParams(dimension_semantics=sem, vmem_limit_bytes=VMEM_LIMIT, flags=flags)


def _rms_mod(x, g, scale, shift):
    ms = jnp.mean(x * x, axis=-1, keepdims=True)
    return (x * lax.rsqrt(ms + RMS_EPS)) * (g * (1.0 + scale)) + shift


def _rms(y, g):
    ms = jnp.mean(y * y, axis=-1, keepdims=True)
    return y * lax.rsqrt(ms + RMS_EPS) * g


def _ada_kernel(c_ref, w_ref, b_ref, o_ref):
    o_ref[...] = jnp.dot(c_ref[...], w_ref[...], preferred_element_type=F32) + b_ref[...]


def _ada(c, w, b):
    bsz, d = c.shape
    n = w.shape[1]
    tn = n // 4
    return pl.pallas_call(
        _ada_kernel,
        grid=(n // tn,),
        in_specs=[pl.BlockSpec((bsz, d), lambda j: (0, 0)),
                  pl.BlockSpec((d, tn), lambda j: (0, j)),
                  pl.BlockSpec((1, tn), lambda j: (0, j))],
        out_specs=pl.BlockSpec((bsz, tn), lambda j: (0, j)),
        out_shape=jax.ShapeDtypeStruct((bsz, n), F32),
        compiler_params=_params("arbitrary"),
        name="ada",
    )(c, w, b.reshape(1, n))


def _rope_table_kernel(pos_ref, invf_ref, cos_ref, sin_ref):
    ang = pos_ref[0] * invf_ref[...]
    cos_ref[0] = jnp.cos(ang)
    sin_ref[0] = jnp.sin(ang)


def _rope_tables(pos_c, inv_freq):
    bsz, rows, _ = pos_c.shape
    blk = pl.BlockSpec((1, rows, LANES), lambda b: (b, 0, 0))
    shape = jax.ShapeDtypeStruct((bsz, rows, LANES), F32)
    return pl.pallas_call(
        _rope_table_kernel,
        grid=(bsz,),
        in_specs=[blk, pl.BlockSpec((1, LANES), lambda b: (0, 0))],
        out_specs=[blk, blk],
        out_shape=[shape, shape],
        compiler_params=_params("parallel"),
        name="rope_tables",
    )(pos_c, inv_freq)


AUG_PARTS = 3


def _aug_base(head):
    return head * LANES + (HEAD_DIM if head % 2 == 0 else 0)


def _aug_placement():
    p = np.zeros((LANES, B_HEADS * LANES), np.float32)
    for head in range(B_HEADS):
        for part in range(AUG_PARTS):
            p[part * B_HEADS + head, _aug_base(head) + part] = 1.0
            p[part * B_HEADS + head, _aug_base(head) + AUG_PARTS + part] = -1.0
    return jnp.asarray(p, BF16)


def _inproj_kernel(x_ref, cos_ref, sin_ref, ada_ref, g_ref, w_ref, wvt_ref, bf_ref, place_ref, hsum_ref,
                   qa_ref, ka_ref, vat_ref, qb_ref, kb_ref, vbt_ref, stat_ref, carry_ref):
    i = pl.program_id(1)
    sub = vbt_ref.shape[3]
    lane = lax.broadcasted_iota(jnp.int32, (sub, LANES), 1)
    row = lax.broadcasted_iota(jnp.int32, (sub, LANES), 0)
    first_half = (lane % HEAD_DIM) < (HEAD_DIM // 2)
    low = lane < HEAD_DIM
    q_scale = LOG2E / math.sqrt(HEAD_DIM)

    @pl.when(i == 0)
    def _():
        carry_ref[...] = jnp.zeros(carry_ref.shape, F32)

    carry = carry_ref[0:1, :]
    for r in range(x_ref.shape[1] // sub):
        rows = slice(r * sub, (r + 1) * sub)
        h = _rms_mod(x_ref[0, rows, :], g_ref[...], ada_ref[0, 1:2, :], ada_ref[0, 0:1, :]).astype(BF16)

        def proj(lo, width):
            return lax.dot_general(h, w_ref[lo:lo + width, :], (((1,), (1,)), ((), ())),
                                   preferred_element_type=F32)

        def proj_t(lo, width):
            return lax.dot_general(wvt_ref[lo:lo + width, :], h, (((1,), (1,)), ((), ())),
                                   preferred_element_type=F32)

        t0 = i * x_ref.shape[1] + r * sub
        quarter = t0 // cos_ref.shape[1]
        t_rows = pl.ds(pl.multiple_of(t0 - quarter * cos_ref.shape[1], sub), sub)
        shift = (LANES - quarter * ROPE_FREQS) % LANES

        def spread(tab):
            y = jnp.where(lane < ROPE_FREQS, pltpu.roll(tab, shift, axis=1), 0.0)
            y = y + pltpu.roll(y, ROPE_FREQS, axis=1)
            return y + pltpu.roll(y, 2 * ROPE_FREQS, axis=1)

        cos = spread(cos_ref[0, t_rows, :])
        sin = spread(sin_ref[0, t_rows, :])
        sin_signed = jnp.where(first_half, -sin, sin)

        def rope(t):
            partner = jnp.where(first_half,
                                pltpu.roll(t, LANES - HEAD_DIM // 2, axis=1),
                                pltpu.roll(t, HEAD_DIM // 2, axis=1))
            return t * cos + partner * sin_signed

        kf = proj(A_Q_W + 2 * B_W, A_KV_W + LANES)
        ka_ref[0, rows, :] = rope(kf[:, :A_KV_W]).astype(BF16)
        fl = kf[:, A_KV_W:] + bf_ref[...]
        c = jnp.minimum(fl, 0.0) - jnp.log1p(jnp.exp(-jnp.abs(fl)))
        d = 1
        while d < sub:
            c = c + jnp.where(row >= d, pltpu.roll(c, d, axis=0), 0.0)
            d *= 2
        c = c + carry
        carry = c[sub - 1:sub, :]
        c = c * LOG2E
        hi = c.astype(BF16).astype(F32)
        r1 = c - hi
        mid = r1.astype(BF16).astype(F32)
        lo = r1 - mid
        packed = jnp.where(lane < B_HEADS, hi,
                           jnp.where(lane < 2 * B_HEADS, pltpu.roll(mid, B_HEADS, axis=1),
                                     jnp.where(lane < 3 * B_HEADS, pltpu.roll(lo, 2 * B_HEADS, axis=1), 0.0)))

        qa = proj(0, A_Q_W)
        for j in range(A_Q_W // LANES):
            qa_ref[0, rows, j * LANES:(j + 1) * LANES] = (
                rope(qa[:, j * LANES:(j + 1) * LANES]) * q_scale).astype(BF16)
        vat_ref[0, :, rows] = proj_t(0, A_KV_W).astype(BF16)
        vbt_ref[0, r] = proj_t(A_KV_W, B_W).astype(BF16)
        qb = proj(A_Q_W, B_W) * q_scale
        kb = proj(A_Q_W + B_W, B_W)
        aug = jnp.dot(packed.astype(BF16), place_ref[...], preferred_element_type=F32)
        def norm_bound(t):
            sq = t * t
            fold = sq[:NORM_ROWS]
            for g in range(1, sub // NORM_ROWS):
                fold = jnp.maximum(fold, sq[g * NORM_ROWS:(g + 1) * NORM_ROWS])
            per_head = jnp.dot(fold.astype(BF16), hsum_ref[...], preferred_element_type=F32)
            return jnp.max(per_head, axis=0, keepdims=True) * NORM_MARGIN

        stats = [norm_bound(qb), norm_bound(kb)]
        stats += [c[0:1, :], c[sub - 1:sub, :], jnp.zeros((4, LANES), F32)]
        stat_ref[0, r] = jnp.concatenate(stats, axis=0)
        for head in range(B_HEADS):
            base = _aug_base(head) % LANES
            own = low if head % 2 == 0 else jnp.logical_not(low)
            plus = (lane >= base) & (lane < base + AUG_PARTS)
            minus = (lane >= base + AUG_PARTS) & (lane < base + 2 * AUG_PARTS)
            a = aug[:, head * LANES:(head + 1) * LANES]
            src = slice(head // 2 * LANES, (head // 2 + 1) * LANES)
            group = slice(head * LANES, (head + 1) * LANES)
            qb_ref[0, rows, group] = jnp.where(own, qb[:, src],
                                               jnp.where(plus, a, jnp.where(minus, 1.0, 0.0))).astype(BF16)
            kb_ref[0, rows, group] = jnp.where(own, kb[:, src],
                                               jnp.where(minus, a, jnp.where(plus, 1.0, 0.0))).astype(BF16)
    carry_ref[0:1, :] = carry


def _in_proj(x, cos, sin, ada, g_pre, w_qk, w_vt, b_f, tm, sub):
    bsz, s, d = x.shape
    grid = (bsz, s // tm)
    tok = lambda w: pl.BlockSpec((1, tm, w), lambda b, i: (b, i, 0))
    const = lambda shape: pl.BlockSpec(shape, lambda b, i: (0,) * len(shape))
    act = lambda w: jax.ShapeDtypeStruct((bsz, s, w), BF16)
    place = _aug_placement()
    head_sum = jnp.asarray(np.kron(np.eye(B_HEADS, LANES), np.ones((HEAD_DIM, 1))), BF16)
    return pl.pallas_call(
        _inproj_kernel,
        grid=grid,
        in_specs=[tok(d), pl.BlockSpec((1,) + cos.shape[1:], lambda b, i: (b, 0, 0)),
                  pl.BlockSpec((1,) + sin.shape[1:], lambda b, i: (b, 0, 0)),
                  pl.BlockSpec((1, N_ADA, d), lambda b, i: (b, 0, 0)),
                  const((1, d)), _resident(w_qk.shape), _resident(w_vt.shape),
                  const((1, LANES)), const(place.shape), const(head_sum.shape)],
        out_specs=[tok(A_Q_W), tok(A_KV_W),
                   pl.BlockSpec((1, A_KV_W, tm), lambda b, i: (b, 0, i)),
                   tok(B_HEADS * LANES), tok(B_HEADS * LANES),
                   pl.BlockSpec((1, tm // sub, B_W, sub), lambda b, i: (b, i, 0, 0)),
                   pl.BlockSpec((1, tm // sub, 8, LANES), lambda b, i: (b, i, 0, 0))],
        out_shape=[act(A_Q_W), act(A_KV_W),
                   jax.ShapeDtypeStruct((bsz, A_KV_W, s), BF16),
                   act(B_HEADS * LANES), act(B_HEADS * LANES),
                   jax.ShapeDtypeStruct((bsz, s // sub, B_W, sub), BF16),
                   jax.ShapeDtypeStruct((bsz, s // sub, 8, LANES), F32)],
        scratch_shapes=[pltpu.VMEM((8, LANES), F32)],
        compiler_params=_params("parallel", "arbitrary"),
        name="in_proj",
    )(x, cos, sin, ada, g_pre, w_qk, w_vt, b_f, place, head_sum)


def _plan_kernel(st_ref, o_ref, *, ratio):
    nt = st_ref.shape[1]
    q_scale = LOG2E / math.sqrt(HEAD_DIM)
    qn = st_ref[0, 0, 0:1, :]
    kn = st_ref[0, 0, 1:2, :]
    for j in range(1, nt):
        qn = jnp.maximum(qn, st_ref[0, j, 0:1, :])
        kn = jnp.maximum(kn, st_ref[0, j, 1:2, :])
    spread = qn * (1.0 / q_scale) + kn * q_scale
    rows = []
    for blk in range(nt // ratio):
        limit = st_ref[0, blk * ratio, 2:3, :] + spread + SKIP_LOG2
        count = jnp.zeros((1, LANES), jnp.int32)
        prefix = jnp.ones((1, LANES), jnp.int32)
        for j in range(blk * ratio):
            prefix = prefix * (st_ref[0, j, 3:4, :] > limit).astype(jnp.int32)
            count = count + prefix
        rows.append(count)
    o_ref[0] = jnp.concatenate(rows, axis=0)


def _fox_plan(stats, ratio):
    bsz, nt, _, _ = stats.shape
    return pl.pallas_call(
        functools.partial(_plan_kernel, ratio=ratio),
        grid=(bsz,),
        in_specs=[pl.BlockSpec((1, nt, 8, LANES), lambda b: (b, 0, 0, 0))],
        out_specs=pl.BlockSpec((1, nt // ratio, LANES), lambda b: (b, 0, 0)),
        out_shape=jax.ShapeDtypeStruct((bsz, nt // ratio, LANES), jnp.int32),
        compiler_params=_params("parallel"),
        name="fox_plan",
    )(stats)


def _swa_kernel(q_ref, kp_ref, kc_ref, vtp_ref, vtc_ref, sink_ref, o_ref):
    i = pl.program_id(1)
    tq = q_ref.shape[1]
    nsub = tq // WINDOW
    ncol = A_Q_W // LANES
    lane_q = lax.broadcasted_iota(jnp.int32, (WINDOW, LANES), 1)
    low = lane_q < HEAD_DIM
    low_dim = lax.broadcasted_iota(jnp.int32, (LANES, WINDOW), 0) < HEAD_DIM
    k_loc = lax.broadcasted_iota(jnp.int32, (2 * WINDOW, WINDOW), 0)
    q_loc = lax.broadcasted_iota(jnp.int32, (2 * WINDOW, WINDOW), 1) + WINDOW
    rel = q_loc - k_loc
    band = (rel >= 0) & (rel < WINDOW)
    sink = sink_ref[...] * LOG2E
    ones = jnp.ones((ONES_ROWS, 2 * WINDOW), BF16)
    zero = jnp.zeros((WINDOW, LANES), BF16)

    def keys(j):
        if j == 0:
            return jnp.concatenate([kp_ref[0], kc_ref[0, :WINDOW, :]], axis=0)
        return kc_ref[0, (j - 1) * WINDOW:(j + 1) * WINDOW, :]

    def values_t(j):
        if j == 0:
            return jnp.concatenate([vtp_ref[0], vtc_ref[0, :, :WINDOW]], axis=1)
        return vtc_ref[0, :, (j - 1) * WINDOW:(j + 1) * WINDOW]

    def scores(j):
        q = q_ref[0, j * WINDOW:(j + 1) * WINDOW, :]
        parts = [jnp.where(low, q[:, c * LANES:(c + 1) * LANES], zero) for c in range(ncol)]
        parts += [jnp.where(low, zero, q[:, c * LANES:(c + 1) * LANES]) for c in range(ncol)]
        qs = jnp.concatenate(parts, axis=0)
        return lax.dot_general(keys(j), qs, (((1,), (1,)), ((), ())), preferred_element_type=F32)

    def softmax_pv(j, st):
        valid = band & ((k_loc >= WINDOW) | (i > 0)) if j == 0 else band
        ps, sink_terms = [], []
        for hh in range(A_Q_HEADS):
            cols = slice(hh * WINDOW, (hh + 1) * WINDOW)
            w = jnp.where(valid, st[:, cols], NEG_INF)
            m = jnp.maximum(jnp.max(w, axis=0, keepdims=True), sink[:, cols])
            ps.append(jnp.exp2(w - m).astype(BF16))
            sink_terms.append(jnp.exp2(sink[:, cols] - m))
        pt = jnp.concatenate(ps, axis=1)
        ot = jnp.dot(jnp.concatenate([values_t(j), ones], axis=0), pt, preferred_element_type=F32)
        for c in range(ncol):
            halves = []
            for hh in (c, ncol + c):
                cols = slice(hh * WINDOW, (hh + 1) * WINDOW)
                denom = ot[LANES:LANES + 1, cols] + sink_terms[hh]
                halves.append(ot[:LANES, cols] * (1.0 / denom))
            o_ref[0, j * WINDOW:(j + 1) * WINDOW, c * LANES:(c + 1) * LANES] = (
                jnp.transpose(jnp.where(low_dim, halves[0], halves[1])).astype(BF16))

    st = scores(0)
    for j in range(nsub):
        st_next = scores(j + 1) if j + 1 < nsub else None
        softmax_pv(j, st)
        st = st_next


def _swa(qa, ka, vat, sinks, tq):
    bsz, s, _ = qa.shape
    r = tq // WINDOW
    cur = lambda b, i: (b, i, 0)
    return pl.pallas_call(
        _swa_kernel,
        grid=(bsz, s // tq),
        in_specs=[pl.BlockSpec((1, tq, A_Q_W), cur),
                  pl.BlockSpec((1, WINDOW, A_KV_W), lambda b, i: (b, jnp.maximum(i * r - 1, 0), 0)),
                  pl.BlockSpec((1, tq, A_KV_W), cur),
                  pl.BlockSpec((1, A_KV_W, WINDOW), lambda b, i: (b, 0, jnp.maximum(i * r - 1, 0))),
                  pl.BlockSpec((1, A_KV_W, tq), lambda b, i: (b, 0, i)),
                  pl.BlockSpec((1, A_Q_HEADS * WINDOW), lambda b, i: (0, 0))],
        out_specs=pl.BlockSpec((1, tq, A_Q_W), cur),
        out_shape=jax.ShapeDtypeStruct((bsz, s, A_Q_W), BF16),
        compiler_params=_params("parallel", "parallel"),
        name="swa",
    )(qa, ka, ka, vat, vat, sinks)


def _fox_kernel(skip_ref, q_ref, k_ref, vt_ref, o_ref, m_ref, acc_ref, sta_ref, stb_ref):
    qi = pl.program_id(2)
    tq = q_ref.shape[1]
    tk = vt_ref.shape[3]
    nchunk = tq // LANES
    assert tq == 2 * tk
    m_ref[...] = jnp.full(m_ref.shape, NEG_INF, F32)
    acc_ref[...] = jnp.zeros(acc_ref.shape, F32)
    ones = jnp.ones((ONES_ROWS, tk), BF16)

    def scores(kj, st_ref, c0=0, c1=nchunk):
        start = pl.multiple_of(kj * tk, tk)
        for e in range(2):
            group = slice(e * LANES, (e + 1) * LANES)
            st_ref[e, :, c0 * LANES:c1 * LANES] = lax.dot_general(
                k_ref[0, pl.ds(start, tk), group], q_ref[0, c0 * LANES:c1 * LANES, group],
                (((1,), (1,)), ((), ())), preferred_element_type=F32)

    def softmax_pv(kj, st_ref, first_query=None, c0=0, c1=nchunk):
        for e in range(2):
            vta = jnp.concatenate([vt_ref[0, kj, e * HEAD_DIM:(e + 1) * HEAD_DIM, :], ones], axis=0)
            ps, alphas = [], []
            for c in range(c0, c1):
                cols = slice(c * LANES, (c + 1) * LANES)
                off = None if first_query is None else first_query + c * LANES
                masked = off is not None and off < tk
                live = min(off + LANES, tk) if masked else tk
                z = st_ref[e, :live, cols]
                if masked:
                    causal = (lax.broadcasted_iota(jnp.int32, (live, LANES), 0)
                              <= lax.broadcasted_iota(jnp.int32, (live, LANES), 1) + off)
                    z = jnp.where(causal, z, NEG_INF)
                m_old = m_ref[e, :, cols]
                m_new = jnp.maximum(m_old, jnp.max(z, axis=0, keepdims=True))
                p = jnp.exp2(z - m_new).astype(BF16)
                if live < tk:
                    p = jnp.concatenate([p, jnp.zeros((tk - live, LANES), BF16)], axis=0)
                ps.append(p)
                alphas.append(jnp.exp2(m_old - m_new))
                m_ref[e, :, cols] = m_new
            pt = jnp.concatenate(ps, axis=1)
            alpha = jnp.concatenate(alphas, axis=1)
            span = slice(c0 * LANES, c1 * LANES)
            acc_ref[e, :, span] = alpha * acc_ref[e, :, span] + jnp.dot(vta, pt, preferred_element_type=F32)

    half = tk // LANES
    base = ((pl.program_id(0) * pl.num_programs(2) + qi) * 2 * B_HEADS + 2 * pl.program_id(1))
    lead = jnp.minimum(skip_ref[base], skip_ref[base + 1])
    first = jnp.minimum(jnp.minimum(skip_ref[base + B_HEADS], skip_ref[base + B_HEADS + 1]), 2 * qi)
    first = jnp.maximum(first, lead)

    def early_half(u, carry):
        scores(u, stb_ref, c1=half)
        softmax_pv(u, stb_ref, c1=half)
        return carry

    lax.fori_loop(lead, first, early_half, 0)
    first_pair = (first + 1) // 2
    odd_start = first % 2 == 1
    second = half

    @pl.when(odd_start)
    def _():
        scores(first, stb_ref)
        scores(first + 1, sta_ref)
        softmax_pv(first, stb_ref)

    @pl.when(jnp.logical_not(odd_start))
    def _():
        scores(first, sta_ref)

    def pair(t, carry):
        u = 2 * t
        scores(u + 1, stb_ref)
        softmax_pv(u, sta_ref)
        scores(u + 2, sta_ref)
        softmax_pv(u + 1, stb_ref)
        return carry

    lax.fori_loop(first_pair, qi, pair, 0)
    scores(2 * qi + 1, stb_ref, c0=second)
    softmax_pv(2 * qi, sta_ref, first_query=0)
    softmax_pv(2 * qi + 1, stb_ref, first_query=-tk, c0=second)

    out_t = jnp.concatenate(
        [acc_ref[e, :HEAD_DIM, :] * (1.0 / acc_ref[e, HEAD_DIM:HEAD_DIM + 1, :]) for e in range(2)], axis=0)
    o_ref[0] = jnp.transpose(out_t).astype(BF16)


def _fox(skip, qb, kb, vbt, tq):
    bsz, s, _ = qb.shape
    _, nk, _, tk = vbt.shape
    npair = B_HEADS // 2
    return pl.pallas_call(
        _fox_kernel,
        grid_spec=pltpu.PrefetchScalarGridSpec(
            num_scalar_prefetch=1,
            grid=(bsz, npair, s // tq),
            in_specs=[pl.BlockSpec((1, tq, 2 * LANES), lambda b, hp, i, skip: (b, i, hp)),
                      pl.BlockSpec((1, s, 2 * LANES), lambda b, hp, i, skip: (b, 0, hp)),
                      pl.BlockSpec((1, nk, 2 * HEAD_DIM, tk), lambda b, hp, i, skip: (b, 0, hp, 0))],
            out_specs=pl.BlockSpec((1, tq, 2 * HEAD_DIM), lambda b, hp, i, skip: (b, i, hp)),
            scratch_shapes=[pltpu.VMEM((2, 1, tq), F32),
                            pltpu.VMEM((2, HEAD_DIM + ONES_ROWS, tq), F32),
                            pltpu.VMEM((2, tk, tq), F32),
                            pltpu.VMEM((2, tk, tq), F32)]),
        out_shape=jax.ShapeDtypeStruct((bsz, s, B_W), BF16),
        compiler_params=_params("parallel", "arbitrary", "arbitrary"),
        name="fox",
    )(skip, qb, kb, vbt)


def _resident(shape):
    return pl.BlockSpec(shape, lambda b, i: (0,) * len(shape), pipeline_mode=pl.Buffered(1))


def _post_kernel(x_ref, oa_ref, ob_ref, ada_ref, gpre_ref, gpost_ref, wg_ref, wa_ref, wb_ref,
                 wo_ref, o_ref, *, sub):
    d = x_ref.shape[2]
    for r in range(x_ref.shape[1] // sub):
        rows = slice(r * sub, (r + 1) * sub)
        x = x_ref[0, rows, :]
        h = _rms_mod(x, gpre_ref[...], ada_ref[0, 1:2, :], ada_ref[0, 0:1, :]).astype(BF16)
        a = jnp.dot(oa_ref[0, rows, :], wa_ref[...], preferred_element_type=F32)
        b = jnp.dot(ob_ref[0, rows, :], wb_ref[...], preferred_element_type=F32)
        nt = (((1,), (1,)), ((), ()))
        ga = jax.nn.sigmoid(lax.dot_general(h, wg_ref[:d, :], nt, preferred_element_type=F32))
        gb = jax.nn.sigmoid(lax.dot_general(h, wg_ref[d:, :], nt, preferred_element_type=F32))
        merged = (ga * a + gb * b).astype(BF16)
        y = jnp.dot(merged, wo_ref[...], preferred_element_type=F32)
        o_ref[0, rows, :] = x + ada_ref[0, 2:3, :] * _rms(y, gpost_ref[...])


def _post(x, oa, ob, ada, g_pre, g_post, wg, wa, wb, wo, tm, sub):
    bsz, s, d = x.shape
    tok = lambda w: pl.BlockSpec((1, tm, w), lambda b, i: (b, i, 0))
    const = lambda shape: pl.BlockSpec(shape, lambda b, i: (0,) * len(shape))
    return pl.pallas_call(
        functools.partial(_post_kernel, sub=sub),
        grid=(bsz, s // tm),
        in_specs=[tok(d), tok(A_Q_W), tok(B_W),
                  pl.BlockSpec((1, N_ADA, d), lambda b, i: (b, 0, 0)),
                  const((1, d)), const((1, d)),
                  _resident(wg.shape), _resident(wa.shape), _resident(wb.shape), _resident(wo.shape)],
        out_specs=tok(d),
        out_shape=jax.ShapeDtypeStruct((bsz, s, d), F32),
        compiler_params=_params("parallel", "parallel"),
        name="post",
    )(x, oa, ob, ada, g_pre, g_post, wg, wa, wb, wo)


def _ffn_kernel(x_ref, ada_ref, gpre_ref, gpost_ref, wi_ref, wo_ref, o_ref, *, sub, chunks):
    d_ff = wo_ref.shape[0]
    for r in range(x_ref.shape[1] // sub):
        rows = slice(r * sub, (r + 1) * sub)
        x = x_ref[0, rows, :]
        h = _rms_mod(x, gpre_ref[...], ada_ref[0, 4:5, :], ada_ref[0, 3:4, :]).astype(BF16)
        y = jnp.zeros(x.shape, F32)
        lo = 0
        for width in chunks:
            g = jnp.dot(h, wi_ref[:, lo:lo + width], preferred_element_type=F32)
            u = jnp.dot(h, wi_ref[:, d_ff + lo:d_ff + lo + width], preferred_element_type=F32)
            act = (g * jax.nn.sigmoid(g) * u).astype(BF16)
            y = y + jnp.dot(act, wo_ref[lo:lo + width, :], preferred_element_type=F32)
            lo += width
        o_ref[0, rows, :] = x + ada_ref[0, 5:6, :] * _rms(y, gpost_ref[...])


def _ffn(x, ada, g_pre, g_post, wi, wo, tm, sub, chunks):
    bsz, s, d = x.shape
    tok = pl.BlockSpec((1, tm, d), lambda b, i: (b, i, 0))
    const = lambda shape: pl.BlockSpec(shape, lambda b, i: (0,) * len(shape))
    return pl.pallas_call(
        functools.partial(_ffn_kernel, sub=sub, chunks=chunks),
        grid=(bsz, s // tm),
        in_specs=[tok, pl.BlockSpec((1, N_ADA, d), lambda b, i: (b, 0, 0)),
                  const((1, d)), const((1, d)), _resident(wi.shape), _resident(wo.shape)],
        out_specs=tok,
        out_shape=jax.ShapeDtypeStruct((bsz, s, d), F32),
        compiler_params=_params("parallel", "parallel"),
        name="ffn",
    )(x, ada, g_pre, g_post, wi, wo)


def _pair_heads(w, axis):
    shape = w.shape
    pre, post = shape[:axis], shape[axis + 1:]
    w = w.reshape(pre + (A_KV_HEADS, A_GROUP, HEAD_DIM) + post)
    w = jnp.swapaxes(w, axis, axis + 1)
    return w.reshape(shape)


def kernel(x, c, positions, w_ada, b_ada, g_pre_mix, g_post_mix, w_in, b_f, sinks, w_branch_a,
           w_branch_b, w_out, g_pre_ffn, g_post_ffn, w_ffn_in, w_ffn_out):
    bsz, s, d = x.shape
    depth = w_ada.shape[0]
    tm = min(1024, s)
    sub = min(512, s)
    rep = LANES // ROPE_FREQS
    assert (s // rep) % sub == 0
    half = jnp.arange(0, HEAD_DIM, 2, dtype=F32) / HEAD_DIM
    inv_freq = jnp.tile(1.0 / (ROPE_THETA ** half), rep).reshape(1, LANES)
    pos_c = jnp.swapaxes(positions.astype(F32).reshape(bsz, rep, s // rep), 1, 2)
    cos, sin = _rope_tables(jnp.repeat(pos_c, ROPE_FREQS, axis=2), inv_freq)
    for l in range(depth):
        w_t = jnp.swapaxes(w_in[l], 0, 1)
        o_va, o_qb, o_vb = A_Q_W + A_KV_W, A_Q_W + 2 * A_KV_W, A_Q_W + 2 * A_KV_W + 2 * B_W
        w_f = jnp.pad(w_t[QKV_W:QKV_W + B_HEADS], ((0, LANES - B_HEADS), (0, 0)))
        w_qk = jnp.concatenate([_pair_heads(w_t[:A_Q_W], 0), w_t[o_qb:o_vb], w_t[A_Q_W:o_va], w_f],
                               axis=0).astype(BF16)
        w_vt = jnp.concatenate([w_t[o_va:o_qb], w_t[o_vb:QKV_W]], axis=0).astype(BF16)
        w_g = w_t[QKV_W + B_HEADS:].astype(BF16)
        bf_row = jnp.pad(b_f[l], (0, LANES - B_HEADS)).reshape(1, LANES)
        sink_row = jnp.repeat(sinks[l], WINDOW).reshape(1, A_Q_HEADS * WINDOW)
        w_a = _pair_heads(w_branch_a[l], 0).astype(BF16)
        w_b = w_branch_b[l].astype(BF16)
        w_o = w_out[l].astype(BF16)
        w_fi = w_ffn_in[l].astype(BF16)
        w_fo = w_ffn_out[l].astype(BF16)
        row = lambda g: g[l].reshape(1, d)

        ada = _ada(c, w_ada[l], b_ada[l]).reshape(bsz, N_ADA, d)
        qa, ka, vat, qb, kb, vbt, stats = _in_proj(x, cos, sin, ada, row(g_pre_mix), w_qk, w_vt, bf_row,
                                                   tm, sub)
        o_a = _swa(qa, ka, vat, sink_row, tm)
        skip = _fox_plan(stats, 1)[:, :, :B_HEADS].reshape(-1)
        o_b = _fox(skip, qb, kb, vbt, 2 * sub)
        x = _post(x, o_a, o_b, ada, row(g_pre_mix), row(g_post_mix), w_g, w_a, w_b, w_o, tm, sub)
        n_tiles = w_fo.shape[0] // MXU_DEPTH
        chunks = ((n_tiles + 1) // 2 * MXU_DEPTH, n_tiles // 2 * MXU_DEPTH)
        x = _ffn(x, ada, row(g_pre_ffn), row(g_post_ffn), w_fi, w_fo, tm, sub, chunks)
    return x
```

```python
import functools
import math

import jax
import jax.numpy as jnp
import numpy as np
from jax import lax
from jax.experimental import pallas as pl
from jax.experimental.pallas import tpu as pltpu

HEAD_DIM = 64
WINDOW = 128
A_Q_HEADS = 8
A_KV_HEADS = 2
A_GROUP = A_Q_HEADS // A_KV_HEADS
B_HEADS = 8
N_ADA = 6
ROPE_THETA = 10000.0
RMS_EPS = 1e-6
LANES = 128
MXU_DEPTH = 256
ROPE_FREQS = HEAD_DIM // 2
VMEM_LIMIT = 56 * 1024 * 1024

A_Q_W = A_Q_HEADS * HEAD_DIM
A_KV_W = A_KV_HEADS * HEAD_DIM
B_W = B_HEADS * HEAD_DIM
QKV_W = A_Q_W + 2 * A_KV_W + 3 * B_W
BF16 = jnp.bfloat16
F32 = jnp.float32
NEG_INF = float("-inf")
LOG2E = math.log2(math.e)
ONES_ROWS = 16
SKIP_LOG2 = 48.0
NORM_MARGIN = 1.02
NORM_ROWS = 256


def _params(*sem, flags=None):
    return pltpu.CompilerParams(dimension_semantics=sem, vmem_limit_bytes=VMEM_LIMIT, flags=flags)


def _rms_mod(x, g, scale, shift):
    ms = jnp.mean(x * x, axis=-1, keepdims=True)
    return (x * lax.rsqrt(ms + RMS_EPS)) * (g * (1.0 + scale)) + shift


def _rms(y, g):
    ms = jnp.mean(y * y, axis=-1, keepdims=True)
    return y * lax.rsqrt(ms + RMS_EPS) * g


def _ada_kernel(c_ref, w_ref, b_ref, o_ref):
    o_ref[...] = jnp.dot(c_ref[...], w_ref[...], preferred_element_type=F32) + b_ref[...]


def _ada(c, w, b):
    bsz, d = c.shape
    n = w.shape[1]
    tn = n // 4
    return pl.pallas_call(
        _ada_kernel,
        grid=(n // tn,),
        in_specs=[pl.BlockSpec((bsz, d), lambda j: (0, 0)),
                  pl.BlockSpec((d, tn), lambda j: (0, j)),
                  pl.BlockSpec((1, tn), lambda j: (0, j))],
        out_specs=pl.BlockSpec((bsz, tn), lambda j: (0, j)),
        out_shape=jax.ShapeDtypeStruct((bsz, n), F32),
        compiler_params=_params("arbitrary"),
        name="ada",
    )(c, w, b.reshape(1, n))


def _rope_table_kernel(pos_ref, invf_ref, cos_ref, sin_ref):
    ang = pos_ref[0] * invf_ref[...]
    cos_ref[0] = jnp.cos(ang)
    sin_ref[0] = jnp.sin(ang)


def _rope_tables(pos_c, inv_freq):
    bsz, rows, _ = pos_c.shape
    blk = pl.BlockSpec((1, rows, LANES), lambda b: (b, 0, 0))
    shape = jax.ShapeDtypeStruct((bsz, rows, LANES), F32)
    return pl.pallas_call(
        _rope_table_kernel,
        grid=(bsz,),
        in_specs=[blk, pl.BlockSpec((1, LANES), lambda b: (0, 0))],
        out_specs=[blk, blk],
        out_shape=[shape, shape],
        compiler_params=_params("parallel"),
        name="rope_tables",
    )(pos_c, inv_freq)


AUG_PARTS = 3


def _aug_base(head):
    return head * LANES + (HEAD_DIM if head % 2 == 0 else 0)


def _aug_placement():
    p = np.zeros((LANES, B_HEADS * LANES), np.float32)
    for head in range(B_HEADS):
        for part in range(AUG_PARTS):
            p[part * B_HEADS + head, _aug_base(head) + part] = 1.0
            p[part * B_HEADS + head, _aug_base(head) + AUG_PARTS + part] = -1.0
    return jnp.asarray(p, BF16)


def _inproj_kernel(x_ref, cos_ref, sin_ref, ada_ref, g_ref, w_ref, wvt_ref, bf_ref, place_ref, hsum_ref,
                   qa_ref, ka_ref, vat_ref, qb_ref, kb_ref, vbt_ref, stat_ref, carry_ref):
    i = pl.program_id(1)
    sub = vbt_ref.shape[3]
    lane = lax.broadcasted_iota(jnp.int32, (sub, LANES), 1)
    row = lax.broadcasted_iota(jnp.int32, (sub, LANES), 0)
    first_half = (lane % HEAD_DIM) < (HEAD_DIM // 2)
    low = lane < HEAD_DIM
    q_scale = LOG2E / math.sqrt(HEAD_DIM)

    @pl.when(i == 0)
    def _():
        carry_ref[...] = jnp.zeros(carry_ref.shape, F32)

    carry = carry_ref[0:1, :]
    for r in range(x_ref.shape[1] // sub):
        rows = slice(r * sub, (r + 1) * sub)
        h = _rms_mod(x_ref[0, rows, :], g_ref[...], ada_ref[0, 1:2, :], ada_ref[0, 0:1, :]).astype(BF16)

        def proj(lo, width):
            return lax.dot_general(h, w_ref[lo:lo + width, :], (((1,), (1,)), ((), ())),
                                   preferred_element_type=F32)

        def proj_t(lo, width):
            return lax.dot_general(wvt_ref[lo:lo + width, :], h, (((1,), (1,)), ((), ())),
                                   preferred_element_type=F32)

        t0 = i * x_ref.shape[1] + r * sub
        quarter = t0 // cos_ref.shape[1]
        t_rows = pl.ds(pl.multiple_of(t0 - quarter * cos_ref.shape[1], sub), sub)
        shift = (LANES - quarter * ROPE_FREQS) % LANES

        def spread(tab):
            y = jnp.where(lane < ROPE_FREQS, pltpu.roll(tab, shift, axis=1), 0.0)
            y = y + pltpu.roll(y, ROPE_FREQS, axis=1)
            return y + pltpu.roll(y, 2 * ROPE_FREQS, axis=1)

        cos = spread(cos_ref[0, t_rows, :])
        sin = spread(sin_ref[0, t_rows, :])
        sin_signed = jnp.where(first_half, -sin, sin)

        def rope(t):
            partner = jnp.where(first_half,
                                pltpu.roll(t, LANES - HEAD_DIM // 2, axis=1),
                                pltpu.roll(t, HEAD_DIM // 2, axis=1))
            return t * cos + partner * sin_signed

        kf = proj(A_Q_W + 2 * B_W, A_KV_W + LANES)
        ka_ref[0, rows, :] = rope(kf[:, :A_KV_W]).astype(BF16)
        fl = kf[:, A_KV_W:] + bf_ref[...]
        c = jnp.minimum(fl, 0.0) - jnp.log1p(jnp.exp(-jnp.abs(fl)))
        d = 1
        while d < sub:
            c = c + jnp.where(row >= d, pltpu.roll(c, d, axis=0), 0.0)
            d *= 2
        c = c + carry
        carry = c[sub - 1:sub, :]
        c = c * LOG2E
        hi = c.astype(BF16).astype(F32)
        r1 = c - hi
        mid = r1.astype(BF16).astype(F32)
        lo = r1 - mid
        packed = jnp.where(lane < B_HEADS, hi,
                           jnp.where(lane < 2 * B_HEADS, pltpu.roll(mid, B_HEADS, axis=1),
                                     jnp.where(lane < 3 * B_HEADS, pltpu.roll(lo, 2 * B_HEADS, axis=1), 0.0)))

        qa = proj(0, A_Q_W)
        for j in range(A_Q_W // LANES):
            qa_ref[0, rows, j * LANES:(j + 1) * LANES] = (
                rope(qa[:, j * LANES:(j + 1) * LANES]) * q_scale).astype(BF16)
        vat_ref[0, :, rows] = proj_t(0, A_KV_W).astype(BF16)
        vbt_ref[0, r] = proj_t(A_KV_W, B_W).astype(BF16)
        qb = proj(A_Q_W, B_W) * q_scale
        kb = proj(A_Q_W + B_W, B_W)
        aug = jnp.dot(packed.astype(BF16), place_ref[...], preferred_element_type=F32)
        def norm_bound(t):
            sq = t * t
            fold = sq[:NORM_ROWS]
            for g in range(1, sub // NORM_ROWS):
                fold = jnp.maximum(fold, sq[g * NORM_ROWS:(g + 1) * NORM_ROWS])
            per_head = jnp.dot(fold.astype(BF16), hsum_ref[...], preferred_element_type=F32)
            return jnp.max(per_head, axis=0, keepdims=True) * NORM_MARGIN

        stats = [norm_bound(qb), norm_bound(kb)]
        stats += [c[0:1, :], c[sub - 1:sub, :], jnp.zeros((4, LANES), F32)]
        stat_ref[0, r] = jnp.concatenate(stats, axis=0)
        for head in range(B_HEADS):
            base = _aug_base(head) % LANES
            own = low if head % 2 == 0 else jnp.logical_not(low)
            plus = (lane >= base) & (lane < base + AUG_PARTS)
            minus = (lane >= base + AUG_PARTS) & (lane < base + 2 * AUG_PARTS)
            a = aug[:, head * LANES:(head + 1) * LANES]
            src = slice(head // 2 * LANES, (head // 2 + 1) * LANES)
            group = slice(head * LANES, (head + 1) * LANES)
            qb_ref[0, rows, group] = jnp.where(own, qb[:, src],
                                               jnp.where(plus, a, jnp.where(minus, 1.0, 0.0))).astype(BF16)
            kb_ref[0, rows, group] = jnp.where(own, kb[:, src],
                                               jnp.where(minus, a, jnp.where(plus, 1.0, 0.0))).astype(BF16)
    carry_ref[0:1, :] = carry


def _in_proj(x, cos, sin, ada, g_pre, w_qk, w_vt, b_f, tm, sub):
    bsz, s, d = x.shape
    grid = (bsz, s // tm)
    tok = lambda w: pl.BlockSpec((1, tm, w), lambda b, i: (b, i, 0))
    const = lambda shape: pl.BlockSpec(shape, lambda b, i: (0,) * len(shape))
    act = lambda w: jax.ShapeDtypeStruct((bsz, s, w), BF16)
    place = _aug_placement()
    head_sum = jnp.asarray(np.kron(np.eye(B_HEADS, LANES), np.ones((HEAD_DIM, 1))), BF16)
    return pl.pallas_call(
        _inproj_kernel,
        grid=grid,
        in_specs=[tok(d), pl.BlockSpec((1,) + cos.shape[1:], lambda b, i: (b, 0, 0)),
                  pl.BlockSpec((1,) + sin.shape[1:], lambda b, i: (b, 0, 0)),
                  pl.BlockSpec((1, N_ADA, d), lambda b, i: (b, 0, 0)),
                  const((1, d)), _resident(w_qk.shape), _resident(w_vt.shape),
                  const((1, LANES)), const(place.shape), const(head_sum.shape)],
        out_specs=[tok(A_Q_W), tok(A_KV_W),
                   pl.BlockSpec((1, A_KV_W, tm), lambda b, i: (b, 0, i)),
                   tok(B_HEADS * LANES), tok(B_HEADS * LANES),
                   pl.BlockSpec((1, tm // sub, B_W, sub), lambda b, i: (b, i, 0, 0)),
                   pl.BlockSpec((1, tm // sub, 8, LANES), lambda b, i: (b, i, 0, 0))],
        out_shape=[act(A_Q_W), act(A_KV_W),
                   jax.ShapeDtypeStruct((bsz, A_KV_W, s), BF16),
                   act(B_HEADS * LANES), act(B_HEADS * LANES),
                   jax.ShapeDtypeStruct((bsz, s // sub, B_W, sub), BF16),
                   jax.ShapeDtypeStruct((bsz, s // sub, 8, LANES), F32)],
        scratch_shapes=[pltpu.VMEM((8, LANES), F32)],
        compiler_params=_params("parallel", "arbitrary"),
        name="in_proj",
    )(x, cos, sin, ada, g_pre, w_qk, w_vt, b_f, place, head_sum)


def _plan_kernel(st_ref, o_ref, *, ratio):
    nt = st_ref.shape[1]
    q_scale = LOG2E / math.sqrt(HEAD_DIM)
    qn = st_ref[0, 0, 0:1, :]
    kn = st_ref[0, 0, 1:2, :]
    for j in range(1, nt):
        qn = jnp.maximum(qn, st_ref[0, j, 0:1, :])
        kn = jnp.maximum(kn, st_ref[0, j, 1:2, :])
    spread = qn * (1.0 / q_scale) + kn * q_scale
    rows = []
    for blk in range(nt // ratio):
        limit = st_ref[0, blk * ratio, 2:3, :] + spread + SKIP_LOG2
        count = jnp.zeros((1, LANES), jnp.int32)
        prefix = jnp.ones((1, LANES), jnp.int32)
        for j in range(blk * ratio):
            prefix = prefix * (st_ref[0, j, 3:4, :] > limit).astype(jnp.int32)
            count = count + prefix
        rows.append(count)
    o_ref[0] = jnp.concatenate(rows, axis=0)


def _fox_plan(stats, ratio):
    bsz, nt, _, _ = stats.shape
    return pl.pallas_call(
        functools.partial(_plan_kernel, ratio=ratio),
        grid=(bsz,),
        in_specs=[pl.BlockSpec((1, nt, 8, LANES), lambda b: (b, 0, 0, 0))],
        out_specs=pl.BlockSpec((1, nt // ratio, LANES), lambda b: (b, 0, 0)),
        out_shape=jax.ShapeDtypeStruct((bsz, nt // ratio, LANES), jnp.int32),
        compiler_params=_params("parallel"),
        name="fox_plan",
    )(stats)


def _swa_kernel(q_ref, kp_ref, kc_ref, vtp_ref, vtc_ref, sink_ref, o_ref):
    i = pl.program_id(1)
    tq = q_ref.shape[1]
    nsub = tq // WINDOW
    ncol = A_Q_W // LANES
    lane_q = lax.broadcasted_iota(jnp.int32, (WINDOW, LANES), 1)
    low = lane_q < HEAD_DIM
    low_dim = lax.broadcasted_iota(jnp.int32, (LANES, WINDOW), 0) < HEAD_DIM
    k_loc = lax.broadcasted_iota(jnp.int32, (2 * WINDOW, WINDOW), 0)
    q_loc = lax.broadcasted_iota(jnp.int32, (2 * WINDOW, WINDOW), 1) + WINDOW
    rel = q_loc - k_loc
    band = (rel >= 0) & (rel < WINDOW)
    sink = sink_ref[...] * LOG2E
    ones = jnp.ones((ONES_ROWS, 2 * WINDOW), BF16)
    zero = jnp.zeros((WINDOW, LANES), BF16)

    def keys(j):
        if j == 0:
            return jnp.concatenate([kp_ref[0], kc_ref[0, :WINDOW, :]], axis=0)
        return kc_ref[0, (j - 1) * WINDOW:(j + 1) * WINDOW, :]

    def values_t(j):
        if j == 0:
            return jnp.concatenate([vtp_ref[0], vtc_ref[0, :, :WINDOW]], axis=1)
        return vtc_ref[0, :, (j - 1) * WINDOW:(j + 1) * WINDOW]

    def scores(j):
        q = q_ref[0, j * WINDOW:(j + 1) * WINDOW, :]
        parts = [jnp.where(low, q[:, c * LANES:(c + 1) * LANES], zero) for c in range(ncol)]
        parts += [jnp.where(low, zero, q[:, c * LANES:(c + 1) * LANES]) for c in range(ncol)]
        qs = jnp.concatenate(parts, axis=0)
        return lax.dot_general(keys(j), qs, (((1,), (1,)), ((), ())), preferred_element_type=F32)

    def softmax_pv(j, st):
        valid = band & ((k_loc >= WINDOW) | (i > 0)) if j == 0 else band
        ps, sink_terms = [], []
        for hh in range(A_Q_HEADS):
            cols = slice(hh * WINDOW, (hh + 1) * WINDOW)
            w = jnp.where(valid, st[:, cols], NEG_INF)
            m = jnp.maximum(jnp.max(w, axis=0, keepdims=True), sink[:, cols])
            ps.append(jnp.exp2(w - m).astype(BF16))
            sink_terms.append(jnp.exp2(sink[:, cols] - m))
        pt = jnp.concatenate(ps, axis=1)
        ot = jnp.dot(jnp.concatenate([values_t(j), ones], axis=0), pt, preferred_element_type=F32)
        for c in range(ncol):
            halves = []
            for hh in (c, ncol + c):
                cols = slice(hh * WINDOW, (hh + 1) * WINDOW)
                denom = ot[LANES:LANES + 1, cols] + sink_terms[hh]
                halves.append(ot[:LANES, cols] * (1.0 / denom))
            o_ref[0, j * WINDOW:(j + 1) * WINDOW, c * LANES:(c + 1) * LANES] = (
                jnp.transpose(jnp.where(low_dim, halves[0], halves[1])).astype(BF16))

    st = scores(0)
    for j in range(nsub):
        st_next = scores(j + 1) if j + 1 < nsub else None
        softmax_pv(j, st)
        st = st_next


def _swa(qa, ka, vat, sinks, tq):
    bsz, s, _ = qa.shape
    r = tq // WINDOW
    cur = lambda b, i: (b, i, 0)
    return pl.pallas_call(
        _swa_kernel,
        grid=(bsz, s // tq),
        in_specs=[pl.BlockSpec((1, tq, A_Q_W), cur),
                  pl.BlockSpec((1, WINDOW, A_KV_W), lambda b, i: (b, jnp.maximum(i * r - 1, 0), 0)),
                  pl.BlockSpec((1, tq, A_KV_W), cur),
                  pl.BlockSpec((1, A_KV_W, WINDOW), lambda b, i: (b, 0, jnp.maximum(i * r - 1, 0))),
                  pl.BlockSpec((1, A_KV_W, tq), lambda b, i: (b, 0, i)),
                  pl.BlockSpec((1, A_Q_HEADS * WINDOW), lambda b, i: (0, 0))],
        out_specs=pl.BlockSpec((1, tq, A_Q_W), cur),
        out_shape=jax.ShapeDtypeStruct((bsz, s, A_Q_W), BF16),
        compiler_params=_params("parallel", "parallel"),
        name="swa",
    )(qa, ka, ka, vat, vat, sinks)


def _fox_kernel(skip_ref, q_ref, k_ref, vt_ref, o_ref, m_ref, acc_ref, sta_ref, stb_ref):
    qi = pl.program_id(2)
    tq = q_ref.shape[1]
    tk = vt_ref.shape[3]
    nchunk = tq // LANES
    assert tq == 2 * tk
    m_ref[...] = jnp.full(m_ref.shape, NEG_INF, F32)
    acc_ref[...] = jnp.zeros(acc_ref.shape, F32)
    ones = jnp.ones((ONES_ROWS, tk), BF16)

    def scores(kj, st_ref, c0=0, c1=nchunk):
        start = pl.multiple_of(kj * tk, tk)
        for e in range(2):
            group = slice(e * LANES, (e + 1) * LANES)
            st_ref[e, :, c0 * LANES:c1 * LANES] = lax.dot_general(
                k_ref[0, pl.ds(start, tk), group], q_ref[0, c0 * LANES:c1 * LANES, group],
                (((1,), (1,)), ((), ())), preferred_element_type=F32)

    def softmax_pv(kj, st_ref, first_query=None, c0=0, c1=nchunk):
        for e in range(2):
            vta = jnp.concatenate([vt_ref[0, kj, e * HEAD_DIM:(e + 1) * HEAD_DIM, :], ones], axis=0)
            ps, alphas = [], []
            for c in range(c0, c1):
                cols = slice(c * LANES, (c + 1) * LANES)
                off = None if first_query is None else first_query + c * LANES
                masked = off is not None and off < tk
                live = min(off + LANES, tk) if masked else tk
                z = st_ref[e, :live, cols]
                if masked:
                    causal = (lax.broadcasted_iota(jnp.int32, (live, LANES), 0)
                              <= lax.broadcasted_iota(jnp.int32, (live, LANES), 1) + off)
                    z = jnp.where(causal, z, NEG_INF)
                m_old = m_ref[e, :, cols]
                m_new = jnp.maximum(m_old, jnp.max(z, axis=0, keepdims=True))
                p = jnp.exp2(z - m_new).astype(BF16)
                if live < tk:
                    p = jnp.concatenate([p, jnp.zeros((tk - live, LANES), BF16)], axis=0)
                ps.append(p)
                alphas.append(jnp.exp2(m_old - m_new))
                m_ref[e, :, cols] = m_new
            pt = jnp.concatenate(ps, axis=1)
            alpha = jnp.concatenate(alphas, axis=1)
            span = slice(c0 * LANES, c1 * LANES)
            acc_ref[e, :, span] = alpha * acc_ref[e, :, span] + jnp.dot(vta, pt, preferred_element_type=F32)

    half = tk // LANES
    base = ((pl.program_id(0) * pl.num_programs(2) + qi) * 2 * B_HEADS + 2 * pl.program_id(1))
    lead = jnp.minimum(skip_ref[base], skip_ref[base + 1])
    first = jnp.minimum(jnp.minimum(skip_ref[base + B_HEADS], skip_ref[base + B_HEADS + 1]), 2 * qi)
    first = jnp.maximum(first, lead)

    second = half
    short = (qi >= 1) & (lead == 2 * qi - 1) & (first == 2 * qi)

    @pl.when(short)
    def _():
        scores(2 * qi - 1, stb_ref, c1=half)
        scores(2 * qi, sta_ref)
        scores(2 * qi + 1, stb_ref, c0=second)
        softmax_pv(2 * qi - 1, stb_ref, c1=half)
        softmax_pv(2 * qi, sta_ref, first_query=0)
        softmax_pv(2 * qi + 1, stb_ref, first_query=-tk, c0=second)

    @pl.when(jnp.logical_not(short))
    def _():
        def early_half(u, carry):
            scores(u, stb_ref, c1=half)
            softmax_pv(u, stb_ref, c1=half)
            return carry

        lax.fori_loop(lead, first, early_half, 0)
        first_pair = (first + 1) // 2
        odd_start = first % 2 == 1

        @pl.when(odd_start)
        def _():
            scores(first, stb_ref)
            scores(first + 1, sta_ref)
            softmax_pv(first, stb_ref)

        @pl.when(jnp.logical_not(odd_start))
        def _():
            scores(first, sta_ref)

        def pair(t, carry):
            u = 2 * t
            scores(u + 1, stb_ref)
            softmax_pv(u, sta_ref)
            scores(u + 2, sta_ref)
            softmax_pv(u + 1, stb_ref)
            return carry

        lax.fori_loop(first_pair, qi, pair, 0)
        scores(2 * qi + 1, stb_ref, c0=second)
        softmax_pv(2 * qi, sta_ref, first_query=0)
        softmax_pv(2 * qi + 1, stb_ref, first_query=-tk, c0=second)

    out_t = jnp.concatenate(
        [acc_ref[e, :HEAD_DIM, :] * (1.0 / acc_ref[e, HEAD_DIM:HEAD_DIM + 1, :]) for e in range(2)], axis=0)
    o_ref[0] = jnp.transpose(out_t).astype(BF16)


def _fox(skip, qb, kb, vbt, tq):
    bsz, s, _ = qb.shape
    _, nk, _, tk = vbt.shape
    npair = B_HEADS // 2
    return pl.pallas_call(
        _fox_kernel,
        grid_spec=pltpu.PrefetchScalarGridSpec(
            num_scalar_prefetch=1,
            grid=(bsz, npair, s // tq),
            in_specs=[pl.BlockSpec((1, tq, 2 * LANES), lambda b, hp, i, skip: (b, i, hp)),
                      pl.BlockSpec((1, s, 2 * LANES), lambda b, hp, i, skip: (b, 0, hp)),
                      pl.BlockSpec((1, nk, 2 * HEAD_DIM, tk), lambda b, hp, i, skip: (b, 0, hp, 0))],
            out_specs=pl.BlockSpec((1, tq, 2 * HEAD_DIM), lambda b, hp, i, skip: (b, i, hp)),
            scratch_shapes=[pltpu.VMEM((2, 1, tq), F32),
                            pltpu.VMEM((2, HEAD_DIM + ONES_ROWS, tq), F32),
                            pltpu.VMEM((2, tk, tq), F32),
                            pltpu.VMEM((2, tk, tq), F32)]),
        out_shape=jax.ShapeDtypeStruct((bsz, s, B_W), BF16),
        compiler_params=_params("parallel", "arbitrary", "arbitrary"),
        name="fox",
    )(skip, qb, kb, vbt)


def _resident(shape):
    return pl.BlockSpec(shape, lambda b, i: (0,) * len(shape), pipeline_mode=pl.Buffered(1))


def _post_kernel(x_ref, oa_ref, ob_ref, ada_ref, gpre_ref, gpost_ref, wg_ref, wa_ref, wb_ref,
                 wo_ref, o_ref, *, sub):
    d = x_ref.shape[2]
    nt = (((1,), (1,)), ((), ()))
    merged = []
    for r in range(x_ref.shape[1] // sub):
        rows = slice(r * sub, (r + 1) * sub)
        h = _rms_mod(x_ref[0, rows, :], gpre_ref[...], ada_ref[0, 1:2, :], ada_ref[0, 0:1, :]).astype(BF16)
        a = jnp.dot(oa_ref[0, rows, :], wa_ref[...], preferred_element_type=F32)
        b = jnp.dot(ob_ref[0, rows, :], wb_ref[...], preferred_element_type=F32)
        ga = jax.nn.sigmoid(lax.dot_general(h, wg_ref[:d, :], nt, preferred_element_type=F32))
        gb = jax.nn.sigmoid(lax.dot_general(h, wg_ref[d:, :], nt, preferred_element_type=F32))
        merged.append((ga * a + gb * b).astype(BF16))
    for r in range(x_ref.shape[1] // sub):
        rows = slice(r * sub, (r + 1) * sub)
        y = jnp.dot(merged[r], wo_ref[...], preferred_element_type=F32)
        o_ref[0, rows, :] = x_ref[0, rows, :] + ada_ref[0, 2:3, :] * _rms(y, gpost_ref[...])


def _post(x, oa, ob, ada, g_pre, g_post, wg, wa, wb, wo, tm, sub):
    bsz, s, d = x.shape
    tok = lambda w: pl.BlockSpec((1, tm, w), lambda b, i: (b, i, 0))
    const = lambda shape: pl.BlockSpec(shape, lambda b, i: (0,) * len(shape))
    return pl.pallas_call(
        functools.partial(_post_kernel, sub=sub),
        grid=(bsz, s // tm),
        in_specs=[tok(d), tok(A_Q_W), tok(B_W),
                  pl.BlockSpec((1, N_ADA, d), lambda b, i: (b, 0, 0)),
                  const((1, d)), const((1, d)),
                  _resident(wg.shape), _resident(wa.shape), _resident(wb.shape), _resident(wo.shape)],
        out_specs=tok(d),
        out_shape=jax.ShapeDtypeStruct((bsz, s, d), F32),
        compiler_params=_params("parallel", "parallel"),
        name="post",
    )(x, oa, ob, ada, g_pre, g_post, wg, wa, wb, wo)


def _ffn_kernel(x_ref, ada_ref, gpre_ref, gpost_ref, wi_ref, wo_ref, o_ref, *, sub, chunks):
    d_ff = wo_ref.shape[0]
    for r in range(x_ref.shape[1] // sub):
        rows = slice(r * sub, (r + 1) * sub)
        x = x_ref[0, rows, :]
        h = _rms_mod(x, gpre_ref[...], ada_ref[0, 4:5, :], ada_ref[0, 3:4, :]).astype(BF16)
        acts = []
        lo = 0
        for width in chunks:
            g = jnp.dot(h, wi_ref[:, lo:lo + width], preferred_element_type=F32)
            u = jnp.dot(h, wi_ref[:, d_ff + lo:d_ff + lo + width], preferred_element_type=F32)
            acts.append((g * jax.nn.sigmoid(g) * u).astype(BF16))
            lo += width
        y = jnp.zeros(x.shape, F32)
        lo = 0
        for width, act in zip(chunks, acts):
            y = y + jnp.dot(act, wo_ref[lo:lo + width, :], preferred_element_type=F32)
            lo += width
        o_ref[0, rows, :] = x + ada_ref[0, 5:6, :] * _rms(y, gpost_ref[...])


def _ffn(x, ada, g_pre, g_post, wi, wo, tm, sub, chunks):
    bsz, s, d = x.shape
    tok = pl.BlockSpec((1, tm, d), lambda b, i: (b, i, 0))
    const = lambda shape: pl.BlockSpec(shape, lambda b, i: (0,) * len(shape))
    return pl.pallas_call(
        functools.partial(_ffn_kernel, sub=sub, chunks=chunks),
        grid=(bsz, s // tm),
        in_specs=[tok, pl.BlockSpec((1, N_ADA, d), lambda b, i: (b, 0, 0)),
                  const((1, d)), const((1, d)), _resident(wi.shape), _resident(wo.shape)],
        out_specs=tok,
        out_shape=jax.ShapeDtypeStruct((bsz, s, d), F32),
        compiler_params=_params("parallel", "parallel"),
        name="ffn",
    )(x, ada, g_pre, g_post, wi, wo)


def _pair_heads(w, axis):
    shape = w.shape
    pre, post = shape[:axis], shape[axis + 1:]
    w = w.reshape(pre + (A_KV_HEADS, A_GROUP, HEAD_DIM) + post)
    w = jnp.swapaxes(w, axis, axis + 1)
    return w.reshape(shape)


def kernel(x, c, positions, w_ada, b_ada, g_pre_mix, g_post_mix, w_in, b_f, sinks, w_branch_a,
           w_branch_b, w_out, g_pre_ffn, g_post_ffn, w_ffn_in, w_ffn_out):
    bsz, s, d = x.shape
    depth = w_ada.shape[0]
    tm = min(1024, s)
    sub = min(512, s)
    rep = LANES // ROPE_FREQS
    assert (s // rep) % sub == 0
    half = jnp.arange(0, HEAD_DIM, 2, dtype=F32) / HEAD_DIM
    inv_freq = jnp.tile(1.0 / (ROPE_THETA ** half), rep).reshape(1, LANES)
    pos_c = jnp.swapaxes(positions.astype(F32).reshape(bsz, rep, s // rep), 1, 2)
    cos, sin = _rope_tables(jnp.repeat(pos_c, ROPE_FREQS, axis=2), inv_freq)
    for l in range(depth):
        w_t = jnp.swapaxes(w_in[l], 0, 1)
        o_va, o_qb, o_vb = A_Q_W + A_KV_W, A_Q_W + 2 * A_KV_W, A_Q_W + 2 * A_KV_W + 2 * B_W
        w_f = jnp.pad(w_t[QKV_W:QKV_W + B_HEADS], ((0, LANES - B_HEADS), (0, 0)))
        w_qk = jnp.concatenate([_pair_heads(w_t[:A_Q_W], 0), w_t[o_qb:o_vb], w_t[A_Q_W:o_va], w_f],
                               axis=0).astype(BF16)
        w_vt = jnp.concatenate([w_t[o_va:o_qb], w_t[o_vb:QKV_W]], axis=0).astype(BF16)
        w_g = w_t[QKV_W + B_HEADS:].astype(BF16)
        bf_row = jnp.pad(b_f[l], (0, LANES - B_HEADS)).reshape(1, LANES)
        sink_row = jnp.repeat(sinks[l], WINDOW).reshape(1, A_Q_HEADS * WINDOW)
        w_a = _pair_heads(w_branch_a[l], 0).astype(BF16)
        w_b = w_branch_b[l].astype(BF16)
        w_o = w_out[l].astype(BF16)
        w_fi = w_ffn_in[l].astype(BF16)
        w_fo = w_ffn_out[l].astype(BF16)
        row = lambda g: g[l].reshape(1, d)

        ada = _ada(c, w_ada[l], b_ada[l]).reshape(bsz, N_ADA, d)
        qa, ka, vat, qb, kb, vbt, stats = _in_proj(x, cos, sin, ada, row(g_pre_mix), w_qk, w_vt, bf_row,
                                                   tm, sub)
        o_a = _swa(qa, ka, vat, sink_row, tm)
        skip = _fox_plan(stats, 1)[:, :, :B_HEADS].reshape(-1)
        o_b = _fox(skip, qb, kb, vbt, 2 * sub)
        x = _post(x, o_a, o_b, ada, row(g_pre_mix), row(g_post_mix), w_g, w_a, w_b, w_o, tm, sub)
        n_tiles = w_fo.shape[0] // MXU_DEPTH
        chunks = ((n_tiles + 1) // 2 * MXU_DEPTH, n_tiles // 2 * MXU_DEPTH)
        x = _ffn(x, ada, row(g_pre_ffn), row(g_post_ffn), w_fi, w_fo, tm, sub, chunks)
    return x
```

```python
import functools
import math

import jax
import jax.numpy as jnp
import numpy as np
from jax import lax
from jax.experimental import pallas as pl
from jax.experimental.pallas import tpu as pltpu

HEAD_DIM = 64
WINDOW = 128
A_Q_HEADS = 8
A_KV_HEADS = 2
A_GROUP = A_Q_HEADS // A_KV_HEADS
B_HEADS = 8
N_ADA = 6
ROPE_THETA = 10000.0
RMS_EPS = 1e-6
LANES = 128
MXU_DEPTH = 256
ROPE_FREQS = HEAD_DIM // 2
BF16_ROWS = 16
VMEM_LIMIT = 56 * 1024 * 1024

A_Q_W = A_Q_HEADS * HEAD_DIM
A_KV_W = A_KV_HEADS * HEAD_DIM
B_W = B_HEADS * HEAD_DIM
QKV_W = A_Q_W + 2 * A_KV_W + 3 * B_W
BF16 = jnp.bfloat16
F32 = jnp.float32
NEG_INF = float("-inf")
LOG2E = math.log2(math.e)
ONES_ROWS = 16
SKIP_LOG2 = 48.0
NORM_MARGIN = 1.02
NORM_ROWS = 256


def _params(*sem, flags=None):
    return pltpu.CompilerParams(dimension_semantics=sem, vmem_limit_bytes=VMEM_LIMIT, flags=flags)


def _rms_mod(x, g, scale, shift):
    ms = jnp.mean(x * x, axis=-1, keepdims=True)
    return (x * lax.rsqrt(ms + RMS_EPS)) * (g * (1.0 + scale)) + shift


def _rms(y, g):
    ms = jnp.mean(y * y, axis=-1, keepdims=True)
    return y * lax.rsqrt(ms + RMS_EPS) * g


def _ada_kernel(c_ref, w_ref, b_ref, o_ref):
    o_ref[...] = jnp.dot(c_ref[...], w_ref[...], preferred_element_type=F32) + b_ref[...]


def _ada(c, w, b):
    bsz, d = c.shape
    n = w.shape[1]
    tn = n // 4
    return pl.pallas_call(
        _ada_kernel,
        grid=(n // tn,),
        in_specs=[pl.BlockSpec((bsz, d), lambda j: (0, 0)),
                  pl.BlockSpec((d, tn), lambda j: (0, j)),
                  pl.BlockSpec((1, tn), lambda j: (0, j))],
        out_specs=pl.BlockSpec((bsz, tn), lambda j: (0, j)),
        out_shape=jax.ShapeDtypeStruct((bsz, n), F32),
        compiler_params=_params("arbitrary"),
        name="ada",
    )(c, w, b.reshape(1, n))


def _rope_table_kernel(pos_ref, invf_ref, cos_ref, sin_ref):
    ang = pos_ref[0] * invf_ref[...]
    cos_ref[0] = jnp.cos(ang)
    sin_ref[0] = jnp.sin(ang)


def _rope_tables(pos_c, inv_freq):
    bsz, rows, _ = pos_c.shape
    blk = pl.BlockSpec((1, rows, LANES), lambda b: (b, 0, 0))
    shape = jax.ShapeDtypeStruct((bsz, rows, LANES), F32)
    return pl.pallas_call(
        _rope_table_kernel,
        grid=(bsz,),
        in_specs=[blk, pl.BlockSpec((1, LANES), lambda b: (0, 0))],
        out_specs=[blk, blk],
        out_shape=[shape, shape],
        compiler_params=_params("parallel"),
        name="rope_tables",
    )(pos_c, inv_freq)


AUG_PARTS = 3


def _aug_base(head):
    return head * LANES + (HEAD_DIM if head % 2 == 0 else 0)


def _aug_placement():
    p = np.zeros((LANES, B_HEADS * LANES), np.float32)
    for head in range(B_HEADS):
        for part in range(AUG_PARTS):
            p[part * B_HEADS + head, _aug_base(head) + part] = 1.0
            p[part * B_HEADS + head, _aug_base(head) + AUG_PARTS + part] = -1.0
    return jnp.asarray(p, BF16)


def _inproj_kernel(x_ref, cos_ref, sin_ref, ada_ref, g_ref, w_ref, wvt_ref, bf_ref, place_ref, hsum_ref,
                   qa_ref, ka_ref, vat_ref, qb_ref, kb_ref, vbt_ref, stat_ref, carry_ref):
    i = pl.program_id(1)
    sub = vbt_ref.shape[3]
    lane = lax.broadcasted_iota(jnp.int32, (sub, LANES), 1)
    row = lax.broadcasted_iota(jnp.int32, (sub, LANES), 0)
    first_half = (lane % HEAD_DIM) < (HEAD_DIM // 2)
    low = lane < HEAD_DIM
    q_scale = LOG2E / math.sqrt(HEAD_DIM)

    @pl.when(i == 0)
    def _():
        carry_ref[...] = jnp.zeros(carry_ref.shape, F32)

    carry = carry_ref[0:1, :]
    for r in range(x_ref.shape[1] // sub):
        rows = slice(r * sub, (r + 1) * sub)
        h = _rms_mod(x_ref[0, rows, :], g_ref[...], ada_ref[0, 1:2, :], ada_ref[0, 0:1, :]).astype(BF16)

        def proj(lo, width):
            return lax.dot_general(h, w_ref[lo:lo + width, :], (((1,), (1,)), ((), ())),
                                   preferred_element_type=F32)

        def proj_t(lo, width):
            return lax.dot_general(wvt_ref[lo:lo + width, :], h, (((1,), (1,)), ((), ())),
                                   preferred_element_type=F32)

        t0 = i * x_ref.shape[1] + r * sub
        quarter = t0 // cos_ref.shape[1]
        t_rows = pl.ds(pl.multiple_of(t0 - quarter * cos_ref.shape[1], sub), sub)
        shift = (LANES - quarter * ROPE_FREQS) % LANES

        def spread(tab):
            y = jnp.where(lane < ROPE_FREQS, pltpu.roll(tab, shift, axis=1), 0.0)
            y = y + pltpu.roll(y, ROPE_FREQS, axis=1)
            return y + pltpu.roll(y, 2 * ROPE_FREQS, axis=1)

        cos = spread(cos_ref[0, t_rows, :])
        sin = spread(sin_ref[0, t_rows, :])
        sin_signed = jnp.where(first_half, -sin, sin)

        def rope(t):
            partner = jnp.where(first_half,
                                pltpu.roll(t, LANES - HEAD_DIM // 2, axis=1),
                                pltpu.roll(t, HEAD_DIM // 2, axis=1))
            return t * cos + partner * sin_signed

        kf = proj(A_Q_W + 2 * B_W, A_KV_W + LANES)
        ka_ref[0, rows, :] = rope(kf[:, :A_KV_W]).astype(BF16)
        fl = kf[:, A_KV_W:] + bf_ref[...]
        c = jnp.minimum(fl, 0.0) - jnp.log1p(jnp.exp(-jnp.abs(fl)))
        d = 1
        while d < sub:
            c = c + jnp.where(row >= d, pltpu.roll(c, d, axis=0), 0.0)
            d *= 2
        c = c + carry
        carry = c[sub - 1:sub, :]
        c = c * LOG2E
        hi = c.astype(BF16).astype(F32)
        r1 = c - hi
        mid = r1.astype(BF16).astype(F32)
        lo = r1 - mid
        packed = jnp.where(lane < B_HEADS, hi,
                           jnp.where(lane < 2 * B_HEADS, pltpu.roll(mid, B_HEADS, axis=1),
                                     jnp.where(lane < 3 * B_HEADS, pltpu.roll(lo, 2 * B_HEADS, axis=1), 0.0)))

        qa = proj(0, A_Q_W)
        for j in range(A_Q_W // LANES):
            qa_ref[0, rows, j * LANES:(j + 1) * LANES] = (
                rope(qa[:, j * LANES:(j + 1) * LANES]) * q_scale).astype(BF16)
        vat_ref[0, :, rows] = proj_t(0, A_KV_W).astype(BF16)
        vbt_ref[0, r] = proj_t(A_KV_W, B_W).astype(BF16)
        qb = proj(A_Q_W, B_W) * q_scale
        kb = proj(A_Q_W + B_W, B_W)
        aug = jnp.dot(packed.astype(BF16), place_ref[...], preferred_element_type=F32)
        def norm_bound(t):
            sq = t * t
            fold = sq[:NORM_ROWS]
            for g in range(1, sub // NORM_ROWS):
                fold = jnp.maximum(fold, sq[g * NORM_ROWS:(g + 1) * NORM_ROWS])
            per_head = jnp.dot(fold.astype(BF16), hsum_ref[...], preferred_element_type=F32)
            return jnp.max(per_head, axis=0, keepdims=True) * NORM_MARGIN

        stats = [norm_bound(qb), norm_bound(kb)]
        stats += [c[0:1, :], c[sub - 1:sub, :], jnp.zeros((4, LANES), F32)]
        stat_ref[0, r] = jnp.concatenate(stats, axis=0)
        for head in range(B_HEADS):
            base = _aug_base(head) % LANES
            own = low if head % 2 == 0 else jnp.logical_not(low)
            plus = (lane >= base) & (lane < base + AUG_PARTS)
            minus = (lane >= base + AUG_PARTS) & (lane < base + 2 * AUG_PARTS)
            a = aug[:, head * LANES:(head + 1) * LANES]
            src = slice(head // 2 * LANES, (head // 2 + 1) * LANES)
            group = slice(head * LANES, (head + 1) * LANES)
            qb_ref[0, rows, group] = jnp.where(own, qb[:, src],
                                               jnp.where(plus, a, jnp.where(minus, 1.0, 0.0))).astype(BF16)
            kb_ref[0, rows, group] = jnp.where(own, kb[:, src],
                                               jnp.where(minus, a, jnp.where(plus, 1.0, 0.0))).astype(BF16)
    carry_ref[0:1, :] = carry


def _in_proj(x, cos, sin, ada, g_pre, w_qk, w_vt, b_f, tm, sub):
    bsz, s, d = x.shape
    grid = (bsz, s // tm)
    tok = lambda w: pl.BlockSpec((1, tm, w), lambda b, i: (b, i, 0))
    const = lambda shape: pl.BlockSpec(shape, lambda b, i: (0,) * len(shape))
    act = lambda w: jax.ShapeDtypeStruct((bsz, s, w), BF16)
    place = _aug_placement()
    head_sum = jnp.asarray(np.kron(np.eye(B_HEADS, LANES), np.ones((HEAD_DIM, 1))), BF16)
    return pl.pallas_call(
        _inproj_kernel,
        grid=grid,
        in_specs=[tok(d), pl.BlockSpec((1,) + cos.shape[1:], lambda b, i: (b, 0, 0)),
                  pl.BlockSpec((1,) + sin.shape[1:], lambda b, i: (b, 0, 0)),
                  pl.BlockSpec((1, N_ADA, d), lambda b, i: (b, 0, 0)),
                  const((1, d)), _resident(w_qk.shape), _resident(w_vt.shape),
                  const((1, LANES)), const(place.shape), const(head_sum.shape)],
        out_specs=[tok(A_Q_W), tok(A_KV_W),
                   pl.BlockSpec((1, A_KV_W, tm), lambda b, i: (b, 0, i)),
                   tok(B_HEADS * LANES), tok(B_HEADS * LANES),
                   pl.BlockSpec((1, tm // sub, B_W, sub), lambda b, i: (b, i, 0, 0)),
                   pl.BlockSpec((1, tm // sub, 8, LANES), lambda b, i: (b, i, 0, 0))],
        out_shape=[act(A_Q_W), act(A_KV_W),
                   jax.ShapeDtypeStruct((bsz, A_KV_W, s), BF16),
                   act(B_HEADS * LANES), act(B_HEADS * LANES),
                   jax.ShapeDtypeStruct((bsz, s // sub, B_W, sub), BF16),
                   jax.ShapeDtypeStruct((bsz, s // sub, 8, LANES), F32)],
        scratch_shapes=[pltpu.VMEM((8, LANES), F32)],
        compiler_params=_params("parallel", "arbitrary"),
        name="in_proj",
    )(x, cos, sin, ada, g_pre, w_qk, w_vt, b_f, place, head_sum)


def _plan_kernel(st_ref, o_ref, *, ratio):
    nt = st_ref.shape[1]
    q_scale = LOG2E / math.sqrt(HEAD_DIM)
    qn = st_ref[0, 0, 0:1, :]
    kn = st_ref[0, 0, 1:2, :]
    for j in range(1, nt):
        qn = jnp.maximum(qn, st_ref[0, j, 0:1, :])
        kn = jnp.maximum(kn, st_ref[0, j, 1:2, :])
    spread = qn * (1.0 / q_scale) + kn * q_scale
    rows = []
    for blk in range(nt // ratio):
        limit = st_ref[0, blk * ratio, 2:3, :] + spread + SKIP_LOG2
        count = jnp.zeros((1, LANES), jnp.int32)
        prefix = jnp.ones((1, LANES), jnp.int32)
        for j in range(blk * ratio):
            prefix = prefix * (st_ref[0, j, 3:4, :] > limit).astype(jnp.int32)
            count = count + prefix
        rows.append(count)
    o_ref[0] = jnp.concatenate(rows, axis=0)


def _fox_plan(stats, ratio):
    bsz, nt, _, _ = stats.shape
    return pl.pallas_call(
        functools.partial(_plan_kernel, ratio=ratio),
        grid=(bsz,),
        in_specs=[pl.BlockSpec((1, nt, 8, LANES), lambda b: (b, 0, 0, 0))],
        out_specs=pl.BlockSpec((1, nt // ratio, LANES), lambda b: (b, 0, 0)),
        out_shape=jax.ShapeDtypeStruct((bsz, nt // ratio, LANES), jnp.int32),
        compiler_params=_params("parallel"),
        name="fox_plan",
    )(stats)


def _swa_kernel(q_ref, kp_ref, kc_ref, vtp_ref, vtc_ref, sink_ref, o_ref):
    i = pl.program_id(1)
    tq = q_ref.shape[1]
    nsub = tq // WINDOW
    ncol = A_Q_W // LANES
    lane_q = lax.broadcasted_iota(jnp.int32, (WINDOW, LANES), 1)
    low = lane_q < HEAD_DIM
    low_dim = lax.broadcasted_iota(jnp.int32, (LANES, WINDOW), 0) < HEAD_DIM
    k_loc = lax.broadcasted_iota(jnp.int32, (2 * WINDOW, WINDOW), 0)
    q_loc = lax.broadcasted_iota(jnp.int32, (2 * WINDOW, WINDOW), 1) + WINDOW
    rel = q_loc - k_loc
    band = (rel >= 0) & (rel < WINDOW)
    sink = sink_ref[...] * LOG2E
    ones = jnp.ones((ONES_ROWS, 2 * WINDOW), BF16)
    zero = jnp.zeros((WINDOW, LANES), BF16)

    def keys(j):
        if j == 0:
            return jnp.concatenate([kp_ref[0], kc_ref[0, :WINDOW, :]], axis=0)
        return kc_ref[0, (j - 1) * WINDOW:(j + 1) * WINDOW, :]

    def values_t(j):
        if j == 0:
            return jnp.concatenate([vtp_ref[0], vtc_ref[0, :, :WINDOW]], axis=1)
        return vtc_ref[0, :, (j - 1) * WINDOW:(j + 1) * WINDOW]

    def scores(j):
        q = q_ref[0, j * WINDOW:(j + 1) * WINDOW, :]
        parts = [jnp.where(low, q[:, c * LANES:(c + 1) * LANES], zero) for c in range(ncol)]
        parts += [jnp.where(low, zero, q[:, c * LANES:(c + 1) * LANES]) for c in range(ncol)]
        qs = jnp.concatenate(parts, axis=0)
        return lax.dot_general(keys(j), qs, (((1,), (1,)), ((), ())), preferred_element_type=F32)

    def softmax_pv(j, st):
        valid = band & ((k_loc >= WINDOW) | (i > 0)) if j == 0 else band
        ps, sink_terms = [], []
        for hh in range(A_Q_HEADS):
            cols = slice(hh * WINDOW, (hh + 1) * WINDOW)
            w = jnp.where(valid, st[:, cols], NEG_INF)
            m = jnp.maximum(jnp.max(w, axis=0, keepdims=True), sink[:, cols])
            ps.append(jnp.exp2(w - m).astype(BF16))
            sink_terms.append(jnp.exp2(sink[:, cols] - m))
        pt = jnp.concatenate(ps, axis=1)
        ot = jnp.dot(jnp.concatenate([values_t(j), ones], axis=0), pt, preferred_element_type=F32)
        for c in range(ncol):
            halves = []
            for hh in (c, ncol + c):
                cols = slice(hh * WINDOW, (hh + 1) * WINDOW)
                denom = ot[LANES:LANES + 1, cols] + sink_terms[hh]
                halves.append(ot[:LANES, cols] * (1.0 / denom))
            o_ref[0, j * WINDOW:(j + 1) * WINDOW, c * LANES:(c + 1) * LANES] = (
                jnp.transpose(jnp.where(low_dim, halves[0], halves[1])).astype(BF16))

    st = scores(0)
    for j in range(nsub):
        st_next = scores(j + 1) if j + 1 < nsub else None
        softmax_pv(j, st)
        st = st_next


def _swa(qa, ka, vat, sinks, tq):
    bsz, s, _ = qa.shape
    r = tq // WINDOW
    cur = lambda b, i: (b, i, 0)
    return pl.pallas_call(
        _swa_kernel,
        grid=(bsz, s // tq),
        in_specs=[pl.BlockSpec((1, tq, A_Q_W), cur),
                  pl.BlockSpec((1, WINDOW, A_KV_W), lambda b, i: (b, jnp.maximum(i * r - 1, 0), 0)),
                  pl.BlockSpec((1, tq, A_KV_W), cur),
                  pl.BlockSpec((1, A_KV_W, WINDOW), lambda b, i: (b, 0, jnp.maximum(i * r - 1, 0))),
                  pl.BlockSpec((1, A_KV_W, tq), lambda b, i: (b, 0, i)),
                  pl.BlockSpec((1, A_Q_HEADS * WINDOW), lambda b, i: (0, 0))],
        out_specs=pl.BlockSpec((1, tq, A_Q_W), cur),
        out_shape=jax.ShapeDtypeStruct((bsz, s, A_Q_W), BF16),
        compiler_params=_params("parallel", "parallel"),
        name="swa",
    )(qa, ka, ka, vat, vat, sinks)


def _fox_kernel(skip_ref, q_ref, k_ref, vt_ref, o_ref, m_ref, acc_ref, sta_ref, stb_ref):
    qi = pl.program_id(2)
    tq = q_ref.shape[1]
    tk = vt_ref.shape[3]
    nchunk = tq // LANES
    assert tq == 2 * tk
    m_ref[...] = jnp.full(m_ref.shape, NEG_INF, F32)
    acc_ref[...] = jnp.zeros(acc_ref.shape, F32)
    ones = jnp.ones((ONES_ROWS, tk), BF16)

    def scores(kj, st_ref, c0=0, c1=nchunk):
        start = pl.multiple_of(kj * tk, tk)
        for e in range(2):
            group = slice(e * LANES, (e + 1) * LANES)
            st_ref[e, :, c0 * LANES:c1 * LANES] = lax.dot_general(
                k_ref[0, pl.ds(start, tk), group], q_ref[0, c0 * LANES:c1 * LANES, group],
                (((1,), (1,)), ((), ())), preferred_element_type=F32)

    def softmax_pv(kj, st_ref, first_query=None, c0=0, c1=nchunk):
        for e in range(2):
            vta = jnp.concatenate([vt_ref[0, kj, e * HEAD_DIM:(e + 1) * HEAD_DIM, :], ones], axis=0)
            ps, alphas = [], []
            for c in range(c0, c1):
                cols = slice(c * LANES, (c + 1) * LANES)
                off = None if first_query is None else first_query + c * LANES
                masked = off is not None and off < tk
                live = min(off + LANES, tk) if masked else tk
                z = st_ref[e, :live, cols]
                if masked:
                    causal = (lax.broadcasted_iota(jnp.int32, (live, LANES), 0)
                              <= lax.broadcasted_iota(jnp.int32, (live, LANES), 1) + off)
                    z = jnp.where(causal, z, NEG_INF)
                m_old = m_ref[e, :, cols]
                m_new = jnp.maximum(m_old, jnp.max(z, axis=0, keepdims=True))
                p = jnp.exp2(z - m_new).astype(BF16)
                if live < tk:
                    p = jnp.concatenate([p, jnp.zeros((tk - live, LANES), BF16)], axis=0)
                ps.append(p)
                alphas.append(jnp.exp2(m_old - m_new))
                m_ref[e, :, cols] = m_new
            pt = jnp.concatenate(ps, axis=1)
            alpha = jnp.concatenate(alphas, axis=1)
            span = slice(c0 * LANES, c1 * LANES)
            acc_ref[e, :, span] = alpha * acc_ref[e, :, span] + jnp.dot(vta, pt, preferred_element_type=F32)

    half = tk // LANES
    base = ((pl.program_id(0) * pl.num_programs(2) + qi) * 2 * B_HEADS + 2 * pl.program_id(1))
    lead = jnp.minimum(skip_ref[base], skip_ref[base + 1])
    first = jnp.minimum(jnp.minimum(skip_ref[base + B_HEADS], skip_ref[base + B_HEADS + 1]), 2 * qi)
    first = jnp.maximum(first, lead)

    second = half
    short = (qi >= 1) & (lead == 2 * qi - 1) & (first == 2 * qi)

    @pl.when(short)
    def _():
        scores(2 * qi - 1, stb_ref, c1=half)
        scores(2 * qi, sta_ref)
        scores(2 * qi + 1, stb_ref, c0=second)
        softmax_pv(2 * qi - 1, stb_ref, c1=half)
        softmax_pv(2 * qi, sta_ref, first_query=0)
        softmax_pv(2 * qi + 1, stb_ref, first_query=-tk, c0=second)

    @pl.when(jnp.logical_not(short))
    def _():
        def early_half(u, carry):
            scores(u, stb_ref, c1=half)
            softmax_pv(u, stb_ref, c1=half)
            return carry

        lax.fori_loop(lead, first, early_half, 0)
        first_pair = (first + 1) // 2
        odd_start = first % 2 == 1

        @pl.when(odd_start)
        def _():
            scores(first, stb_ref)
            scores(first + 1, sta_ref)
            softmax_pv(first, stb_ref)

        @pl.when(jnp.logical_not(odd_start))
        def _():
            scores(first, sta_ref)

        def pair(t, carry):
            u = 2 * t
            scores(u + 1, stb_ref)
            softmax_pv(u, sta_ref)
            scores(u + 2, sta_ref)
            softmax_pv(u + 1, stb_ref)
            return carry

        lax.fori_loop(first_pair, qi, pair, 0)
        scores(2 * qi + 1, stb_ref, c0=second)
        softmax_pv(2 * qi, sta_ref, first_query=0)
        softmax_pv(2 * qi + 1, stb_ref, first_query=-tk, c0=second)

    out_t = jnp.concatenate(
        [acc_ref[e, :HEAD_DIM, :] * (1.0 / acc_ref[e, HEAD_DIM:HEAD_DIM + 1, :]) for e in range(2)], axis=0)
    o_ref[0] = jnp.transpose(out_t).astype(BF16)


def _fox(skip, qb, kb, vbt, tq):
    bsz, s, _ = qb.shape
    _, nk, _, tk = vbt.shape
    npair = B_HEADS // 2
    return pl.pallas_call(
        _fox_kernel,
        grid_spec=pltpu.PrefetchScalarGridSpec(
            num_scalar_prefetch=1,
            grid=(bsz, npair, s // tq),
            in_specs=[pl.BlockSpec((1, tq, 2 * LANES), lambda b, hp, i, skip: (b, i, hp)),
                      pl.BlockSpec((1, s, 2 * LANES), lambda b, hp, i, skip: (b, 0, hp)),
                      pl.BlockSpec((1, nk, 2 * HEAD_DIM, tk), lambda b, hp, i, skip: (b, 0, hp, 0))],
            out_specs=pl.BlockSpec((1, tq, 2 * HEAD_DIM), lambda b, hp, i, skip: (b, i, hp)),
            scratch_shapes=[pltpu.VMEM((2, 1, tq), F32),
                            pltpu.VMEM((2, HEAD_DIM + ONES_ROWS, tq), F32),
                            pltpu.VMEM((2, tk, tq), F32),
                            pltpu.VMEM((2, tk, tq), F32)]),
        out_shape=jax.ShapeDtypeStruct((bsz, s, B_W), BF16),
        compiler_params=_params("parallel", "arbitrary", "arbitrary"),
        name="fox",
    )(skip, qb, kb, vbt)


def _resident(shape):
    return pl.BlockSpec(shape, lambda b, i: (0,) * len(shape), pipeline_mode=pl.Buffered(1))


def _post_kernel(x_ref, oa_ref, ob_ref, ada_ref, gpre_ref, gpost_ref, wg_ref, wa_ref, wb_ref,
                 wo_ref, wfi_ref, wfo_ref, o_ref, wfi_out_ref, wfo_out_ref, *, sub):
    d = x_ref.shape[2]
    wfi_out_ref[...] = wfi_ref[...].astype(BF16)
    wfo_out_ref[...] = wfo_ref[...].astype(BF16)
    nt = (((1,), (1,)), ((), ()))
    merged = []
    for r in range(x_ref.shape[1] // sub):
        rows = slice(r * sub, (r + 1) * sub)
        h = _rms_mod(x_ref[0, rows, :], gpre_ref[...], ada_ref[0, 1:2, :], ada_ref[0, 0:1, :]).astype(BF16)
        a = jnp.dot(oa_ref[0, rows, :], wa_ref[...], preferred_element_type=F32)
        b = jnp.dot(ob_ref[0, rows, :], wb_ref[...], preferred_element_type=F32)
        ga = jax.nn.sigmoid(lax.dot_general(h, wg_ref[:d, :], nt, preferred_element_type=F32))
        gb = jax.nn.sigmoid(lax.dot_general(h, wg_ref[d:, :], nt, preferred_element_type=F32))
        merged.append((ga * a + gb * b).astype(BF16))
    for r in range(x_ref.shape[1] // sub):
        rows = slice(r * sub, (r + 1) * sub)
        y = jnp.dot(merged[r], wo_ref[...], preferred_element_type=F32)
        o_ref[0, rows, :] = x_ref[0, rows, :] + ada_ref[0, 2:3, :] * _rms(y, gpost_ref[...])


def _post(x, oa, ob, ada, g_pre, g_post, wg, wa, wb, wo, w_ffn_in, w_ffn_out, tm, sub):
    bsz, s, d = x.shape
    nt = s // tm
    steps = bsz * nt
    n_in = steps
    n_out = math.gcd(steps, w_ffn_out.shape[0] // BF16_ROWS)
    tok = lambda w: pl.BlockSpec((1, tm, w), lambda b, i: (b, i, 0))
    const = lambda shape: pl.BlockSpec(shape, lambda b, i: (0,) * len(shape))
    slab_in = pl.BlockSpec((d // n_in, w_ffn_in.shape[1]), lambda b, i: (b * nt + i, 0))
    slab_out = pl.BlockSpec((w_ffn_out.shape[0] // n_out, d), lambda b, i: ((b * nt + i) * n_out // steps, 0))
    return pl.pallas_call(
        functools.partial(_post_kernel, sub=sub),
        grid=(bsz, nt),
        in_specs=[tok(d), tok(A_Q_W), tok(B_W),
                  pl.BlockSpec((1, N_ADA, d), lambda b, i: (b, 0, 0)),
                  const((1, d)), const((1, d)),
                  _resident(wg.shape), _resident(wa.shape), _resident(wb.shape), _resident(wo.shape),
                  slab_in, slab_out],
        out_specs=[tok(d), slab_in, slab_out],
        out_shape=[jax.ShapeDtypeStruct((bsz, s, d), F32),
                   jax.ShapeDtypeStruct(w_ffn_in.shape, BF16),
                   jax.ShapeDtypeStruct(w_ffn_out.shape, BF16)],
        compiler_params=_params("arbitrary", "arbitrary"),
        name="post",
    )(x, oa, ob, ada, g_pre, g_post, wg, wa, wb, wo, w_ffn_in, w_ffn_out)


def _ffn_kernel(x_ref, ada_ref, gpre_ref, gpost_ref, wi_ref, wo_ref, o_ref, *, sub, chunks):
    d_ff = wo_ref.shape[0]
    for r in range(x_ref.shape[1] // sub):
        rows = slice(r * sub, (r + 1) * sub)
        x = x_ref[0, rows, :]
        h = _rms_mod(x, gpre_ref[...], ada_ref[0, 4:5, :], ada_ref[0, 3:4, :]).astype(BF16)
        acts = []
        lo = 0
        for width in chunks:
            g = jnp.dot(h, wi_ref[:, lo:lo + width], preferred_element_type=F32)
            u = jnp.dot(h, wi_ref[:, d_ff + lo:d_ff + lo + width], preferred_element_type=F32)
            acts.append((g * jax.nn.sigmoid(g) * u).astype(BF16))
            lo += width
        y = jnp.zeros(x.shape, F32)
        lo = 0
        for width, act in zip(chunks, acts):
            y = y + jnp.dot(act, wo_ref[lo:lo + width, :], preferred_element_type=F32)
            lo += width
        o_ref[0, rows, :] = x + ada_ref[0, 5:6, :] * _rms(y, gpost_ref[...])


def _ffn(x, ada, g_pre, g_post, wi, wo, tm, sub, chunks):
    bsz, s, d = x.shape
    tok = pl.BlockSpec((1, tm, d), lambda b, i: (b, i, 0))
    const = lambda shape: pl.BlockSpec(shape, lambda b, i: (0,) * len(shape))
    return pl.pallas_call(
        functools.partial(_ffn_kernel, sub=sub, chunks=chunks),
        grid=(bsz, s // tm),
        in_specs=[tok, pl.BlockSpec((1, N_ADA, d), lambda b, i: (b, 0, 0)),
                  const((1, d)), const((1, d)), _resident(wi.shape), _resident(wo.shape)],
        out_specs=tok,
        out_shape=jax.ShapeDtypeStruct((bsz, s, d), F32),
        compiler_params=_params("parallel", "parallel"),
        name="ffn",
    )(x, ada, g_pre, g_post, wi, wo)


def _pair_heads(w, axis):
    shape = w.shape
    pre, post = shape[:axis], shape[axis + 1:]
    w = w.reshape(pre + (A_KV_HEADS, A_GROUP, HEAD_DIM) + post)
    w = jnp.swapaxes(w, axis, axis + 1)
    return w.reshape(shape)


def kernel(x, c, positions, w_ada, b_ada, g_pre_mix, g_post_mix, w_in, b_f, sinks, w_branch_a,
           w_branch_b, w_out, g_pre_ffn, g_post_ffn, w_ffn_in, w_ffn_out):
    bsz, s, d = x.shape
    depth = w_ada.shape[0]
    tm = min(1024, s)
    sub = min(512, s)
    rep = LANES // ROPE_FREQS
    assert (s // rep) % sub == 0
    half = jnp.arange(0, HEAD_DIM, 2, dtype=F32) / HEAD_DIM
    inv_freq = jnp.tile(1.0 / (ROPE_THETA ** half), rep).reshape(1, LANES)
    pos_c = jnp.swapaxes(positions.astype(F32).reshape(bsz, rep, s // rep), 1, 2)
    cos, sin = _rope_tables(jnp.repeat(pos_c, ROPE_FREQS, axis=2), inv_freq)
    for l in range(depth):
        w_t = jnp.swapaxes(w_in[l], 0, 1)
        o_va, o_qb, o_vb = A_Q_W + A_KV_W, A_Q_W + 2 * A_KV_W, A_Q_W + 2 * A_KV_W + 2 * B_W
        w_f = jnp.pad(w_t[QKV_W:QKV_W + B_HEADS], ((0, LANES - B_HEADS), (0, 0)))
        w_qk = jnp.concatenate([_pair_heads(w_t[:A_Q_W], 0), w_t[o_qb:o_vb], w_t[A_Q_W:o_va], w_f],
                               axis=0).astype(BF16)
        w_vt = jnp.concatenate([w_t[o_va:o_qb], w_t[o_vb:QKV_W]], axis=0).astype(BF16)
        w_g = w_t[QKV_W + B_HEADS:].astype(BF16)
        bf_row = jnp.pad(b_f[l], (0, LANES - B_HEADS)).reshape(1, LANES)
        sink_row = jnp.repeat(sinks[l], WINDOW).reshape(1, A_Q_HEADS * WINDOW)
        w_a = _pair_heads(w_branch_a[l], 0).astype(BF16)
        w_b = w_branch_b[l].astype(BF16)
        w_o = w_out[l].astype(BF16)
        row = lambda g: g[l].reshape(1, d)

        ada = _ada(c, w_ada[l], b_ada[l]).reshape(bsz, N_ADA, d)
        qa, ka, vat, qb, kb, vbt, stats = _in_proj(x, cos, sin, ada, row(g_pre_mix), w_qk, w_vt, bf_row,
                                                   tm, sub)
        o_a = _swa(qa, ka, vat, sink_row, tm)
        skip = _fox_plan(stats, 1)[:, :, :B_HEADS].reshape(-1)
        o_b = _fox(skip, qb, kb, vbt, 2 * sub)
        x, w_fi, w_fo = _post(x, o_a, o_b, ada, row(g_pre_mix), row(g_post_mix), w_g, w_a, w_b, w_o,
                              w_ffn_in[l], w_ffn_out[l], tm, sub)
        n_tiles = w_fo.shape[0] // MXU_DEPTH
        chunks = ((n_tiles + 1) // 2 * MXU_DEPTH, n_tiles // 2 * MXU_DEPTH)
        x = _ffn(x, ada, row(g_pre_ffn), row(g_post_ffn), w_fi, w_fo, tm, sub, chunks)
    return x
```

```python
import functools
import math

import jax
import jax.numpy as jnp
import numpy as np
from jax import lax
from jax.experimental import pallas as pl
from jax.experimental.pallas import tpu as pltpu

HEAD_DIM = 64
WINDOW = 128
A_Q_HEADS = 8
A_KV_HEADS = 2
A_GROUP = A_Q_HEADS // A_KV_HEADS
B_HEADS = 8
N_ADA = 6
ROPE_THETA = 10000.0
RMS_EPS = 1e-6
LANES = 128
MXU_DEPTH = 256
ROPE_FREQS = HEAD_DIM // 2
BF16_ROWS = 16
VMEM_LIMIT = 56 * 1024 * 1024

A_Q_W = A_Q_HEADS * HEAD_DIM
A_KV_W = A_KV_HEADS * HEAD_DIM
B_W = B_HEADS * HEAD_DIM
QKV_W = A_Q_W + 2 * A_KV_W + 3 * B_W
BF16 = jnp.bfloat16
F32 = jnp.float32
NEG_INF = float("-inf")
LOG2E = math.log2(math.e)
ONES_ROWS = 16
SKIP_LOG2 = 48.0
NORM_MARGIN = 1.02
NORM_ROWS = 256


def _params(*sem, flags=None):
    return pltpu.CompilerParams(dimension_semantics=sem, vmem_limit_bytes=VMEM_LIMIT, flags=flags)


def _rms_mod(x, g, scale, shift):
    ms = jnp.mean(x * x, axis=-1, keepdims=True)
    return (x * lax.rsqrt(ms + RMS_EPS)) * (g * (1.0 + scale)) + shift


def _rms(y, g):
    ms = jnp.mean(y * y, axis=-1, keepdims=True)
    return y * lax.rsqrt(ms + RMS_EPS) * g


def _ada_kernel(c_ref, w_ref, b_ref, o_ref):
    o_ref[...] = jnp.dot(c_ref[...], w_ref[...], preferred_element_type=F32) + b_ref[...]


def _ada(c, w, b):
    bsz, d = c.shape
    n = w.shape[1]
    tn = n // 4
    return pl.pallas_call(
        _ada_kernel,
        grid=(n // tn,),
        in_specs=[pl.BlockSpec((bsz, d), lambda j: (0, 0)),
                  pl.BlockSpec((d, tn), lambda j: (0, j)),
                  pl.BlockSpec((1, tn), lambda j: (0, j))],
        out_specs=pl.BlockSpec((bsz, tn), lambda j: (0, j)),
        out_shape=jax.ShapeDtypeStruct((bsz, n), F32),
        compiler_params=_params("arbitrary"),
        name="ada",
    )(c, w, b.reshape(1, n))


def _ada_rope_kernel(c_ref, w_ref, b_ref, pos_ref, invf_ref, ada_ref, cos_ref, sin_ref):
    ada_ref[...] = jnp.dot(c_ref[...], w_ref[...], preferred_element_type=F32) + b_ref[...]
    ang = pos_ref[0] * invf_ref[...]
    cos_ref[0] = jnp.cos(ang)
    sin_ref[0] = jnp.sin(ang)


def _ada_rope(c, w, b, pos_c, inv_freq):
    bsz, d = c.shape
    n = w.shape[1]
    tn = n // bsz
    assert tn % LANES == 0
    rows = pos_c.shape[1]
    tab = pl.BlockSpec((1, rows, LANES), lambda j: (j, 0, 0))
    tab_shape = jax.ShapeDtypeStruct((bsz, rows, LANES), F32)
    return pl.pallas_call(
        _ada_rope_kernel,
        grid=(bsz,),
        in_specs=[pl.BlockSpec((bsz, d), lambda j: (0, 0)),
                  pl.BlockSpec((d, tn), lambda j: (0, j)),
                  pl.BlockSpec((1, tn), lambda j: (0, j)),
                  tab, pl.BlockSpec((1, LANES), lambda j: (0, 0))],
        out_specs=[pl.BlockSpec((bsz, tn), lambda j: (0, j)), tab, tab],
        out_shape=[jax.ShapeDtypeStruct((bsz, n), F32), tab_shape, tab_shape],
        compiler_params=_params("arbitrary"),
        name="ada_rope",
    )(c, w, b.reshape(1, n), pos_c, inv_freq)


AUG_PARTS = 3


def _aug_base(head):
    return head * LANES + (HEAD_DIM if head % 2 == 0 else 0)


def _aug_placement():
    p = np.zeros((LANES, B_HEADS * LANES), np.float32)
    for head in range(B_HEADS):
        for part in range(AUG_PARTS):
            p[part * B_HEADS + head, _aug_base(head) + part] = 1.0
            p[part * B_HEADS + head, _aug_base(head) + AUG_PARTS + part] = -1.0
    return jnp.asarray(p, BF16)


def _inproj_kernel(x_ref, cos_ref, sin_ref, ada_ref, g_ref, w_ref, wvt_ref, bf_ref, place_ref, hsum_ref,
                   qa_ref, ka_ref, vat_ref, qb_ref, kb_ref, vbt_ref, stat_ref, carry_ref):
    i = pl.program_id(1)
    sub = vbt_ref.shape[3]
    lane = lax.broadcasted_iota(jnp.int32, (sub, LANES), 1)
    row = lax.broadcasted_iota(jnp.int32, (sub, LANES), 0)
    first_half = (lane % HEAD_DIM) < (HEAD_DIM // 2)
    low = lane < HEAD_DIM
    q_scale = LOG2E / math.sqrt(HEAD_DIM)

    @pl.when(i == 0)
    def _():
        carry_ref[...] = jnp.zeros(carry_ref.shape, F32)

    carry = carry_ref[0:1, :]
    for r in range(x_ref.shape[1] // sub):
        rows = slice(r * sub, (r + 1) * sub)
        h = _rms_mod(x_ref[0, rows, :], g_ref[...], ada_ref[0, 1:2, :], ada_ref[0, 0:1, :]).astype(BF16)

        def proj(lo, width):
            return lax.dot_general(h, w_ref[lo:lo + width, :], (((1,), (1,)), ((), ())),
                                   preferred_element_type=F32)

        def proj_t(lo, width):
            return lax.dot_general(wvt_ref[lo:lo + width, :], h, (((1,), (1,)), ((), ())),
                                   preferred_element_type=F32)

        t0 = i * x_ref.shape[1] + r * sub
        quarter = t0 // cos_ref.shape[1]
        t_rows = pl.ds(pl.multiple_of(t0 - quarter * cos_ref.shape[1], sub), sub)
        shift = (LANES - quarter * ROPE_FREQS) % LANES

        def spread(tab):
            y = jnp.where(lane < ROPE_FREQS, pltpu.roll(tab, shift, axis=1), 0.0)
            y = y + pltpu.roll(y, ROPE_FREQS, axis=1)
            return y + pltpu.roll(y, 2 * ROPE_FREQS, axis=1)

        cos = spread(cos_ref[0, t_rows, :])
        sin = spread(sin_ref[0, t_rows, :])
        sin_signed = jnp.where(first_half, -sin, sin)

        def rope(t):
            partner = jnp.where(first_half,
                                pltpu.roll(t, LANES - HEAD_DIM // 2, axis=1),
                                pltpu.roll(t, HEAD_DIM // 2, axis=1))
            return t * cos + partner * sin_signed

        kf = proj(A_Q_W + 2 * B_W, A_KV_W + LANES)
        ka_ref[0, rows, :] = rope(kf[:, :A_KV_W]).astype(BF16)
        fl = kf[:, A_KV_W:] + bf_ref[...]
        c = jnp.minimum(fl, 0.0) - jnp.log1p(jnp.exp(-jnp.abs(fl)))
        d = 1
        while d < sub:
            c = c + jnp.where(row >= d, pltpu.roll(c, d, axis=0), 0.0)
            d *= 2
        c = c + carry
        carry = c[sub - 1:sub, :]
        c = c * LOG2E
        hi = c.astype(BF16).astype(F32)
        r1 = c - hi
        mid = r1.astype(BF16).astype(F32)
        lo = r1 - mid
        packed = jnp.where(lane < B_HEADS, hi,
                           jnp.where(lane < 2 * B_HEADS, pltpu.roll(mid, B_HEADS, axis=1),
                                     jnp.where(lane < 3 * B_HEADS, pltpu.roll(lo, 2 * B_HEADS, axis=1), 0.0)))

        qa = proj(0, A_Q_W)
        for j in range(A_Q_W // LANES):
            qa_ref[0, rows, j * LANES:(j + 1) * LANES] = (
                rope(qa[:, j * LANES:(j + 1) * LANES]) * q_scale).astype(BF16)
        vat_ref[0, :, rows] = proj_t(0, A_KV_W).astype(BF16)
        vbt_ref[0, r] = proj_t(A_KV_W, B_W).astype(BF16)
        qb = proj(A_Q_W, B_W) * q_scale
        kb = proj(A_Q_W + B_W, B_W)
        aug = jnp.dot(packed.astype(BF16), place_ref[...], preferred_element_type=F32)
        def norm_bound(t):
            sq = t * t
            fold = sq[:NORM_ROWS]
            for g in range(1, sub // NORM_ROWS):
                fold = jnp.maximum(fold, sq[g * NORM_ROWS:(g + 1) * NORM_ROWS])
            per_head = jnp.dot(fold.astype(BF16), hsum_ref[...], preferred_element_type=F32)
            return jnp.max(per_head, axis=0, keepdims=True) * NORM_MARGIN

        stats = [norm_bound(qb), norm_bound(kb)]
        stats += [c[0:1, :], c[sub - 1:sub, :], jnp.zeros((4, LANES), F32)]
        stat_ref[0, r] = jnp.concatenate(stats, axis=0)
        for head in range(B_HEADS):
            base = _aug_base(head) % LANES
            own = low if head % 2 == 0 else jnp.logical_not(low)
            plus = (lane >= base) & (lane < base + AUG_PARTS)
            minus = (lane >= base + AUG_PARTS) & (lane < base + 2 * AUG_PARTS)
            a = aug[:, head * LANES:(head + 1) * LANES]
            src = slice(head // 2 * LANES, (head // 2 + 1) * LANES)
            group = slice(head * LANES, (head + 1) * LANES)
            qb_ref[0, rows, group] = jnp.where(own, qb[:, src],
                                               jnp.where(plus, a, jnp.where(minus, 1.0, 0.0))).astype(BF16)
            kb_ref[0, rows, group] = jnp.where(own, kb[:, src],
                                               jnp.where(minus, a, jnp.where(plus, 1.0, 0.0))).astype(BF16)
    carry_ref[0:1, :] = carry


def _in_proj(x, cos, sin, ada, g_pre, w_qk, w_vt, b_f, tm, sub):
    bsz, s, d = x.shape
    grid = (bsz, s // tm)
    tok = lambda w: pl.BlockSpec((1, tm, w), lambda b, i: (b, i, 0))
    const = lambda shape: pl.BlockSpec(shape, lambda b, i: (0,) * len(shape))
    act = lambda w: jax.ShapeDtypeStruct((bsz, s, w), BF16)
    place = _aug_placement()
    head_sum = jnp.asarray(np.kron(np.eye(B_HEADS, LANES), np.ones((HEAD_DIM, 1))), BF16)
    return pl.pallas_call(
        _inproj_kernel,
        grid=grid,
        in_specs=[tok(d), pl.BlockSpec((1,) + cos.shape[1:], lambda b, i: (b, 0, 0)),
                  pl.BlockSpec((1,) + sin.shape[1:], lambda b, i: (b, 0, 0)),
                  pl.BlockSpec((1, N_ADA, d), lambda b, i: (b, 0, 0)),
                  const((1, d)), _resident(w_qk.shape), _resident(w_vt.shape),
                  const((1, LANES)), const(place.shape), const(head_sum.shape)],
        out_specs=[tok(A_Q_W), tok(A_KV_W),
                   pl.BlockSpec((1, A_KV_W, tm), lambda b, i: (b, 0, i)),
                   tok(B_HEADS * LANES), tok(B_HEADS * LANES),
                   pl.BlockSpec((1, tm // sub, B_W, sub), lambda b, i: (b, i, 0, 0)),
                   pl.BlockSpec((1, tm // sub, 8, LANES), lambda b, i: (b, i, 0, 0))],
        out_shape=[act(A_Q_W), act(A_KV_W),
                   jax.ShapeDtypeStruct((bsz, A_KV_W, s), BF16),
                   act(B_HEADS * LANES), act(B_HEADS * LANES),
                   jax.ShapeDtypeStruct((bsz, s // sub, B_W, sub), BF16),
                   jax.ShapeDtypeStruct((bsz, s // sub, 8, LANES), F32)],
        scratch_shapes=[pltpu.VMEM((8, LANES), F32)],
        compiler_params=_params("parallel", "arbitrary"),
        name="in_proj",
    )(x, cos, sin, ada, g_pre, w_qk, w_vt, b_f, place, head_sum)


def _plan_kernel(st_ref, o_ref, *, ratio):
    nt = st_ref.shape[1]
    q_scale = LOG2E / math.sqrt(HEAD_DIM)
    for b in range(st_ref.shape[0]):
        qn = st_ref[b, 0, 0:1, :]
        kn = st_ref[b, 0, 1:2, :]
        for j in range(1, nt):
            qn = jnp.maximum(qn, st_ref[b, j, 0:1, :])
            kn = jnp.maximum(kn, st_ref[b, j, 1:2, :])
        spread = qn * (1.0 / q_scale) + kn * q_scale
        rows = []
        for blk in range(nt // ratio):
            limit = st_ref[b, blk * ratio, 2:3, :] + spread + SKIP_LOG2
            count = jnp.zeros((1, LANES), jnp.int32)
            prefix = jnp.ones((1, LANES), jnp.int32)
            for j in range(blk * ratio):
                prefix = prefix * (st_ref[b, j, 3:4, :] > limit).astype(jnp.int32)
                count = count + prefix
            rows.append(count)
        o_ref[b] = jnp.concatenate(rows, axis=0)


def _fox_plan(stats, ratio):
    bsz, nt, _, _ = stats.shape
    return pl.pallas_call(
        functools.partial(_plan_kernel, ratio=ratio),
        grid=(1,),
        in_specs=[pl.BlockSpec((bsz, nt, 8, LANES), lambda i: (0, 0, 0, 0))],
        out_specs=pl.BlockSpec((bsz, nt // ratio, LANES), lambda i: (0, 0, 0)),
        out_shape=jax.ShapeDtypeStruct((bsz, nt // ratio, LANES), jnp.int32),
        compiler_params=_params("arbitrary"),
        name="fox_plan",
    )(stats)


def _swa_kernel(q_ref, kp_ref, kc_ref, vtp_ref, vtc_ref, sink_ref, o_ref):
    i = pl.program_id(1)
    tq = q_ref.shape[1]
    nsub = tq // WINDOW
    ncol = A_Q_W // LANES
    lane_q = lax.broadcasted_iota(jnp.int32, (WINDOW, LANES), 1)
    low = lane_q < HEAD_DIM
    low_dim = lax.broadcasted_iota(jnp.int32, (LANES, WINDOW), 0) < HEAD_DIM
    k_loc = lax.broadcasted_iota(jnp.int32, (2 * WINDOW, WINDOW), 0)
    q_loc = lax.broadcasted_iota(jnp.int32, (2 * WINDOW, WINDOW), 1) + WINDOW
    rel = q_loc - k_loc
    band = (rel >= 0) & (rel < WINDOW)
    sink = sink_ref[...] * LOG2E
    ones = jnp.ones((ONES_ROWS, 2 * WINDOW), BF16)
    zero = jnp.zeros((WINDOW, LANES), BF16)

    def keys(j):
        if j == 0:
            return jnp.concatenate([kp_ref[0], kc_ref[0, :WINDOW, :]], axis=0)
        return kc_ref[0, (j - 1) * WINDOW:(j + 1) * WINDOW, :]

    def values_t(j):
        if j == 0:
            return jnp.concatenate([vtp_ref[0], vtc_ref[0, :, :WINDOW]], axis=1)
        return vtc_ref[0, :, (j - 1) * WINDOW:(j + 1) * WINDOW]

    def scores(j):
        q = q_ref[0, j * WINDOW:(j + 1) * WINDOW, :]
        parts = [jnp.where(low, q[:, c * LANES:(c + 1) * LANES], zero) for c in range(ncol)]
        parts += [jnp.where(low, zero, q[:, c * LANES:(c + 1) * LANES]) for c in range(ncol)]
        qs = jnp.concatenate(parts, axis=0)
        return lax.dot_general(keys(j), qs, (((1,), (1,)), ((), ())), preferred_element_type=F32)

    def softmax_pv(j, st):
        valid = band & ((k_loc >= WINDOW) | (i > 0)) if j == 0 else band
        ps, sink_terms = [], []
        for hh in range(A_Q_HEADS):
            cols = slice(hh * WINDOW, (hh + 1) * WINDOW)
            w = jnp.where(valid, st[:, cols], NEG_INF)
            m = jnp.maximum(jnp.max(w, axis=0, keepdims=True), sink[:, cols])
            ps.append(jnp.exp2(w - m).astype(BF16))
            sink_terms.append(jnp.exp2(sink[:, cols] - m))
        pt = jnp.concatenate(ps, axis=1)
        ot = jnp.dot(jnp.concatenate([values_t(j), ones], axis=0), pt, preferred_element_type=F32)
        for c in range(ncol):
            halves = []
            for hh in (c, ncol + c):
                cols = slice(hh * WINDOW, (hh + 1) * WINDOW)
                denom = ot[LANES:LANES + 1, cols] + sink_terms[hh]
                halves.append(ot[:LANES, cols] * (1.0 / denom))
            o_ref[0, j * WINDOW:(j + 1) * WINDOW, c * LANES:(c + 1) * LANES] = (
                jnp.transpose(jnp.where(low_dim, halves[0], halves[1])).astype(BF16))

    st = scores(0)
    for j in range(nsub):
        st_next = scores(j + 1) if j + 1 < nsub else None
        softmax_pv(j, st)
        st = st_next


def _swa(qa, ka, vat, sinks, tq):
    bsz, s, _ = qa.shape
    r = tq // WINDOW
    cur = lambda b, i: (b, i, 0)
    return pl.pallas_call(
        _swa_kernel,
        grid=(bsz, s // tq),
        in_specs=[pl.BlockSpec((1, tq, A_Q_W), cur),
                  pl.BlockSpec((1, WINDOW, A_KV_W), lambda b, i: (b, jnp.maximum(i * r - 1, 0), 0)),
                  pl.BlockSpec((1, tq, A_KV_W), cur),
                  pl.BlockSpec((1, A_KV_W, WINDOW), lambda b, i: (b, 0, jnp.maximum(i * r - 1, 0))),
                  pl.BlockSpec((1, A_KV_W, tq), lambda b, i: (b, 0, i)),
                  pl.BlockSpec((1, A_Q_HEADS * WINDOW), lambda b, i: (0, 0))],
        out_specs=pl.BlockSpec((1, tq, A_Q_W), cur),
        out_shape=jax.ShapeDtypeStruct((bsz, s, A_Q_W), BF16),
        compiler_params=_params("parallel", "parallel"),
        name="swa",
    )(qa, ka, ka, vat, vat, sinks)


def _fox_kernel(skip_ref, q_ref, k_ref, vt_ref, o_ref, m_ref, acc_ref, sta_ref, stb_ref):
    qi = pl.program_id(2)
    tq = q_ref.shape[1]
    tk = vt_ref.shape[3]
    nchunk = tq // LANES
    assert tq == 2 * tk
    m_ref[...] = jnp.full(m_ref.shape, NEG_INF, F32)
    acc_ref[...] = jnp.zeros(acc_ref.shape, F32)
    ones = jnp.ones((ONES_ROWS, tk), BF16)

    def scores(kj, st_ref, c0=0, c1=nchunk):
        start = pl.multiple_of(kj * tk, tk)
        for e in range(2):
            group = slice(e * LANES, (e + 1) * LANES)
            st_ref[e, :, c0 * LANES:c1 * LANES] = lax.dot_general(
                k_ref[0, pl.ds(start, tk), group], q_ref[0, c0 * LANES:c1 * LANES, group],
                (((1,), (1,)), ((), ())), preferred_element_type=F32)

    def softmax_pv(kj, st_ref, first_query=None, c0=0, c1=nchunk):
        for e in range(2):
            vta = jnp.concatenate([vt_ref[0, kj, e * HEAD_DIM:(e + 1) * HEAD_DIM, :], ones], axis=0)
            ps, alphas = [], []
            for c in range(c0, c1):
                cols = slice(c * LANES, (c + 1) * LANES)
                off = None if first_query is None else first_query + c * LANES
                masked = off is not None and off < tk
                live = min(off + LANES, tk) if masked else tk
                z = st_ref[e, :live, cols]
                if masked:
                    causal = (lax.broadcasted_iota(jnp.int32, (live, LANES), 0)
                              <= lax.broadcasted_iota(jnp.int32, (live, LANES), 1) + off)
                    z = jnp.where(causal, z, NEG_INF)
                m_old = m_ref[e, :, cols]
                m_new = jnp.maximum(m_old, jnp.max(z, axis=0, keepdims=True))
                p = jnp.exp2(z - m_new).astype(BF16)
                if live < tk:
                    p = jnp.concatenate([p, jnp.zeros((tk - live, LANES), BF16)], axis=0)
                ps.append(p)
                alphas.append(jnp.exp2(m_old - m_new))
                m_ref[e, :, cols] = m_new
            pt = jnp.concatenate(ps, axis=1)
            alpha = jnp.concatenate(alphas, axis=1)
            span = slice(c0 * LANES, c1 * LANES)
            acc_ref[e, :, span] = alpha * acc_ref[e, :, span] + jnp.dot(vta, pt, preferred_element_type=F32)

    half = tk // LANES
    base = ((pl.program_id(0) * pl.num_programs(2) + qi) * 2 * B_HEADS + 2 * pl.program_id(1))
    lead = jnp.minimum(skip_ref[base], skip_ref[base + 1])
    first = jnp.minimum(jnp.minimum(skip_ref[base + B_HEADS], skip_ref[base + B_HEADS + 1]), 2 * qi)
    first = jnp.maximum(first, lead)

    second = half
    short = (qi >= 1) & (lead == 2 * qi - 1) & (first == 2 * qi)

    @pl.when(short)
    def _():
        scores(2 * qi - 1, stb_ref, c1=half)
        scores(2 * qi, sta_ref)
        scores(2 * qi + 1, stb_ref, c0=second)
        softmax_pv(2 * qi - 1, stb_ref, c1=half)
        softmax_pv(2 * qi, sta_ref, first_query=0)
        softmax_pv(2 * qi + 1, stb_ref, first_query=-tk, c0=second)

    @pl.when(jnp.logical_not(short))
    def _():
        def early_half(u, carry):
            scores(u, stb_ref, c1=half)
            softmax_pv(u, stb_ref, c1=half)
            return carry

        lax.fori_loop(lead, first, early_half, 0)
        first_pair = (first + 1) // 2
        odd_start = first % 2 == 1

        @pl.when(odd_start)
        def _():
            scores(first, stb_ref)
            scores(first + 1, sta_ref)
            softmax_pv(first, stb_ref)

        @pl.when(jnp.logical_not(odd_start))
        def _():
            scores(first, sta_ref)

        def pair(t, carry):
            u = 2 * t
            scores(u + 1, stb_ref)
            softmax_pv(u, sta_ref)
            scores(u + 2, sta_ref)
            softmax_pv(u + 1, stb_ref)
            return carry

        lax.fori_loop(first_pair, qi, pair, 0)
        scores(2 * qi + 1, stb_ref, c0=second)
        softmax_pv(2 * qi, sta_ref, first_query=0)
        softmax_pv(2 * qi + 1, stb_ref, first_query=-tk, c0=second)

    out_t = jnp.concatenate(
        [acc_ref[e, :HEAD_DIM, :] * (1.0 / acc_ref[e, HEAD_DIM:HEAD_DIM + 1, :]) for e in range(2)], axis=0)
    o_ref[0] = jnp.transpose(out_t).astype(BF16)


def _fox(skip, qb, kb, vbt, tq):
    bsz, s, _ = qb.shape
    _, nk, _, tk = vbt.shape
    npair = B_HEADS // 2
    return pl.pallas_call(
        _fox_kernel,
        grid_spec=pltpu.PrefetchScalarGridSpec(
            num_scalar_prefetch=1,
            grid=(bsz, npair, s // tq),
            in_specs=[pl.BlockSpec((1, tq, 2 * LANES), lambda b, hp, i, skip: (b, i, hp)),
                      pl.BlockSpec((1, s, 2 * LANES), lambda b, hp, i, skip: (b, 0, hp)),
                      pl.BlockSpec((1, nk, 2 * HEAD_DIM, tk), lambda b, hp, i, skip: (b, 0, hp, 0))],
            out_specs=pl.BlockSpec((1, tq, 2 * HEAD_DIM), lambda b, hp, i, skip: (b, i, hp)),
            scratch_shapes=[pltpu.VMEM((2, 1, tq), F32),
                            pltpu.VMEM((2, HEAD_DIM + ONES_ROWS, tq), F32),
                            pltpu.VMEM((2, tk, tq), F32),
                            pltpu.VMEM((2, tk, tq), F32)]),
        out_shape=jax.ShapeDtypeStruct((bsz, s, B_W), BF16),
        compiler_params=_params("parallel", "arbitrary", "arbitrary"),
        name="fox",
    )(skip, qb, kb, vbt)


def _resident(shape):
    return pl.BlockSpec(shape, lambda b, i: (0,) * len(shape), pipeline_mode=pl.Buffered(1))


def _post_kernel(x_ref, oa_ref, ob_ref, ada_ref, gpre_ref, gpost_ref, wg_ref, wa_ref, wb_ref,
                 wo_ref, wfi_ref, wfo_ref, o_ref, wfi_out_ref, wfo_out_ref, *, sub):
    d = x_ref.shape[2]
    wfi_out_ref[...] = wfi_ref[...].astype(BF16)
    wfo_out_ref[...] = wfo_ref[...].astype(BF16)
    nt = (((1,), (1,)), ((), ()))
    merged = []
    for r in range(x_ref.shape[1] // sub):
        rows = slice(r * sub, (r + 1) * sub)
        h = _rms_mod(x_ref[0, rows, :], gpre_ref[...], ada_ref[0, 1:2, :], ada_ref[0, 0:1, :]).astype(BF16)
        a = jnp.dot(oa_ref[0, rows, :], wa_ref[...], preferred_element_type=F32)
        b = jnp.dot(ob_ref[0, rows, :], wb_ref[...], preferred_element_type=F32)
        ga = jax.nn.sigmoid(lax.dot_general(h, wg_ref[:d, :], nt, preferred_element_type=F32))
        gb = jax.nn.sigmoid(lax.dot_general(h, wg_ref[d:, :], nt, preferred_element_type=F32))
        merged.append((ga * a + gb * b).astype(BF16))
    for r in range(x_ref.shape[1] // sub):
        rows = slice(r * sub, (r + 1) * sub)
        y = jnp.dot(merged[r], wo_ref[...], preferred_element_type=F32)
        o_ref[0, rows, :] = x_ref[0, rows, :] + ada_ref[0, 2:3, :] * _rms(y, gpost_ref[...])


def _post(x, oa, ob, ada, g_pre, g_post, wg, wa, wb, wo, w_ffn_in, w_ffn_out, tm, sub):
    bsz, s, d = x.shape
    nt = s // tm
    steps = bsz * nt
    n_in = steps
    n_out = math.gcd(steps, w_ffn_out.shape[0] // BF16_ROWS)
    tok = lambda w: pl.BlockSpec((1, tm, w), lambda b, i: (b, i, 0))
    const = lambda shape: pl.BlockSpec(shape, lambda b, i: (0,) * len(shape))
    slab_in = pl.BlockSpec((d // n_in, w_ffn_in.shape[1]), lambda b, i: (b * nt + i, 0))
    slab_out = pl.BlockSpec((w_ffn_out.shape[0] // n_out, d), lambda b, i: ((b * nt + i) * n_out // steps, 0))
    return pl.pallas_call(
        functools.partial(_post_kernel, sub=sub),
        grid=(bsz, nt),
        in_specs=[tok(d), tok(A_Q_W), tok(B_W),
                  pl.BlockSpec((1, N_ADA, d), lambda b, i: (b, 0, 0)),
                  const((1, d)), const((1, d)),
                  _resident(wg.shape), _resident(wa.shape), _resident(wb.shape), _resident(wo.shape),
                  slab_in, slab_out],
        out_specs=[tok(d), slab_in, slab_out],
        out_shape=[jax.ShapeDtypeStruct((bsz, s, d), F32),
                   jax.ShapeDtypeStruct(w_ffn_in.shape, BF16),
                   jax.ShapeDtypeStruct(w_ffn_out.shape, BF16)],
        compiler_params=_params("arbitrary", "arbitrary"),
        name="post",
    )(x, oa, ob, ada, g_pre, g_post, wg, wa, wb, wo, w_ffn_in, w_ffn_out)


def _ffn_kernel(x_ref, ada_ref, gpre_ref, gpost_ref, wi_ref, wo_ref, o_ref, *, sub, chunks):
    d_ff = wo_ref.shape[0]
    for r in range(x_ref.shape[1] // sub):
        rows = slice(r * sub, (r + 1) * sub)
        x = x_ref[0, rows, :]
        h = _rms_mod(x, gpre_ref[...], ada_ref[0, 4:5, :], ada_ref[0, 3:4, :]).astype(BF16)
        acts = []
        lo = 0
        for width in chunks:
            g = jnp.dot(h, wi_ref[:, lo:lo + width], preferred_element_type=F32)
            u = jnp.dot(h, wi_ref[:, d_ff + lo:d_ff + lo + width], preferred_element_type=F32)
            acts.append((g * jax.nn.sigmoid(g) * u).astype(BF16))
            lo += width
        y = jnp.zeros(x.shape, F32)
        lo = 0
        for width, act in zip(chunks, acts):
            y = y + jnp.dot(act, wo_ref[lo:lo + width, :], preferred_element_type=F32)
            lo += width
        o_ref[0, rows, :] = x + ada_ref[0, 5:6, :] * _rms(y, gpost_ref[...])


def _ffn(x, ada, g_pre, g_post, wi, wo, tm, sub, chunks):
    bsz, s, d = x.shape
    tok = pl.BlockSpec((1, tm, d), lambda b, i: (b, i, 0))
    const = lambda shape: pl.BlockSpec(shape, lambda b, i: (0,) * len(shape))
    return pl.pallas_call(
        functools.partial(_ffn_kernel, sub=sub, chunks=chunks),
        grid=(bsz, s // tm),
        in_specs=[tok, pl.BlockSpec((1, N_ADA, d), lambda b, i: (b, 0, 0)),
                  const((1, d)), const((1, d)), _resident(wi.shape), _resident(wo.shape)],
        out_specs=tok,
        out_shape=jax.ShapeDtypeStruct((bsz, s, d), F32),
        compiler_params=_params("parallel", "parallel"),
        name="ffn",
    )(x, ada, g_pre, g_post, wi, wo)


def _pair_heads(w, axis):
    shape = w.shape
    pre, post = shape[:axis], shape[axis + 1:]
    w = w.reshape(pre + (A_KV_HEADS, A_GROUP, HEAD_DIM) + post)
    w = jnp.swapaxes(w, axis, axis + 1)
    return w.reshape(shape)


def kernel(x, c, positions, w_ada, b_ada, g_pre_mix, g_post_mix, w_in, b_f, sinks, w_branch_a,
           w_branch_b, w_out, g_pre_ffn, g_post_ffn, w_ffn_in, w_ffn_out):
    bsz, s, d = x.shape
    depth = w_ada.shape[0]
    tm = min(1024, s)
    sub = min(512, s)
    rep = LANES // ROPE_FREQS
    assert (s // rep) % sub == 0
    half = jnp.arange(0, HEAD_DIM, 2, dtype=F32) / HEAD_DIM
    inv_freq = jnp.tile(1.0 / (ROPE_THETA ** half), rep).reshape(1, LANES)
    pos_c = jnp.swapaxes(positions.astype(F32).reshape(bsz, rep, s // rep), 1, 2)
    pos_c = jnp.repeat(pos_c, ROPE_FREQS, axis=2)
    fine = sub // 2
    for l in range(depth):
        w_t = jnp.swapaxes(w_in[l], 0, 1)
        o_va, o_qb, o_vb = A_Q_W + A_KV_W, A_Q_W + 2 * A_KV_W, A_Q_W + 2 * A_KV_W + 2 * B_W
        w_f = jnp.pad(w_t[QKV_W:QKV_W + B_HEADS], ((0, LANES - B_HEADS), (0, 0)))
        w_qk = jnp.concatenate([_pair_heads(w_t[:A_Q_W], 0), w_t[o_qb:o_vb], w_t[A_Q_W:o_va], w_f],
                               axis=0).astype(BF16)
        w_vt = jnp.concatenate([w_t[o_va:o_qb], w_t[o_vb:QKV_W]], axis=0).astype(BF16)
        w_g = w_t[QKV_W + B_HEADS:].astype(BF16)
        bf_row = jnp.pad(b_f[l], (0, LANES - B_HEADS)).reshape(1, LANES)
        sink_row = jnp.repeat(sinks[l], WINDOW).reshape(1, A_Q_HEADS * WINDOW)
        w_a = _pair_heads(w_branch_a[l], 0).astype(BF16)
        w_b = w_branch_b[l].astype(BF16)
        w_o = w_out[l].astype(BF16)
        row = lambda g: g[l].reshape(1, d)

        if l == 0:
            ada, cos, sin = _ada_rope(c, w_ada[l], b_ada[l], pos_c, inv_freq)
        else:
            ada = _ada(c, w_ada[l], b_ada[l])
        ada = ada.reshape(bsz, N_ADA, d)
        qa, ka, vat, qb, kb, vbt, stats = _in_proj(x, cos, sin, ada, row(g_pre_mix), w_qk, w_vt, bf_row,
                                                   tm, sub)
        o_a = _swa(qa, ka, vat, sink_row, tm)
        skip = _fox_plan(stats, 1)[:, :, :B_HEADS].reshape(-1)
        o_b = _fox(skip, qb, kb, vbt, 2 * sub)
        x, w_fi, w_fo = _post(x, o_a, o_b, ada, row(g_pre_mix), row(g_post_mix), w_g, w_a, w_b, w_o,
                              w_ffn_in[l], w_ffn_out[l], tm, fine)
        n_tiles = w_fo.shape[0] // MXU_DEPTH
        chunks = ((n_tiles + 1) // 2 * MXU_DEPTH, n_tiles // 2 * MXU_DEPTH)
        x = _ffn(x, ada, row(g_pre_ffn), row(g_post_ffn), w_fi, w_fo, tm, fine, chunks)
    return x
```

```python
import functools
import math

import jax
import jax.numpy as jnp
import numpy as np
from jax import lax
from jax.experimental import pallas as pl
from jax.experimental.pallas import tpu as pltpu

HEAD_DIM = 64
WINDOW = 128
A_Q_HEADS = 8
A_KV_HEADS = 2
A_GROUP = A_Q_HEADS // A_KV_HEADS
B_HEADS = 8
N_ADA = 6
ROPE_THETA = 10000.0
RMS_EPS = 1e-6
LANES = 128
MXU_DEPTH = 256
ROPE_FREQS = HEAD_DIM // 2
BF16_ROWS = 16
VMEM_LIMIT = 56 * 1024 * 1024

A_Q_W = A_Q_HEADS * HEAD_DIM
A_KV_W = A_KV_HEADS * HEAD_DIM
B_W = B_HEADS * HEAD_DIM
QKV_W = A_Q_W + 2 * A_KV_W + 3 * B_W
BF16 = jnp.bfloat16
F32 = jnp.float32
NEG_INF = float("-inf")
LOG2E = math.log2(math.e)
ONES_ROWS = 16
SKIP_LOG2 = 48.0
NORM_MARGIN = 1.02
NORM_ROWS = 256


def _params(*sem, flags=None):
    return pltpu.CompilerParams(dimension_semantics=sem, vmem_limit_bytes=VMEM_LIMIT, flags=flags)


def _rms_mod(x, g, scale, shift):
    ms = jnp.mean(x * x, axis=-1, keepdims=True)
    return (x * lax.rsqrt(ms + RMS_EPS)) * (g * (1.0 + scale)) + shift


def _rms(y, g):
    ms = jnp.mean(y * y, axis=-1, keepdims=True)
    return y * lax.rsqrt(ms + RMS_EPS) * g


def _ada_kernel(c_ref, w_ref, b_ref, o_ref):
    o_ref[...] = jnp.dot(c_ref[...], w_ref[...], preferred_element_type=F32) + b_ref[...]


def _ada(c, w, b):
    bsz, d = c.shape
    n = w.shape[1]
    tn = n // 4
    return pl.pallas_call(
        _ada_kernel,
        grid=(n // tn,),
        in_specs=[pl.BlockSpec((bsz, d), lambda j: (0, 0)),
                  pl.BlockSpec((d, tn), lambda j: (0, j)),
                  pl.BlockSpec((1, tn), lambda j: (0, j))],
        out_specs=pl.BlockSpec((bsz, tn), lambda j: (0, j)),
        out_shape=jax.ShapeDtypeStruct((bsz, n), F32),
        compiler_params=_params("arbitrary"),
        name="ada",
    )(c, w, b.reshape(1, n))


def _ada_rope_kernel(c_ref, w_ref, b_ref, pos_ref, invf_ref, ada_ref, cos_ref, sin_ref):
    ada_ref[...] = jnp.dot(c_ref[...], w_ref[...], preferred_element_type=F32) + b_ref[...]
    ang = pos_ref[0] * invf_ref[...]
    cos_ref[0] = jnp.cos(ang)
    sin_ref[0] = jnp.sin(ang)


def _ada_rope(c, w, b, pos_c, inv_freq):
    bsz, d = c.shape
    n = w.shape[1]
    tn = n // bsz
    assert tn % LANES == 0
    rows = pos_c.shape[1]
    tab = pl.BlockSpec((1, rows, LANES), lambda j: (j, 0, 0))
    tab_shape = jax.ShapeDtypeStruct((bsz, rows, LANES), F32)
    return pl.pallas_call(
        _ada_rope_kernel,
        grid=(bsz,),
        in_specs=[pl.BlockSpec((bsz, d), lambda j: (0, 0)),
                  pl.BlockSpec((d, tn), lambda j: (0, j)),
                  pl.BlockSpec((1, tn), lambda j: (0, j)),
                  tab, pl.BlockSpec((1, LANES), lambda j: (0, 0))],
        out_specs=[pl.BlockSpec((bsz, tn), lambda j: (0, j)), tab, tab],
        out_shape=[jax.ShapeDtypeStruct((bsz, n), F32), tab_shape, tab_shape],
        compiler_params=_params("arbitrary"),
        name="ada_rope",
    )(c, w, b.reshape(1, n), pos_c, inv_freq)


AUG_PARTS = 3


def _aug_base(head):
    return head * LANES + (HEAD_DIM if head % 2 == 0 else 0)


def _aug_placement():
    p = np.zeros((LANES, B_HEADS * LANES), np.float32)
    for head in range(B_HEADS):
        for part in range(AUG_PARTS):
            p[part * B_HEADS + head, _aug_base(head) + part] = 1.0
            p[part * B_HEADS + head, _aug_base(head) + AUG_PARTS + part] = -1.0
    return jnp.asarray(p, BF16)


def _inproj_kernel(x_ref, cos_ref, sin_ref, ada_ref, g_ref, w_ref, wvt_ref, bf_ref, place_ref, hsum_ref,
                   qa_ref, ka_ref, vat_ref, qb_ref, kb_ref, vbt_ref, stat_ref, carry_ref):
    i = pl.program_id(1)
    sub = vbt_ref.shape[3]
    lane = lax.broadcasted_iota(jnp.int32, (sub, LANES), 1)
    row = lax.broadcasted_iota(jnp.int32, (sub, LANES), 0)
    first_half = (lane % HEAD_DIM) < (HEAD_DIM // 2)
    low = lane < HEAD_DIM
    q_scale = LOG2E / math.sqrt(HEAD_DIM)

    @pl.when(i == 0)
    def _():
        carry_ref[...] = jnp.zeros(carry_ref.shape, F32)

    carry = carry_ref[0:1, :]
    for r in range(x_ref.shape[1] // sub):
        rows = slice(r * sub, (r + 1) * sub)
        h = _rms_mod(x_ref[0, rows, :], g_ref[...], ada_ref[0, 1:2, :], ada_ref[0, 0:1, :]).astype(BF16)

        def proj(lo, width):
            return lax.dot_general(h, w_ref[lo:lo + width, :], (((1,), (1,)), ((), ())),
                                   preferred_element_type=F32)

        def proj_t(lo, width):
            return lax.dot_general(wvt_ref[lo:lo + width, :], h, (((1,), (1,)), ((), ())),
                                   preferred_element_type=F32)

        t0 = i * x_ref.shape[1] + r * sub
        quarter = t0 // cos_ref.shape[1]
        t_rows = pl.ds(pl.multiple_of(t0 - quarter * cos_ref.shape[1], sub), sub)
        shift = (LANES - quarter * ROPE_FREQS) % LANES

        def spread(tab):
            y = jnp.where(lane < ROPE_FREQS, pltpu.roll(tab, shift, axis=1), 0.0)
            y = y + pltpu.roll(y, ROPE_FREQS, axis=1)
            return y + pltpu.roll(y, 2 * ROPE_FREQS, axis=1)

        cos = spread(cos_ref[0, t_rows, :])
        sin = spread(sin_ref[0, t_rows, :])
        sin_signed = jnp.where(first_half, -sin, sin)

        def rope(t):
            partner = jnp.where(first_half,
                                pltpu.roll(t, LANES - HEAD_DIM // 2, axis=1),
                                pltpu.roll(t, HEAD_DIM // 2, axis=1))
            return t * cos + partner * sin_signed

        kf = proj(A_Q_W + 2 * B_W, A_KV_W + LANES)
        ka_ref[0, rows, :] = rope(kf[:, :A_KV_W]).astype(BF16)
        fl = kf[:, A_KV_W:] + bf_ref[...]
        c = jnp.minimum(fl, 0.0) - jnp.log1p(jnp.exp(-jnp.abs(fl)))
        d = 1
        while d < sub:
            c = c + jnp.where(row >= d, pltpu.roll(c, d, axis=0), 0.0)
            d *= 2
        c = c + carry
        carry = c[sub - 1:sub, :]
        c = c * LOG2E
        hi = c.astype(BF16).astype(F32)
        r1 = c - hi
        mid = r1.astype(BF16).astype(F32)
        lo = r1 - mid
        packed = jnp.where(lane < B_HEADS, hi,
                           jnp.where(lane < 2 * B_HEADS, pltpu.roll(mid, B_HEADS, axis=1),
                                     jnp.where(lane < 3 * B_HEADS, pltpu.roll(lo, 2 * B_HEADS, axis=1), 0.0)))

        qa = proj(0, A_Q_W)
        for j in range(A_Q_W // LANES):
            qa_ref[0, rows, j * LANES:(j + 1) * LANES] = (
                rope(qa[:, j * LANES:(j + 1) * LANES]) * q_scale).astype(BF16)
        vat_ref[0, :, rows] = proj_t(0, A_KV_W).astype(BF16)
        vbt_ref[0, r] = proj_t(A_KV_W, B_W).astype(BF16)
        qb = proj(A_Q_W, B_W) * q_scale
        kb = proj(A_Q_W + B_W, B_W)
        aug = jnp.dot(packed.astype(BF16), place_ref[...], preferred_element_type=F32)
        def norm_bound(t):
            sq = t * t
            fold = sq[:NORM_ROWS]
            for g in range(1, sub // NORM_ROWS):
                fold = jnp.maximum(fold, sq[g * NORM_ROWS:(g + 1) * NORM_ROWS])
            per_head = jnp.dot(fold.astype(BF16), hsum_ref[...], preferred_element_type=F32)
            return jnp.max(per_head, axis=0, keepdims=True) * NORM_MARGIN

        stats = [norm_bound(qb), norm_bound(kb)]
        stats += [c[0:1, :], c[sub - 1:sub, :], jnp.zeros((4, LANES), F32)]
        stat_ref[0, r] = jnp.concatenate(stats, axis=0)
        for head in range(B_HEADS):
            base = _aug_base(head) % LANES
            own = low if head % 2 == 0 else jnp.logical_not(low)
            plus = (lane >= base) & (lane < base + AUG_PARTS)
            minus = (lane >= base + AUG_PARTS) & (lane < base + 2 * AUG_PARTS)
            a = aug[:, head * LANES:(head + 1) * LANES]
            src = slice(head // 2 * LANES, (head // 2 + 1) * LANES)
            group = slice(head * LANES, (head + 1) * LANES)
            qb_ref[0, rows, group] = jnp.where(own, qb[:, src],
                                               jnp.where(plus, a, jnp.where(minus, 1.0, 0.0))).astype(BF16)
            kb_ref[0, rows, group] = jnp.where(own, kb[:, src],
                                               jnp.where(minus, a, jnp.where(plus, 1.0, 0.0))).astype(BF16)
    carry_ref[0:1, :] = carry


def _in_proj(x, cos, sin, ada, g_pre, w_qk, w_vt, b_f, tm, sub):
    bsz, s, d = x.shape
    grid = (bsz, s // tm)
    tok = lambda w: pl.BlockSpec((1, tm, w), lambda b, i: (b, i, 0))
    const = lambda shape: pl.BlockSpec(shape, lambda b, i: (0,) * len(shape))
    act = lambda w: jax.ShapeDtypeStruct((bsz, s, w), BF16)
    place = _aug_placement()
    head_sum = jnp.asarray(np.kron(np.eye(B_HEADS, LANES), np.ones((HEAD_DIM, 1))), BF16)
    return pl.pallas_call(
        _inproj_kernel,
        grid=grid,
        in_specs=[tok(d), pl.BlockSpec((1,) + cos.shape[1:], lambda b, i: (b, 0, 0)),
                  pl.BlockSpec((1,) + sin.shape[1:], lambda b, i: (b, 0, 0)),
                  pl.BlockSpec((1, N_ADA, d), lambda b, i: (b, 0, 0)),
                  const((1, d)), _resident(w_qk.shape), _resident(w_vt.shape),
                  const((1, LANES)), const(place.shape), const(head_sum.shape)],
        out_specs=[tok(A_Q_W), tok(A_KV_W),
                   pl.BlockSpec((1, A_KV_W, tm), lambda b, i: (b, 0, i)),
                   tok(B_HEADS * LANES), tok(B_HEADS * LANES),
                   pl.BlockSpec((1, tm // sub, B_W, sub), lambda b, i: (b, i, 0, 0)),
                   pl.BlockSpec((1, tm // sub, 8, LANES), lambda b, i: (b, i, 0, 0))],
        out_shape=[act(A_Q_W), act(A_KV_W),
                   jax.ShapeDtypeStruct((bsz, A_KV_W, s), BF16),
                   act(B_HEADS * LANES), act(B_HEADS * LANES),
                   jax.ShapeDtypeStruct((bsz, s // sub, B_W, sub), BF16),
                   jax.ShapeDtypeStruct((bsz, s // sub, 8, LANES), F32)],
        scratch_shapes=[pltpu.VMEM((8, LANES), F32)],
        compiler_params=_params("parallel", "arbitrary"),
        name="in_proj",
    )(x, cos, sin, ada, g_pre, w_qk, w_vt, b_f, place, head_sum)


def _plan_kernel(st_ref, o_ref, *, ratio):
    nt = st_ref.shape[1]
    q_scale = LOG2E / math.sqrt(HEAD_DIM)
    for b in range(st_ref.shape[0]):
        qn = st_ref[b, 0, 0:1, :]
        kn = st_ref[b, 0, 1:2, :]
        for j in range(1, nt):
            qn = jnp.maximum(qn, st_ref[b, j, 0:1, :])
            kn = jnp.maximum(kn, st_ref[b, j, 1:2, :])
        spread = qn * (1.0 / q_scale) + kn * q_scale
        rows = []
        for blk in range(nt // ratio):
            limit = st_ref[b, blk * ratio, 2:3, :] + spread + SKIP_LOG2
            count = jnp.zeros((1, LANES), jnp.int32)
            prefix = jnp.ones((1, LANES), jnp.int32)
            for j in range(blk * ratio):
                prefix = prefix * (st_ref[b, j, 3:4, :] > limit).astype(jnp.int32)
                count = count + prefix
            rows.append(count)
        o_ref[b] = jnp.concatenate(rows, axis=0)


def _fox_plan(stats, ratio):
    bsz, nt, _, _ = stats.shape
    return pl.pallas_call(
        functools.partial(_plan_kernel, ratio=ratio),
        grid=(1,),
        in_specs=[pl.BlockSpec((bsz, nt, 8, LANES), lambda i: (0, 0, 0, 0))],
        out_specs=pl.BlockSpec((bsz, nt // ratio, LANES), lambda i: (0, 0, 0)),
        out_shape=jax.ShapeDtypeStruct((bsz, nt // ratio, LANES), jnp.int32),
        compiler_params=_params("arbitrary"),
        name="fox_plan",
    )(stats)


def _swa_kernel(q_ref, kp_ref, kc_ref, vtp_ref, vtc_ref, sink_ref, o_ref):
    i = pl.program_id(1)
    tq = q_ref.shape[1]
    nsub = tq // WINDOW
    ncol = A_Q_W // LANES
    lane_q = lax.broadcasted_iota(jnp.int32, (WINDOW, LANES), 1)
    low = lane_q < HEAD_DIM
    low_dim = lax.broadcasted_iota(jnp.int32, (LANES, WINDOW), 0) < HEAD_DIM
    k_loc = lax.broadcasted_iota(jnp.int32, (2 * WINDOW, WINDOW), 0)
    q_loc = lax.broadcasted_iota(jnp.int32, (2 * WINDOW, WINDOW), 1) + WINDOW
    rel = q_loc - k_loc
    band = (rel >= 0) & (rel < WINDOW)
    sink = sink_ref[...] * LOG2E
    ones = jnp.ones((ONES_ROWS, 2 * WINDOW), BF16)
    zero = jnp.zeros((WINDOW, LANES), BF16)

    def keys(j):
        if j == 0:
            return jnp.concatenate([kp_ref[0], kc_ref[0, :WINDOW, :]], axis=0)
        return kc_ref[0, (j - 1) * WINDOW:(j + 1) * WINDOW, :]

    def values_t(j):
        if j == 0:
            return jnp.concatenate([vtp_ref[0], vtc_ref[0, :, :WINDOW]], axis=1)
        return vtc_ref[0, :, (j - 1) * WINDOW:(j + 1) * WINDOW]

    def scores(j):
        q = q_ref[0, j * WINDOW:(j + 1) * WINDOW, :]
        parts = [jnp.where(low, q[:, c * LANES:(c + 1) * LANES], zero) for c in range(ncol)]
        parts += [jnp.where(low, zero, q[:, c * LANES:(c + 1) * LANES]) for c in range(ncol)]
        qs = jnp.concatenate(parts, axis=0)
        return lax.dot_general(keys(j), qs, (((1,), (1,)), ((), ())), preferred_element_type=F32)

    def softmax_pv(j, st):
        valid = band & ((k_loc >= WINDOW) | (i > 0)) if j == 0 else band
        ps, sink_terms = [], []
        for hh in range(A_Q_HEADS):
            cols = slice(hh * WINDOW, (hh + 1) * WINDOW)
            w = jnp.where(valid, st[:, cols], NEG_INF)
            m = jnp.maximum(jnp.max(w, axis=0, keepdims=True), sink[:, cols])
            ps.append(jnp.exp2(w - m).astype(BF16))
            sink_terms.append(jnp.exp2(sink[:, cols] - m))
        pt = jnp.concatenate(ps, axis=1)
        ot = jnp.dot(jnp.concatenate([values_t(j), ones], axis=0), pt, preferred_element_type=F32)
        for c in range(ncol):
            halves = []
            for hh in (c, ncol + c):
                cols = slice(hh * WINDOW, (hh + 1) * WINDOW)
                denom = ot[LANES:LANES + 1, cols] + sink_terms[hh]
                halves.append(ot[:LANES, cols] * (1.0 / denom))
            o_ref[0, j * WINDOW:(j + 1) * WINDOW, c * LANES:(c + 1) * LANES] = (
                jnp.transpose(jnp.where(low_dim, halves[0], halves[1])).astype(BF16))

    st = scores(0)
    for j in range(nsub):
        st_next = scores(j + 1) if j + 1 < nsub else None
        softmax_pv(j, st)
        st = st_next


def _swa(qa, ka, vat, sinks, tq):
    bsz, s, _ = qa.shape
    r = tq // WINDOW
    cur = lambda b, i: (b, i, 0)
    return pl.pallas_call(
        _swa_kernel,
        grid=(bsz, s // tq),
        in_specs=[pl.BlockSpec((1, tq, A_Q_W), cur),
                  pl.BlockSpec((1, WINDOW, A_KV_W), lambda b, i: (b, jnp.maximum(i * r - 1, 0), 0)),
                  pl.BlockSpec((1, tq, A_KV_W), cur),
                  pl.BlockSpec((1, A_KV_W, WINDOW), lambda b, i: (b, 0, jnp.maximum(i * r - 1, 0))),
                  pl.BlockSpec((1, A_KV_W, tq), lambda b, i: (b, 0, i)),
                  pl.BlockSpec((1, A_Q_HEADS * WINDOW), lambda b, i: (0, 0))],
        out_specs=pl.BlockSpec((1, tq, A_Q_W), cur),
        out_shape=jax.ShapeDtypeStruct((bsz, s, A_Q_W), BF16),
        compiler_params=_params("parallel", "parallel"),
        name="swa",
    )(qa, ka, ka, vat, vat, sinks)


def _fox_kernel(skip_ref, q_ref, k_ref, vt_ref, o_ref, *scratch, tq):
    nq = q_ref.shape[1] // tq

    def tile(qi, carry):
        _fox_tile(qi, tq, nq, skip_ref, q_ref, k_ref, vt_ref, o_ref, *scratch)
        return carry

    lax.fori_loop(0, nq, tile, 0)


def _fox_tile(qi, tq, nq, skip_ref, q_ref, k_ref, vt_ref, o_ref, m_ref, acc_ref, sta_ref, stb_ref):
    tk = vt_ref.shape[3]
    nchunk = tq // LANES
    q_row0 = pl.multiple_of(qi * tq, tq)
    assert tq == 2 * tk
    m_ref[...] = jnp.full(m_ref.shape, NEG_INF, F32)
    acc_ref[...] = jnp.zeros(acc_ref.shape, F32)
    ones = jnp.ones((ONES_ROWS, tk), BF16)

    def scores(kj, st_ref, c0=0, c1=nchunk):
        start = pl.multiple_of(kj * tk, tk)
        for e in range(2):
            group = slice(e * LANES, (e + 1) * LANES)
            st_ref[e, :, c0 * LANES:c1 * LANES] = lax.dot_general(
                k_ref[0, pl.ds(start, tk), group],
                q_ref[0, pl.ds(pl.multiple_of(q_row0 + c0 * LANES, LANES), (c1 - c0) * LANES), group],
                (((1,), (1,)), ((), ())), preferred_element_type=F32)

    def softmax_pv(kj, st_ref, first_query=None, c0=0, c1=nchunk):
        for e in range(2):
            vta = jnp.concatenate([vt_ref[0, kj, e * HEAD_DIM:(e + 1) * HEAD_DIM, :], ones], axis=0)
            ps, alphas = [], []
            for c in range(c0, c1):
                cols = slice(c * LANES, (c + 1) * LANES)
                off = None if first_query is None else first_query + c * LANES
                masked = off is not None and off < tk
                live = min(off + LANES, tk) if masked else tk
                z = st_ref[e, :live, cols]
                if masked:
                    causal = (lax.broadcasted_iota(jnp.int32, (live, LANES), 0)
                              <= lax.broadcasted_iota(jnp.int32, (live, LANES), 1) + off)
                    z = jnp.where(causal, z, NEG_INF)
                m_old = m_ref[e, :, cols]
                m_new = jnp.maximum(m_old, jnp.max(z, axis=0, keepdims=True))
                p = jnp.exp2(z - m_new).astype(BF16)
                if live < tk:
                    p = jnp.concatenate([p, jnp.zeros((tk - live, LANES), BF16)], axis=0)
                ps.append(p)
                alphas.append(jnp.exp2(m_old - m_new))
                m_ref[e, :, cols] = m_new
            pt = jnp.concatenate(ps, axis=1)
            alpha = jnp.concatenate(alphas, axis=1)
            span = slice(c0 * LANES, c1 * LANES)
            acc_ref[e, :, span] = alpha * acc_ref[e, :, span] + jnp.dot(vta, pt, preferred_element_type=F32)

    half = tk // LANES
    base = ((pl.program_id(0) * nq + qi) * 2 * B_HEADS + 2 * pl.program_id(1))
    lead = jnp.minimum(skip_ref[base], skip_ref[base + 1])
    first = jnp.minimum(jnp.minimum(skip_ref[base + B_HEADS], skip_ref[base + B_HEADS + 1]), 2 * qi)
    first = jnp.maximum(first, lead)

    second = half
    short = (qi >= 1) & (lead == 2 * qi - 1) & (first == 2 * qi)

    @pl.when(short)
    def _():
        scores(2 * qi - 1, stb_ref, c1=half)
        scores(2 * qi, sta_ref)
        scores(2 * qi + 1, stb_ref, c0=second)
        softmax_pv(2 * qi - 1, stb_ref, c1=half)
        softmax_pv(2 * qi, sta_ref, first_query=0)
        softmax_pv(2 * qi + 1, stb_ref, first_query=-tk, c0=second)

    @pl.when(jnp.logical_not(short))
    def _():
        def early_half(u, carry):
            scores(u, stb_ref, c1=half)
            softmax_pv(u, stb_ref, c1=half)
            return carry

        lax.fori_loop(lead, first, early_half, 0)
        first_pair = (first + 1) // 2
        odd_start = first % 2 == 1

        @pl.when(odd_start)
        def _():
            scores(first, stb_ref)
            scores(first + 1, sta_ref)
            softmax_pv(first, stb_ref)

        @pl.when(jnp.logical_not(odd_start))
        def _():
            scores(first, sta_ref)

        def pair(t, carry):
            u = 2 * t
            scores(u + 1, stb_ref)
            softmax_pv(u, sta_ref)
            scores(u + 2, sta_ref)
            softmax_pv(u + 1, stb_ref)
            return carry

        lax.fori_loop(first_pair, qi, pair, 0)
        scores(2 * qi + 1, stb_ref, c0=second)
        softmax_pv(2 * qi, sta_ref, first_query=0)
        softmax_pv(2 * qi + 1, stb_ref, first_query=-tk, c0=second)

    out_t = jnp.concatenate(
        [acc_ref[e, :HEAD_DIM, :] * (1.0 / acc_ref[e, HEAD_DIM:HEAD_DIM + 1, :]) for e in range(2)], axis=0)
    o_ref[0, pl.ds(q_row0, tq), :] = jnp.transpose(out_t).astype(BF16)


def _fox(skip, qb, kb, vbt, tq):
    bsz, s, _ = qb.shape
    _, nk, _, tk = vbt.shape
    npair = B_HEADS // 2
    whole = pl.BlockSpec((1, s, 2 * LANES), lambda b, hp, skip: (b, 0, hp))
    return pl.pallas_call(
        functools.partial(_fox_kernel, tq=tq),
        grid_spec=pltpu.PrefetchScalarGridSpec(
            num_scalar_prefetch=1,
            grid=(bsz, npair),
            in_specs=[whole, whole,
                      pl.BlockSpec((1, nk, 2 * HEAD_DIM, tk), lambda b, hp, skip: (b, 0, hp, 0))],
            out_specs=pl.BlockSpec((1, s, 2 * HEAD_DIM), lambda b, hp, skip: (b, 0, hp)),
            scratch_shapes=[pltpu.VMEM((2, 1, tq), F32),
                            pltpu.VMEM((2, HEAD_DIM + ONES_ROWS, tq), F32),
                            pltpu.VMEM((2, tk, tq), F32),
                            pltpu.VMEM((2, tk, tq), F32)]),
        out_shape=jax.ShapeDtypeStruct((bsz, s, B_W), BF16),
        compiler_params=_params("parallel", "arbitrary"),
        name="fox",
    )(skip, qb, kb, vbt)


def _resident(shape):
    return pl.BlockSpec(shape, lambda b, i: (0,) * len(shape), pipeline_mode=pl.Buffered(1))


def _post_kernel(x_ref, oa_ref, ob_ref, ada_ref, gpre_ref, gpost_ref, wg_ref, wa_ref, wb_ref,
                 wo_ref, wfi_ref, wfo_ref, o_ref, wfi_out_ref, wfo_out_ref, *, sub):
    d = x_ref.shape[2]
    wfi_out_ref[...] = wfi_ref[...].astype(BF16)
    wfo_out_ref[...] = wfo_ref[...].astype(BF16)
    nt = (((1,), (1,)), ((), ()))
    merged = []
    for r in range(x_ref.shape[1] // sub):
        rows = slice(r * sub, (r + 1) * sub)
        h = _rms_mod(x_ref[0, rows, :], gpre_ref[...], ada_ref[0, 1:2, :], ada_ref[0, 0:1, :]).astype(BF16)
        a = jnp.dot(oa_ref[0, rows, :], wa_ref[...], preferred_element_type=F32)
        b = jnp.dot(ob_ref[0, rows, :], wb_ref[...], preferred_element_type=F32)
        ga = jax.nn.sigmoid(lax.dot_general(h, wg_ref[:d, :], nt, preferred_element_type=F32))
        gb = jax.nn.sigmoid(lax.dot_general(h, wg_ref[d:, :], nt, preferred_element_type=F32))
        merged.append((ga * a + gb * b).astype(BF16))
    for r in range(x_ref.shape[1] // sub):
        rows = slice(r * sub, (r + 1) * sub)
        y = jnp.dot(merged[r], wo_ref[...], preferred_element_type=F32)
        o_ref[0, rows, :] = x_ref[0, rows, :] + ada_ref[0, 2:3, :] * _rms(y, gpost_ref[...])


def _post(x, oa, ob, ada, g_pre, g_post, wg, wa, wb, wo, w_ffn_in, w_ffn_out, tm, sub):
    bsz, s, d = x.shape
    nt = s // tm
    steps = bsz * nt
    n_in = steps
    n_out = math.gcd(steps, w_ffn_out.shape[0] // BF16_ROWS)
    tok = lambda w: pl.BlockSpec((1, tm, w), lambda b, i: (b, i, 0))
    const = lambda shape: pl.BlockSpec(shape, lambda b, i: (0,) * len(shape))
    slab_in = pl.BlockSpec((d // n_in, w_ffn_in.shape[1]), lambda b, i: (b * nt + i, 0))
    slab_out = pl.BlockSpec((w_ffn_out.shape[0] // n_out, d), lambda b, i: ((b * nt + i) * n_out // steps, 0))
    return pl.pallas_call(
        functools.partial(_post_kernel, sub=sub),
        grid=(bsz, nt),
        in_specs=[tok(d), tok(A_Q_W), tok(B_W),
                  pl.BlockSpec((1, N_ADA, d), lambda b, i: (b, 0, 0)),
                  const((1, d)), const((1, d)),
                  _resident(wg.shape), _resident(wa.shape), _resident(wb.shape), _resident(wo.shape),
                  slab_in, slab_out],
        out_specs=[tok(d), slab_in, slab_out],
        out_shape=[jax.ShapeDtypeStruct((bsz, s, d), F32),
                   jax.ShapeDtypeStruct(w_ffn_in.shape, BF16),
                   jax.ShapeDtypeStruct(w_ffn_out.shape, BF16)],
        compiler_params=_params("arbitrary", "arbitrary"),
        name="post",
    )(x, oa, ob, ada, g_pre, g_post, wg, wa, wb, wo, w_ffn_in, w_ffn_out)


def _ffn_kernel(x_ref, ada_ref, gpre_ref, gpost_ref, wi_ref, wo_ref, o_ref, *, sub, chunks):
    d_ff = wo_ref.shape[0]
    for r in range(x_ref.shape[1] // sub):
        rows = slice(r * sub, (r + 1) * sub)
        x = x_ref[0, rows, :]
        h = _rms_mod(x, gpre_ref[...], ada_ref[0, 4:5, :], ada_ref[0, 3:4, :]).astype(BF16)
        acts = []
        lo = 0
        for width in chunks:
            g = jnp.dot(h, wi_ref[:, lo:lo + width], preferred_element_type=F32)
            u = jnp.dot(h, wi_ref[:, d_ff + lo:d_ff + lo + width], preferred_element_type=F32)
            acts.append((g * jax.nn.sigmoid(g) * u).astype(BF16))
            lo += width
        y = jnp.zeros(x.shape, F32)
        lo = 0
        for width, act in zip(chunks, acts):
            y = y + jnp.dot(act, wo_ref[lo:lo + width, :], preferred_element_type=F32)
            lo += width
        o_ref[0, rows, :] = x + ada_ref[0, 5:6, :] * _rms(y, gpost_ref[...])


def _ffn(x, ada, g_pre, g_post, wi, wo, tm, sub, chunks):
    bsz, s, d = x.shape
    tok = pl.BlockSpec((1, tm, d), lambda b, i: (b, i, 0))
    const = lambda shape: pl.BlockSpec(shape, lambda b, i: (0,) * len(shape))
    return pl.pallas_call(
        functools.partial(_ffn_kernel, sub=sub, chunks=chunks),
        grid=(bsz, s // tm),
        in_specs=[tok, pl.BlockSpec((1, N_ADA, d), lambda b, i: (b, 0, 0)),
                  const((1, d)), const((1, d)), _resident(wi.shape), _resident(wo.shape)],
        out_specs=tok,
        out_shape=jax.ShapeDtypeStruct((bsz, s, d), F32),
        compiler_params=_params("parallel", "parallel"),
        name="ffn",
    )(x, ada, g_pre, g_post, wi, wo)


def _pair_heads(w, axis):
    shape = w.shape
    pre, post = shape[:axis], shape[axis + 1:]
    w = w.reshape(pre + (A_KV_HEADS, A_GROUP, HEAD_DIM) + post)
    w = jnp.swapaxes(w, axis, axis + 1)
    return w.reshape(shape)


def kernel(x, c, positions, w_ada, b_ada, g_pre_mix, g_post_mix, w_in, b_f, sinks, w_branch_a,
           w_branch_b, w_out, g_pre_ffn, g_post_ffn, w_ffn_in, w_ffn_out):
    bsz, s, d = x.shape
    depth = w_ada.shape[0]
    tm = min(1024, s)
    sub = min(512, s)
    rep = LANES // ROPE_FREQS
    assert (s // rep) % sub == 0
    half = jnp.arange(0, HEAD_DIM, 2, dtype=F32) / HEAD_DIM
    inv_freq = jnp.tile(1.0 / (ROPE_THETA ** half), rep).reshape(1, LANES)
    pos_c = jnp.swapaxes(positions.astype(F32).reshape(bsz, rep, s // rep), 1, 2)
    pos_c = jnp.repeat(pos_c, ROPE_FREQS, axis=2)
    fine = sub // 2
    for l in range(depth):
        w_t = jnp.swapaxes(w_in[l], 0, 1)
        o_va, o_qb, o_vb = A_Q_W + A_KV_W, A_Q_W + 2 * A_KV_W, A_Q_W + 2 * A_KV_W + 2 * B_W
        w_f = jnp.pad(w_t[QKV_W:QKV_W + B_HEADS], ((0, LANES - B_HEADS), (0, 0)))
        w_qk = jnp.concatenate([_pair_heads(w_t[:A_Q_W], 0), w_t[o_qb:o_vb], w_t[A_Q_W:o_va], w_f],
                               axis=0).astype(BF16)
        w_vt = jnp.concatenate([w_t[o_va:o_qb], w_t[o_vb:QKV_W]], axis=0).astype(BF16)
        w_g = w_t[QKV_W + B_HEADS:].astype(BF16)
        bf_row = jnp.pad(b_f[l], (0, LANES - B_HEADS)).reshape(1, LANES)
        sink_row = jnp.repeat(sinks[l], WINDOW).reshape(1, A_Q_HEADS * WINDOW)
        w_a = _pair_heads(w_branch_a[l], 0).astype(BF16)
        w_b = w_branch_b[l].astype(BF16)
        w_o = w_out[l].astype(BF16)
        row = lambda g: g[l].reshape(1, d)

        if l == 0:
            ada, cos, sin = _ada_rope(c, w_ada[l], b_ada[l], pos_c, inv_freq)
        else:
            ada = _ada(c, w_ada[l], b_ada[l])
        ada = ada.reshape(bsz, N_ADA, d)
        qa, ka, vat, qb, kb, vbt, stats = _in_proj(x, cos, sin, ada, row(g_pre_mix), w_qk, w_vt, bf_row,
                                                   tm, sub)
        o_a = _swa(qa, ka, vat, sink_row, min(2 * tm, s))
        skip = _fox_plan(stats, 1)[:, :, :B_HEADS].reshape(-1)
        o_b = _fox(skip, qb, kb, vbt, 2 * sub)
        x, w_fi, w_fo = _post(x, o_a, o_b, ada, row(g_pre_mix), row(g_post_mix), w_g, w_a, w_b, w_o,
                              w_ffn_in[l], w_ffn_out[l], tm, fine)
        n_tiles = w_fo.shape[0] // MXU_DEPTH
        chunks = ((n_tiles + 1) // 2 * MXU_DEPTH, n_tiles // 2 * MXU_DEPTH)
        x = _ffn(x, ada, row(g_pre_ffn), row(g_post_ffn), w_fi, w_fo, tm, fine, chunks)
    return x
```

```python
import functools
import math

import jax
import jax.numpy as jnp
import numpy as np
from jax import lax
from jax.experimental import pallas as pl
from jax.experimental.pallas import tpu as pltpu

HEAD_DIM = 64
WINDOW = 128
A_Q_HEADS = 8
A_KV_HEADS = 2
A_GROUP = A_Q_HEADS // A_KV_HEADS
B_HEADS = 8
N_ADA = 6
ROPE_THETA = 10000.0
RMS_EPS = 1e-6
LANES = 128
MXU_DEPTH = 256
ROPE_FREQS = HEAD_DIM // 2
BF16_ROWS = 16
VMEM_LIMIT = 56 * 1024 * 1024

A_Q_W = A_Q_HEADS * HEAD_DIM
A_KV_W = A_KV_HEADS * HEAD_DIM
B_W = B_HEADS * HEAD_DIM
QKV_W = A_Q_W + 2 * A_KV_W + 3 * B_W
BF16 = jnp.bfloat16
F32 = jnp.float32
NEG_INF = float("-inf")
LOG2E = math.log2(math.e)
ONES_ROWS = 16
SKIP_LOG2 = 48.0
NORM_MARGIN = 1.02
NORM_ROWS = 256


def _params(*sem, flags=None):
    return pltpu.CompilerParams(dimension_semantics=sem, vmem_limit_bytes=VMEM_LIMIT, flags=flags)


def _rms_mod(x, g, scale, shift):
    ms = jnp.mean(x * x, axis=-1, keepdims=True)
    return (x * lax.rsqrt(ms + RMS_EPS)) * (g * (1.0 + scale)) + shift


def _rms(y, g):
    ms = jnp.mean(y * y, axis=-1, keepdims=True)
    return y * lax.rsqrt(ms + RMS_EPS) * g


def _ada_kernel(c_ref, w_ref, b_ref, o_ref):
    o_ref[...] = jnp.dot(c_ref[...], w_ref[...], preferred_element_type=F32) + b_ref[...]


def _ada(c, w, b):
    bsz, d = c.shape
    n = w.shape[1]
    tn = n // 4
    return pl.pallas_call(
        _ada_kernel,
        grid=(n // tn,),
        in_specs=[pl.BlockSpec((bsz, d), lambda j: (0, 0)),
                  pl.BlockSpec((d, tn), lambda j: (0, j)),
                  pl.BlockSpec((1, tn), lambda j: (0, j))],
        out_specs=pl.BlockSpec((bsz, tn), lambda j: (0, j)),
        out_shape=jax.ShapeDtypeStruct((bsz, n), F32),
        compiler_params=_params("arbitrary"),
        name="ada",
    )(c, w, b.reshape(1, n))


def _ada_rope_kernel(c_ref, w_ref, b_ref, pos_ref, invf_ref, ada_ref, cos_ref, sin_ref):
    ada_ref[...] = jnp.dot(c_ref[...], w_ref[...], preferred_element_type=F32) + b_ref[...]
    ang = pos_ref[0] * invf_ref[...]
    cos_ref[0] = jnp.cos(ang)
    sin_ref[0] = jnp.sin(ang)


def _ada_rope(c, w, b, pos_c, inv_freq):
    bsz, d = c.shape
    n = w.shape[1]
    tn = n // bsz
    assert tn % LANES == 0
    rows = pos_c.shape[1]
    tab = pl.BlockSpec((1, rows, LANES), lambda j: (j, 0, 0))
    tab_shape = jax.ShapeDtypeStruct((bsz, rows, LANES), F32)
    return pl.pallas_call(
        _ada_rope_kernel,
        grid=(bsz,),
        in_specs=[pl.BlockSpec((bsz, d), lambda j: (0, 0)),
                  pl.BlockSpec((d, tn), lambda j: (0, j)),
                  pl.BlockSpec((1, tn), lambda j: (0, j)),
                  tab, pl.BlockSpec((1, LANES), lambda j: (0, 0))],
        out_specs=[pl.BlockSpec((bsz, tn), lambda j: (0, j)), tab, tab],
        out_shape=[jax.ShapeDtypeStruct((bsz, n), F32), tab_shape, tab_shape],
        compiler_params=_params("arbitrary"),
        name="ada_rope",
    )(c, w, b.reshape(1, n), pos_c, inv_freq)


AUG_PARTS = 3


def _aug_base(head):
    return head * LANES + (HEAD_DIM if head % 2 == 0 else 0)


def _aug_placement():
    p = np.zeros((LANES, B_HEADS * LANES), np.float32)
    for head in range(B_HEADS):
        for part in range(AUG_PARTS):
            p[part * B_HEADS + head, _aug_base(head) + part] = 1.0
            p[part * B_HEADS + head, _aug_base(head) + AUG_PARTS + part] = -1.0
    return jnp.asarray(p, BF16)


def _inproj_kernel(x_ref, cos_ref, sin_ref, ada_ref, g_ref, w_ref, wvt_ref, bf_ref, place_ref, hsum_ref,
                   qa_ref, ka_ref, vat_ref, qb_ref, kb_ref, vbt_ref, stat_ref, carry_ref):
    i = pl.program_id(1)
    sub = vbt_ref.shape[3]
    lane = lax.broadcasted_iota(jnp.int32, (sub, LANES), 1)
    row = lax.broadcasted_iota(jnp.int32, (sub, LANES), 0)
    first_half = (lane % HEAD_DIM) < (HEAD_DIM // 2)
    low = lane < HEAD_DIM
    q_scale = LOG2E / math.sqrt(HEAD_DIM)

    @pl.when(i == 0)
    def _():
        carry_ref[...] = jnp.zeros(carry_ref.shape, F32)

    carry = carry_ref[0:1, :]
    pending = []
    for r in range(x_ref.shape[1] // sub):
        rows = slice(r * sub, (r + 1) * sub)
        h = _rms_mod(x_ref[0, rows, :], g_ref[...], ada_ref[0, 1:2, :], ada_ref[0, 0:1, :]).astype(BF16)

        def proj(lo, width):
            return lax.dot_general(h, w_ref[lo:lo + width, :], (((1,), (1,)), ((), ())),
                                   preferred_element_type=F32)

        def proj_t(lo, width):
            return lax.dot_general(wvt_ref[lo:lo + width, :], h, (((1,), (1,)), ((), ())),
                                   preferred_element_type=F32)

        t0 = i * x_ref.shape[1] + r * sub
        quarter = t0 // cos_ref.shape[1]
        t_rows = pl.ds(pl.multiple_of(t0 - quarter * cos_ref.shape[1], sub), sub)
        shift = (LANES - quarter * ROPE_FREQS) % LANES

        def spread(tab):
            y = jnp.where(lane < ROPE_FREQS, pltpu.roll(tab, shift, axis=1), 0.0)
            y = y + pltpu.roll(y, ROPE_FREQS, axis=1)
            return y + pltpu.roll(y, 2 * ROPE_FREQS, axis=1)

        cos = spread(cos_ref[0, t_rows, :])
        sin = spread(sin_ref[0, t_rows, :])
        sin_signed = jnp.where(first_half, -sin, sin)

        def rope(t):
            partner = jnp.where(first_half,
                                pltpu.roll(t, LANES - HEAD_DIM // 2, axis=1),
                                pltpu.roll(t, HEAD_DIM // 2, axis=1))
            return t * cos + partner * sin_signed

        kf = proj(A_Q_W + 2 * B_W, A_KV_W + LANES)
        ka_ref[0, rows, :] = rope(kf[:, :A_KV_W]).astype(BF16)
        fl = kf[:, A_KV_W:] + bf_ref[...]
        c = jnp.minimum(fl, 0.0) - jnp.log1p(jnp.exp(-jnp.abs(fl)))
        d = 1
        while d < sub:
            c = c + jnp.where(row >= d, pltpu.roll(c, d, axis=0), 0.0)
            d *= 2
        c = c + carry
        carry = c[sub - 1:sub, :]
        c = c * LOG2E
        hi = c.astype(BF16).astype(F32)
        r1 = c - hi
        mid = r1.astype(BF16).astype(F32)
        lo = r1 - mid
        packed = jnp.where(lane < B_HEADS, hi,
                           jnp.where(lane < 2 * B_HEADS, pltpu.roll(mid, B_HEADS, axis=1),
                                     jnp.where(lane < 3 * B_HEADS, pltpu.roll(lo, 2 * B_HEADS, axis=1), 0.0)))

        qa = proj(0, A_Q_W)
        for j in range(A_Q_W // LANES):
            qa_ref[0, rows, j * LANES:(j + 1) * LANES] = (
                rope(qa[:, j * LANES:(j + 1) * LANES]) * q_scale).astype(BF16)
        vat_ref[0, :, rows] = proj_t(0, A_KV_W).astype(BF16)
        vbt_ref[0, r] = proj_t(A_KV_W, B_W).astype(BF16)
        qb = proj(A_Q_W, B_W) * q_scale
        kb = proj(A_Q_W + B_W, B_W)
        pending.append((rows, r, packed, qb, kb, c))

    for rows, r, packed, qb, kb, c in pending:
        aug = jnp.dot(packed.astype(BF16), place_ref[...], preferred_element_type=F32)

        def norm_bound(t):
            sq = t * t
            fold = sq[:NORM_ROWS]
            for g in range(1, sub // NORM_ROWS):
                fold = jnp.maximum(fold, sq[g * NORM_ROWS:(g + 1) * NORM_ROWS])
            per_head = jnp.dot(fold.astype(BF16), hsum_ref[...], preferred_element_type=F32)
            return jnp.max(per_head, axis=0, keepdims=True) * NORM_MARGIN

        stats = [norm_bound(qb), norm_bound(kb)]
        stats += [c[0:1, :], c[sub - 1:sub, :], jnp.zeros((4, LANES), F32)]
        stat_ref[0, r] = jnp.concatenate(stats, axis=0)
        for head in range(B_HEADS):
            base = _aug_base(head) % LANES
            own = low if head % 2 == 0 else jnp.logical_not(low)
            plus = (lane >= base) & (lane < base + AUG_PARTS)
            minus = (lane >= base + AUG_PARTS) & (lane < base + 2 * AUG_PARTS)
            a = aug[:, head * LANES:(head + 1) * LANES]
            src = slice(head // 2 * LANES, (head // 2 + 1) * LANES)
            group = slice(head * LANES, (head + 1) * LANES)
            qb_ref[0, rows, group] = jnp.where(own, qb[:, src],
                                               jnp.where(plus, a, jnp.where(minus, 1.0, 0.0))).astype(BF16)
            kb_ref[0, rows, group] = jnp.where(own, kb[:, src],
                                               jnp.where(minus, a, jnp.where(plus, 1.0, 0.0))).astype(BF16)
    carry_ref[0:1, :] = carry


def _in_proj(x, cos, sin, ada, g_pre, w_qk, w_vt, b_f, tm, sub):
    bsz, s, d = x.shape
    grid = (bsz, s // tm)
    tok = lambda w: pl.BlockSpec((1, tm, w), lambda b, i: (b, i, 0))
    const = lambda shape: pl.BlockSpec(shape, lambda b, i: (0,) * len(shape))
    act = lambda w: jax.ShapeDtypeStruct((bsz, s, w), BF16)
    place = _aug_placement()
    head_sum = jnp.asarray(np.kron(np.eye(B_HEADS, LANES), np.ones((HEAD_DIM, 1))), BF16)
    return pl.pallas_call(
        _inproj_kernel,
        grid=grid,
        in_specs=[tok(d), pl.BlockSpec((1,) + cos.shape[1:], lambda b, i: (b, 0, 0)),
                  pl.BlockSpec((1,) + sin.shape[1:], lambda b, i: (b, 0, 0)),
                  pl.BlockSpec((1, N_ADA, d), lambda b, i: (b, 0, 0)),
                  const((1, d)), _resident(w_qk.shape), _resident(w_vt.shape),
                  const((1, LANES)), const(place.shape), const(head_sum.shape)],
        out_specs=[tok(A_Q_W), tok(A_KV_W),
                   pl.BlockSpec((1, A_KV_W, tm), lambda b, i: (b, 0, i)),
                   tok(B_HEADS * LANES), tok(B_HEADS * LANES),
                   pl.BlockSpec((1, tm // sub, B_W, sub), lambda b, i: (b, i, 0, 0)),
                   pl.BlockSpec((1, tm // sub, 8, LANES), lambda b, i: (b, i, 0, 0))],
        out_shape=[act(A_Q_W), act(A_KV_W),
                   jax.ShapeDtypeStruct((bsz, A_KV_W, s), BF16),
                   act(B_HEADS * LANES), act(B_HEADS * LANES),
                   jax.ShapeDtypeStruct((bsz, s // sub, B_W, sub), BF16),
                   jax.ShapeDtypeStruct((bsz, s // sub, 8, LANES), F32)],
        scratch_shapes=[pltpu.VMEM((8, LANES), F32)],
        compiler_params=_params("parallel", "arbitrary"),
        name="in_proj",
    )(x, cos, sin, ada, g_pre, w_qk, w_vt, b_f, place, head_sum)


def _plan_kernel(st_ref, o_ref, *, ratio):
    nt = st_ref.shape[1]
    q_scale = LOG2E / math.sqrt(HEAD_DIM)
    for b in range(st_ref.shape[0]):
        qn = st_ref[b, 0, 0:1, :]
        kn = st_ref[b, 0, 1:2, :]
        for j in range(1, nt):
            qn = jnp.maximum(qn, st_ref[b, j, 0:1, :])
            kn = jnp.maximum(kn, st_ref[b, j, 1:2, :])
        spread = qn * (1.0 / q_scale) + kn * q_scale
        rows = []
        for blk in range(nt // ratio):
            limit = st_ref[b, blk * ratio, 2:3, :] + spread + SKIP_LOG2
            count = jnp.zeros((1, LANES), jnp.int32)
            prefix = jnp.ones((1, LANES), jnp.int32)
            for j in range(blk * ratio):
                prefix = prefix * (st_ref[b, j, 3:4, :] > limit).astype(jnp.int32)
                count = count + prefix
            rows.append(count)
        o_ref[b] = jnp.concatenate(rows, axis=0)


def _fox_plan(stats, ratio):
    bsz, nt, _, _ = stats.shape
    return pl.pallas_call(
        functools.partial(_plan_kernel, ratio=ratio),
        grid=(1,),
        in_specs=[pl.BlockSpec((bsz, nt, 8, LANES), lambda i: (0, 0, 0, 0))],
        out_specs=pl.BlockSpec((bsz, nt // ratio, LANES), lambda i: (0, 0, 0)),
        out_shape=jax.ShapeDtypeStruct((bsz, nt // ratio, LANES), jnp.int32),
        compiler_params=_params("arbitrary"),
        name="fox_plan",
    )(stats)


def _swa_kernel(q_ref, kp_ref, kc_ref, vtp_ref, vtc_ref, sink_ref, o_ref):
    i = pl.program_id(1)
    tq = q_ref.shape[1]
    nsub = tq // WINDOW
    ncol = A_Q_W // LANES
    lane_q = lax.broadcasted_iota(jnp.int32, (WINDOW, LANES), 1)
    low = lane_q < HEAD_DIM
    low_dim = lax.broadcasted_iota(jnp.int32, (LANES, WINDOW), 0) < HEAD_DIM
    k_loc = lax.broadcasted_iota(jnp.int32, (2 * WINDOW, WINDOW), 0)
    q_loc = lax.broadcasted_iota(jnp.int32, (2 * WINDOW, WINDOW), 1) + WINDOW
    rel = q_loc - k_loc
    band = (rel >= 0) & (rel < WINDOW)
    sink = sink_ref[...] * LOG2E
    ones = jnp.ones((ONES_ROWS, 2 * WINDOW), BF16)
    zero = jnp.zeros((WINDOW, LANES), BF16)

    def keys(j):
        if j == 0:
            return jnp.concatenate([kp_ref[0], kc_ref[0, :WINDOW, :]], axis=0)
        return kc_ref[0, (j - 1) * WINDOW:(j + 1) * WINDOW, :]

    def values_t(j):
        if j == 0:
            return jnp.concatenate([vtp_ref[0], vtc_ref[0, :, :WINDOW]], axis=1)
        return vtc_ref[0, :, (j - 1) * WINDOW:(j + 1) * WINDOW]

    def scores(j):
        q = q_ref[0, j * WINDOW:(j + 1) * WINDOW, :]
        parts = [jnp.where(low, q[:, c * LANES:(c + 1) * LANES], zero) for c in range(ncol)]
        parts += [jnp.where(low, zero, q[:, c * LANES:(c + 1) * LANES]) for c in range(ncol)]
        qs = jnp.concatenate(parts, axis=0)
        return lax.dot_general(keys(j), qs, (((1,), (1,)), ((), ())), preferred_element_type=F32)

    def softmax_pv(j, st):
        valid = band & ((k_loc >= WINDOW) | (i > 0)) if j == 0 else band
        ps, sink_terms = [], []
        for hh in range(A_Q_HEADS):
            cols = slice(hh * WINDOW, (hh + 1) * WINDOW)
            w = jnp.where(valid, st[:, cols], NEG_INF)
            m = jnp.maximum(jnp.max(w, axis=0, keepdims=True), sink[:, cols])
            ps.append(jnp.exp2(w - m).astype(BF16))
            sink_terms.append(jnp.exp2(sink[:, cols] - m))
        pt = jnp.concatenate(ps, axis=1)
        ot = jnp.dot(jnp.concatenate([values_t(j), ones], axis=0), pt, preferred_element_type=F32)
        for c in range(ncol):
            halves = []
            for hh in (c, ncol + c):
                cols = slice(hh * WINDOW, (hh + 1) * WINDOW)
                denom = ot[LANES:LANES + 1, cols] + sink_terms[hh]
                halves.append(ot[:LANES, cols] * (1.0 / denom))
            o_ref[0, j * WINDOW:(j + 1) * WINDOW, c * LANES:(c + 1) * LANES] = (
                jnp.transpose(jnp.where(low_dim, halves[0], halves[1])).astype(BF16))

    st = scores(0)
    for j in range(nsub):
        st_next = scores(j + 1) if j + 1 < nsub else None
        softmax_pv(j, st)
        st = st_next


def _swa(qa, ka, vat, sinks, tq):
    bsz, s, _ = qa.shape
    r = tq // WINDOW
    cur = lambda b, i: (b, i, 0)
    return pl.pallas_call(
        _swa_kernel,
        grid=(bsz, s // tq),
        in_specs=[pl.BlockSpec((1, tq, A_Q_W), cur),
                  pl.BlockSpec((1, WINDOW, A_KV_W), lambda b, i: (b, jnp.maximum(i * r - 1, 0), 0)),
                  pl.BlockSpec((1, tq, A_KV_W), cur),
                  pl.BlockSpec((1, A_KV_W, WINDOW), lambda b, i: (b, 0, jnp.maximum(i * r - 1, 0))),
                  pl.BlockSpec((1, A_KV_W, tq), lambda b, i: (b, 0, i)),
                  pl.BlockSpec((1, A_Q_HEADS * WINDOW), lambda b, i: (0, 0))],
        out_specs=pl.BlockSpec((1, tq, A_Q_W), cur),
        out_shape=jax.ShapeDtypeStruct((bsz, s, A_Q_W), BF16),
        compiler_params=_params("parallel", "parallel"),
        name="swa",
    )(qa, ka, ka, vat, vat, sinks)


def _fox_kernel(skip_ref, q_ref, k_ref, vt_ref, o_ref, *scratch, tq):
    nq = q_ref.shape[1] // tq

    def tile(qi, carry):
        _fox_tile(qi, tq, nq, skip_ref, q_ref, k_ref, vt_ref, o_ref, *scratch)
        return carry

    lax.fori_loop(0, nq, tile, 0)


def _fox_tile(qi, tq, nq, skip_ref, q_ref, k_ref, vt_ref, o_ref, m_ref, acc_ref, sta_ref, stb_ref):
    tk = vt_ref.shape[3]
    nchunk = tq // LANES
    q_row0 = pl.multiple_of(qi * tq, tq)
    assert tq == 2 * tk
    m_ref[...] = jnp.full(m_ref.shape, NEG_INF, F32)
    acc_ref[...] = jnp.zeros(acc_ref.shape, F32)
    ones = jnp.ones((ONES_ROWS, tk), BF16)

    def scores(kj, st_ref, c0=0, c1=nchunk):
        start = pl.multiple_of(kj * tk, tk)
        for e in range(2):
            group = slice(e * LANES, (e + 1) * LANES)
            st_ref[e, :, c0 * LANES:c1 * LANES] = lax.dot_general(
                k_ref[0, pl.ds(start, tk), group],
                q_ref[0, pl.ds(pl.multiple_of(q_row0 + c0 * LANES, LANES), (c1 - c0) * LANES), group],
                (((1,), (1,)), ((), ())), preferred_element_type=F32)

    def softmax_pv(kj, st_ref, first_query=None, c0=0, c1=nchunk):
        for e in range(2):
            vta = jnp.concatenate([vt_ref[0, kj, e * HEAD_DIM:(e + 1) * HEAD_DIM, :], ones], axis=0)
            ps, alphas = [], []
            for c in range(c0, c1):
                cols = slice(c * LANES, (c + 1) * LANES)
                off = None if first_query is None else first_query + c * LANES
                masked = off is not None and off < tk
                live = min(off + LANES, tk) if masked else tk
                z = st_ref[e, :live, cols]
                if masked:
                    causal = (lax.broadcasted_iota(jnp.int32, (live, LANES), 0)
                              <= lax.broadcasted_iota(jnp.int32, (live, LANES), 1) + off)
                    z = jnp.where(causal, z, NEG_INF)
                m_old = m_ref[e, :, cols]
                m_new = jnp.maximum(m_old, jnp.max(z, axis=0, keepdims=True))
                p = jnp.exp2(z - m_new).astype(BF16)
                if live < tk:
                    p = jnp.concatenate([p, jnp.zeros((tk - live, LANES), BF16)], axis=0)
                ps.append(p)
                alphas.append(jnp.exp2(m_old - m_new))
                m_ref[e, :, cols] = m_new
            pt = jnp.concatenate(ps, axis=1)
            alpha = jnp.concatenate(alphas, axis=1)
            span = slice(c0 * LANES, c1 * LANES)
            acc_ref[e, :, span] = alpha * acc_ref[e, :, span] + jnp.dot(vta, pt, preferred_element_type=F32)

    half = tk // LANES
    base = ((pl.program_id(0) * nq + qi) * 2 * B_HEADS + 2 * pl.program_id(1))
    lead = jnp.minimum(skip_ref[base], skip_ref[base + 1])
    first = jnp.minimum(jnp.minimum(skip_ref[base + B_HEADS], skip_ref[base + B_HEADS + 1]), 2 * qi)
    first = jnp.maximum(first, lead)

    second = half
    short = (qi >= 1) & (lead == 2 * qi - 1) & (first == 2 * qi)

    @pl.when(short)
    def _():
        scores(2 * qi - 1, stb_ref, c1=half)
        scores(2 * qi, sta_ref)
        scores(2 * qi + 1, stb_ref, c0=second)
        softmax_pv(2 * qi - 1, stb_ref, c1=half)
        softmax_pv(2 * qi, sta_ref, first_query=0)
        softmax_pv(2 * qi + 1, stb_ref, first_query=-tk, c0=second)

    @pl.when(jnp.logical_not(short))
    def _():
        def early_half(u, carry):
            scores(u, stb_ref, c1=half)
            softmax_pv(u, stb_ref, c1=half)
            return carry

        lax.fori_loop(lead, first, early_half, 0)
        first_pair = (first + 1) // 2
        odd_start = first % 2 == 1

        @pl.when(odd_start)
        def _():
            scores(first, stb_ref)
            scores(first + 1, sta_ref)
            softmax_pv(first, stb_ref)

        @pl.when(jnp.logical_not(odd_start))
        def _():
            scores(first, sta_ref)

        def pair(t, carry):
            u = 2 * t
            scores(u + 1, stb_ref)
            softmax_pv(u, sta_ref)
            scores(u + 2, sta_ref)
            softmax_pv(u + 1, stb_ref)
            return carry

        lax.fori_loop(first_pair, qi, pair, 0)
        scores(2 * qi + 1, stb_ref, c0=second)
        softmax_pv(2 * qi, sta_ref, first_query=0)
        softmax_pv(2 * qi + 1, stb_ref, first_query=-tk, c0=second)

    out_t = jnp.concatenate(
        [acc_ref[e, :HEAD_DIM, :] * (1.0 / acc_ref[e, HEAD_DIM:HEAD_DIM + 1, :]) for e in range(2)], axis=0)
    o_ref[0, pl.ds(q_row0, tq), :] = jnp.transpose(out_t).astype(BF16)


def _fox(skip, qb, kb, vbt, tq):
    bsz, s, _ = qb.shape
    _, nk, _, tk = vbt.shape
    npair = B_HEADS // 2
    whole = pl.BlockSpec((1, s, 2 * LANES), lambda b, hp, skip: (b, 0, hp))
    return pl.pallas_call(
        functools.partial(_fox_kernel, tq=tq),
        grid_spec=pltpu.PrefetchScalarGridSpec(
            num_scalar_prefetch=1,
            grid=(bsz, npair),
            in_specs=[whole, whole,
                      pl.BlockSpec((1, nk, 2 * HEAD_DIM, tk), lambda b, hp, skip: (b, 0, hp, 0))],
            out_specs=pl.BlockSpec((1, s, 2 * HEAD_DIM), lambda b, hp, skip: (b, 0, hp)),
            scratch_shapes=[pltpu.VMEM((2, 1, tq), F32),
                            pltpu.VMEM((2, HEAD_DIM + ONES_ROWS, tq), F32),
                            pltpu.VMEM((2, tk, tq), F32),
                            pltpu.VMEM((2, tk, tq), F32)]),
        out_shape=jax.ShapeDtypeStruct((bsz, s, B_W), BF16),
        compiler_params=_params("parallel", "arbitrary"),
        name="fox",
    )(skip, qb, kb, vbt)


def _resident(shape):
    return pl.BlockSpec(shape, lambda b, i: (0,) * len(shape), pipeline_mode=pl.Buffered(1))


def _post_kernel(x_ref, oa_ref, ob_ref, ada_ref, gpre_ref, gpost_ref, wg_ref, wa_ref, wb_ref,
                 wo_ref, wfi_ref, wfo_ref, o_ref, wfi_out_ref, wfo_out_ref, *, sub):
    d = x_ref.shape[2]
    wfi_out_ref[...] = wfi_ref[...].astype(BF16)
    wfo_out_ref[...] = wfo_ref[...].astype(BF16)
    nt = (((1,), (1,)), ((), ()))
    merged = []
    for r in range(x_ref.shape[1] // sub):
        rows = slice(r * sub, (r + 1) * sub)
        h = _rms_mod(x_ref[0, rows, :], gpre_ref[...], ada_ref[0, 1:2, :], ada_ref[0, 0:1, :]).astype(BF16)
        a = jnp.dot(oa_ref[0, rows, :], wa_ref[...], preferred_element_type=F32)
        b = jnp.dot(ob_ref[0, rows, :], wb_ref[...], preferred_element_type=F32)
        ga = jax.nn.sigmoid(lax.dot_general(h, wg_ref[:d, :], nt, preferred_element_type=F32))
        gb = jax.nn.sigmoid(lax.dot_general(h, wg_ref[d:, :], nt, preferred_element_type=F32))
        merged.append((ga * a + gb * b).astype(BF16))
    for r in range(x_ref.shape[1] // sub):
        rows = slice(r * sub, (r + 1) * sub)
        y = jnp.dot(merged[r], wo_ref[...], preferred_element_type=F32)
        o_ref[0, rows, :] = x_ref[0, rows, :] + ada_ref[0, 2:3, :] * _rms(y, gpost_ref[...])


def _post(x, oa, ob, ada, g_pre, g_post, wg, wa, wb, wo, w_ffn_in, w_ffn_out, tm, sub):
    bsz, s, d = x.shape
    nt = s // tm
    steps = bsz * nt
    n_in = steps
    n_out = math.gcd(steps, w_ffn_out.shape[0] // BF16_ROWS)
    tok = lambda w: pl.BlockSpec((1, tm, w), lambda b, i: (b, i, 0))
    const = lambda shape: pl.BlockSpec(shape, lambda b, i: (0,) * len(shape))
    slab_in = pl.BlockSpec((d // n_in, w_ffn_in.shape[1]), lambda b, i: (b * nt + i, 0))
    slab_out = pl.BlockSpec((w_ffn_out.shape[0] // n_out, d), lambda b, i: ((b * nt + i) * n_out // steps, 0))
    return pl.pallas_call(
        functools.partial(_post_kernel, sub=sub),
        grid=(bsz, nt),
        in_specs=[tok(d), tok(A_Q_W), tok(B_W),
                  pl.BlockSpec((1, N_ADA, d), lambda b, i: (b, 0, 0)),
                  const((1, d)), const((1, d)),
                  _resident(wg.shape), _resident(wa.shape), _resident(wb.shape), _resident(wo.shape),
                  slab_in, slab_out],
        out_specs=[tok(d), slab_in, slab_out],
        out_shape=[jax.ShapeDtypeStruct((bsz, s, d), F32),
                   jax.ShapeDtypeStruct(w_ffn_in.shape, BF16),
                   jax.ShapeDtypeStruct(w_ffn_out.shape, BF16)],
        compiler_params=_params("arbitrary", "arbitrary"),
        name="post",
    )(x, oa, ob, ada, g_pre, g_post, wg, wa, wb, wo, w_ffn_in, w_ffn_out)


def _ffn_kernel(x_ref, ada_ref, gpre_ref, gpost_ref, wi_ref, wo_ref, o_ref, *, sub, chunks):
    d_ff = wo_ref.shape[0]
    for r in range(x_ref.shape[1] // sub):
        rows = slice(r * sub, (r + 1) * sub)
        x = x_ref[0, rows, :]
        h = _rms_mod(x, gpre_ref[...], ada_ref[0, 4:5, :], ada_ref[0, 3:4, :]).astype(BF16)
        acts = []
        lo = 0
        for width in chunks:
            g = jnp.dot(h, wi_ref[:, lo:lo + width], preferred_element_type=F32)
            u = jnp.dot(h, wi_ref[:, d_ff + lo:d_ff + lo + width], preferred_element_type=F32)
            acts.append((g * jax.nn.sigmoid(g) * u).astype(BF16))
            lo += width
        y = jnp.zeros(x.shape, F32)
        lo = 0
        for width, act in zip(chunks, acts):
            y = y + jnp.dot(act, wo_ref[lo:lo + width, :], preferred_element_type=F32)
            lo += width
        o_ref[0, rows, :] = x + ada_ref[0, 5:6, :] * _rms(y, gpost_ref[...])


def _ffn(x, ada, g_pre, g_post, wi, wo, tm, sub, chunks):
    bsz, s, d = x.shape
    tok = pl.BlockSpec((1, tm, d), lambda b, i: (b, i, 0))
    const = lambda shape: pl.BlockSpec(shape, lambda b, i: (0,) * len(shape))
    return pl.pallas_call(
        functools.partial(_ffn_kernel, sub=sub, chunks=chunks),
        grid=(bsz, s // tm),
        in_specs=[tok, pl.BlockSpec((1, N_ADA, d), lambda b, i: (b, 0, 0)),
                  const((1, d)), const((1, d)), _resident(wi.shape), _resident(wo.shape)],
        out_specs=tok,
        out_shape=jax.ShapeDtypeStruct((bsz, s, d), F32),
        compiler_params=_params("parallel", "parallel"),
        name="ffn",
    )(x, ada, g_pre, g_post, wi, wo)


def _pair_heads(w, axis):
    shape = w.shape
    pre, post = shape[:axis], shape[axis + 1:]
    w = w.reshape(pre + (A_KV_HEADS, A_GROUP, HEAD_DIM) + post)
    w = jnp.swapaxes(w, axis, axis + 1)
    return w.reshape(shape)


def kernel(x, c, positions, w_ada, b_ada, g_pre_mix, g_post_mix, w_in, b_f, sinks, w_branch_a,
           w_branch_b, w_out, g_pre_ffn, g_post_ffn, w_ffn_in, w_ffn_out):
    bsz, s, d = x.shape
    depth = w_ada.shape[0]
    tm = min(1024, s)
    sub = min(512, s)
    rep = LANES // ROPE_FREQS
    assert (s // rep) % sub == 0
    half = jnp.arange(0, HEAD_DIM, 2, dtype=F32) / HEAD_DIM
    inv_freq = jnp.tile(1.0 / (ROPE_THETA ** half), rep).reshape(1, LANES)
    pos_c = jnp.swapaxes(positions.astype(F32).reshape(bsz, rep, s // rep), 1, 2)
    pos_c = jnp.repeat(pos_c, ROPE_FREQS, axis=2)
    fine = sub // 2
    for l in range(depth):
        w_t = jnp.swapaxes(w_in[l], 0, 1)
        o_va, o_qb, o_vb = A_Q_W + A_KV_W, A_Q_W + 2 * A_KV_W, A_Q_W + 2 * A_KV_W + 2 * B_W
        w_f = jnp.pad(w_t[QKV_W:QKV_W + B_HEADS], ((0, LANES - B_HEADS), (0, 0)))
        w_qk = jnp.concatenate([_pair_heads(w_t[:A_Q_W], 0), w_t[o_qb:o_vb], w_t[A_Q_W:o_va], w_f],
                               axis=0).astype(BF16)
        w_vt = jnp.concatenate([w_t[o_va:o_qb], w_t[o_vb:QKV_W]], axis=0).astype(BF16)
        w_g = w_t[QKV_W + B_HEADS:].astype(BF16)
        bf_row = jnp.pad(b_f[l], (0, LANES - B_HEADS)).reshape(1, LANES)
        sink_row = jnp.repeat(sinks[l], WINDOW).reshape(1, A_Q_HEADS * WINDOW)
        w_a = _pair_heads(w_branch_a[l], 0).astype(BF16)
        w_b = w_branch_b[l].astype(BF16)
        w_o = w_out[l].astype(BF16)
        row = lambda g: g[l].reshape(1, d)

        if l == 0:
            ada, cos, sin = _ada_rope(c, w_ada[l], b_ada[l], pos_c, inv_freq)
        else:
            ada = _ada(c, w_ada[l], b_ada[l])
        ada = ada.reshape(bsz, N_ADA, d)
        qa, ka, vat, qb, kb, vbt, stats = _in_proj(x, cos, sin, ada, row(g_pre_mix), w_qk, w_vt, bf_row,
                                                   tm, sub)
        o_a = _swa(qa, ka, vat, sink_row, min(2 * tm, s))
        skip = _fox_plan(stats, 1)[:, :, :B_HEADS].reshape(-1)
        o_b = _fox(skip, qb, kb, vbt, 2 * sub)
        x, w_fi, w_fo = _post(x, o_a, o_b, ada, row(g_pre_mix), row(g_post_mix), w_g, w_a, w_b, w_o,
                              w_ffn_in[l], w_ffn_out[l], tm, fine)
        n_tiles = w_fo.shape[0] // MXU_DEPTH
        chunks = ((n_tiles + 1) // 2 * MXU_DEPTH, n_tiles // 2 * MXU_DEPTH)
        x = _ffn(x, ada, row(g_pre_ffn), row(g_post_ffn), w_fi, w_fo, tm, fine, chunks)
    return x
```

```python
import functools
import math

import jax
import jax.numpy as jnp
import numpy as np
from jax import lax
from jax.experimental import pallas as pl
from jax.experimental.pallas import tpu as pltpu

HEAD_DIM = 64
WINDOW = 128
A_Q_HEADS = 8
A_KV_HEADS = 2
A_GROUP = A_Q_HEADS // A_KV_HEADS
B_HEADS = 8
N_ADA = 6
ROPE_THETA = 10000.0
RMS_EPS = 1e-6
LANES = 128
MXU_DEPTH = 256
ROPE_FREQS = HEAD_DIM // 2
BF16_ROWS = 16
VMEM_LIMIT = 56 * 1024 * 1024

A_Q_W = A_Q_HEADS * HEAD_DIM
A_KV_W = A_KV_HEADS * HEAD_DIM
B_W = B_HEADS * HEAD_DIM
QKV_W = A_Q_W + 2 * A_KV_W + 3 * B_W
BF16 = jnp.bfloat16
F32 = jnp.float32
NEG_INF = float("-inf")
LOG2E = math.log2(math.e)
ONES_ROWS = 16
SKIP_LOG2 = 48.0
NORM_MARGIN = 1.02
NORM_ROWS = 256


def _params(*sem, flags=None):
    return pltpu.CompilerParams(dimension_semantics=sem, vmem_limit_bytes=VMEM_LIMIT, flags=flags)


def _rms_mod(x, g, scale, shift):
    ms = jnp.mean(x * x, axis=-1, keepdims=True)
    return (x * lax.rsqrt(ms + RMS_EPS)) * (g * (1.0 + scale)) + shift


def _rms(y, g):
    ms = jnp.mean(y * y, axis=-1, keepdims=True)
    return y * lax.rsqrt(ms + RMS_EPS) * g


def _ada_kernel(c_ref, w_ref, b_ref, o_ref):
    o_ref[...] = jnp.dot(c_ref[...], w_ref[...], preferred_element_type=F32) + b_ref[...]


def _ada(c, w, b):
    bsz, d = c.shape
    n = w.shape[1]
    tn = n // 4
    return pl.pallas_call(
        _ada_kernel,
        grid=(n // tn,),
        in_specs=[pl.BlockSpec((bsz, d), lambda j: (0, 0)),
                  pl.BlockSpec((d, tn), lambda j: (0, j)),
                  pl.BlockSpec((1, tn), lambda j: (0, j))],
        out_specs=pl.BlockSpec((bsz, tn), lambda j: (0, j)),
        out_shape=jax.ShapeDtypeStruct((bsz, n), F32),
        compiler_params=_params("arbitrary"),
        name="ada",
    )(c, w, b.reshape(1, n))


def _ada_rope_kernel(c_ref, w_ref, b_ref, pos_ref, invf_ref, ada_ref, cos_ref, sin_ref):
    ada_ref[...] = jnp.dot(c_ref[...], w_ref[...], preferred_element_type=F32) + b_ref[...]
    ang = pos_ref[0] * invf_ref[...]
    cos_ref[0] = jnp.cos(ang)
    sin_ref[0] = jnp.sin(ang)


def _ada_rope(c, w, b, pos_c, inv_freq):
    bsz, d = c.shape
    n = w.shape[1]
    tn = n // bsz
    assert tn % LANES == 0
    rows = pos_c.shape[1]
    tab = pl.BlockSpec((1, rows, LANES), lambda j: (j, 0, 0))
    tab_shape = jax.ShapeDtypeStruct((bsz, rows, LANES), F32)
    return pl.pallas_call(
        _ada_rope_kernel,
        grid=(bsz,),
        in_specs=[pl.BlockSpec((bsz, d), lambda j: (0, 0)),
                  pl.BlockSpec((d, tn), lambda j: (0, j)),
                  pl.BlockSpec((1, tn), lambda j: (0, j)),
                  tab, pl.BlockSpec((1, LANES), lambda j: (0, 0))],
        out_specs=[pl.BlockSpec((bsz, tn), lambda j: (0, j)), tab, tab],
        out_shape=[jax.ShapeDtypeStruct((bsz, n), F32), tab_shape, tab_shape],
        compiler_params=_params("arbitrary"),
        name="ada_rope",
    )(c, w, b.reshape(1, n), pos_c, inv_freq)


AUG_PARTS = 3


def _aug_base(head):
    return head * LANES + (HEAD_DIM if head % 2 == 0 else 0)


def _aug_placement():
    p = np.zeros((LANES, B_HEADS * LANES), np.float32)
    for head in range(B_HEADS):
        for part in range(AUG_PARTS):
            p[part * B_HEADS + head, _aug_base(head) + part] = 1.0
            p[part * B_HEADS + head, _aug_base(head) + AUG_PARTS + part] = -1.0
    return jnp.asarray(p, BF16)


def _inproj_kernel(x_ref, cos_ref, sin_ref, ada_ref, g_ref, w_ref, wvt_ref, bf_ref, place_ref, hsum_ref,
                   qa_ref, ka_ref, vat_ref, qb_ref, kb_ref, vbt_ref, stat_ref, carry_ref):
    i = pl.program_id(1)
    sub = vbt_ref.shape[3]
    lane = lax.broadcasted_iota(jnp.int32, (sub, LANES), 1)
    row = lax.broadcasted_iota(jnp.int32, (sub, LANES), 0)
    first_half = (lane % HEAD_DIM) < (HEAD_DIM // 2)
    low = lane < HEAD_DIM
    q_scale = LOG2E / math.sqrt(HEAD_DIM)

    @pl.when(i == 0)
    def _():
        carry_ref[...] = jnp.zeros(carry_ref.shape, F32)

    carry = carry_ref[0:1, :]
    pending = []
    for r in range(x_ref.shape[1] // sub):
        rows = slice(r * sub, (r + 1) * sub)
        h = _rms_mod(x_ref[0, rows, :], g_ref[...], ada_ref[0, 1:2, :], ada_ref[0, 0:1, :]).astype(BF16)

        def proj(lo, width):
            return lax.dot_general(h, w_ref[lo:lo + width, :], (((1,), (1,)), ((), ())),
                                   preferred_element_type=F32)

        def proj_t(lo, width):
            return lax.dot_general(wvt_ref[lo:lo + width, :], h, (((1,), (1,)), ((), ())),
                                   preferred_element_type=F32)

        t0 = i * x_ref.shape[1] + r * sub
        quarter = t0 // cos_ref.shape[1]
        t_rows = pl.ds(pl.multiple_of(t0 - quarter * cos_ref.shape[1], sub), sub)
        shift = (LANES - quarter * ROPE_FREQS) % LANES

        def spread(tab):
            y = jnp.where(lane < ROPE_FREQS, pltpu.roll(tab, shift, axis=1), 0.0)
            y = y + pltpu.roll(y, ROPE_FREQS, axis=1)
            return y + pltpu.roll(y, 2 * ROPE_FREQS, axis=1)

        cos = spread(cos_ref[0, t_rows, :])
        sin = spread(sin_ref[0, t_rows, :])
        sin_signed = jnp.where(first_half, -sin, sin)

        def rope(t):
            partner = jnp.where(first_half,
                                pltpu.roll(t, LANES - HEAD_DIM // 2, axis=1),
                                pltpu.roll(t, HEAD_DIM // 2, axis=1))
            return t * cos + partner * sin_signed

        kf = proj(A_Q_W + 2 * B_W, A_KV_W + LANES)
        ka_ref[0, rows, :] = rope(kf[:, :A_KV_W]).astype(BF16)
        fl = kf[:, A_KV_W:] + bf_ref[...]
        c = jnp.minimum(fl, 0.0) - jnp.log1p(jnp.exp(-jnp.abs(fl)))
        d = 1
        while d < sub:
            c = c + jnp.where(row >= d, pltpu.roll(c, d, axis=0), 0.0)
            d *= 2
        c = c + carry
        carry = c[sub - 1:sub, :]
        c = c * LOG2E
        hi = c.astype(BF16).astype(F32)
        r1 = c - hi
        mid = r1.astype(BF16).astype(F32)
        lo = r1 - mid
        packed = jnp.where(lane < B_HEADS, hi,
                           jnp.where(lane < 2 * B_HEADS, pltpu.roll(mid, B_HEADS, axis=1),
                                     jnp.where(lane < 3 * B_HEADS, pltpu.roll(lo, 2 * B_HEADS, axis=1), 0.0)))

        qa = proj(0, A_Q_W)
        for j in range(A_Q_W // LANES):
            qa_ref[0, rows, j * LANES:(j + 1) * LANES] = (
                rope(qa[:, j * LANES:(j + 1) * LANES]) * q_scale).astype(BF16)
        vat_ref[0, :, rows] = proj_t(0, A_KV_W).astype(BF16)
        vbt_ref[0, r] = proj_t(A_KV_W, B_W).astype(BF16)
        qb = proj(A_Q_W, B_W) * q_scale
        kb = proj(A_Q_W + B_W, B_W)
        pending.append((rows, r, packed, qb, kb, c))

    for rows, r, packed, qb, kb, c in pending:
        aug = jnp.dot(packed.astype(BF16), place_ref[...], preferred_element_type=F32)

        def norm_bound(t):
            sq = t * t
            fold = sq[:NORM_ROWS]
            for g in range(1, sub // NORM_ROWS):
                fold = jnp.maximum(fold, sq[g * NORM_ROWS:(g + 1) * NORM_ROWS])
            per_head = jnp.dot(fold.astype(BF16), hsum_ref[...], preferred_element_type=F32)
            return jnp.max(per_head, axis=0, keepdims=True) * NORM_MARGIN

        stats = [norm_bound(qb), norm_bound(kb)]
        stats += [c[0:1, :], c[sub - 1:sub, :], jnp.zeros((4, LANES), F32)]
        stat_ref[0, r] = jnp.concatenate(stats, axis=0)
        for head in range(B_HEADS):
            base = _aug_base(head) % LANES
            own = low if head % 2 == 0 else jnp.logical_not(low)
            plus = (lane >= base) & (lane < base + AUG_PARTS)
            minus = (lane >= base + AUG_PARTS) & (lane < base + 2 * AUG_PARTS)
            a = aug[:, head * LANES:(head + 1) * LANES]
            src = slice(head // 2 * LANES, (head // 2 + 1) * LANES)
            group = slice(head * LANES, (head + 1) * LANES)
            qb_ref[0, rows, group] = jnp.where(own, qb[:, src],
                                               jnp.where(plus, a, jnp.where(minus, 1.0, 0.0))).astype(BF16)
            kb_ref[0, rows, group] = jnp.where(own, kb[:, src],
                                               jnp.where(minus, a, jnp.where(plus, 1.0, 0.0))).astype(BF16)
    carry_ref[0:1, :] = carry


def _in_proj(x, cos, sin, ada, g_pre, w_qk, w_vt, b_f, tm, sub):
    bsz, s, d = x.shape
    grid = (bsz, s // tm)
    tok = lambda w: pl.BlockSpec((1, tm, w), lambda b, i: (b, i, 0))
    const = lambda shape: pl.BlockSpec(shape, lambda b, i: (0,) * len(shape))
    act = lambda w: jax.ShapeDtypeStruct((bsz, s, w), BF16)
    place = _aug_placement()
    head_sum = jnp.asarray(np.kron(np.eye(B_HEADS, LANES), np.ones((HEAD_DIM, 1))), BF16)
    return pl.pallas_call(
        _inproj_kernel,
        grid=grid,
        in_specs=[tok(d), pl.BlockSpec((1,) + cos.shape[1:], lambda b, i: (b, 0, 0)),
                  pl.BlockSpec((1,) + sin.shape[1:], lambda b, i: (b, 0, 0)),
                  pl.BlockSpec((1, N_ADA, d), lambda b, i: (b, 0, 0)),
                  const((1, d)), _resident(w_qk.shape), _resident(w_vt.shape),
                  const((1, LANES)), const(place.shape), const(head_sum.shape)],
        out_specs=[tok(A_Q_W), tok(A_KV_W),
                   pl.BlockSpec((1, A_KV_W, tm), lambda b, i: (b, 0, i)),
                   tok(B_HEADS * LANES), tok(B_HEADS * LANES),
                   pl.BlockSpec((1, tm // sub, B_W, sub), lambda b, i: (b, i, 0, 0)),
                   pl.BlockSpec((1, tm // sub, 8, LANES), lambda b, i: (b, i, 0, 0))],
        out_shape=[act(A_Q_W), act(A_KV_W),
                   jax.ShapeDtypeStruct((bsz, A_KV_W, s), BF16),
                   act(B_HEADS * LANES), act(B_HEADS * LANES),
                   jax.ShapeDtypeStruct((bsz, s // sub, B_W, sub), BF16),
                   jax.ShapeDtypeStruct((bsz, s // sub, 8, LANES), F32)],
        scratch_shapes=[pltpu.VMEM((8, LANES), F32)],
        compiler_params=_params("parallel", "arbitrary"),
        name="in_proj",
    )(x, cos, sin, ada, g_pre, w_qk, w_vt, b_f, place, head_sum)


def _plan_kernel(st_ref, o_ref, *, ratio):
    nt = st_ref.shape[1]
    q_scale = LOG2E / math.sqrt(HEAD_DIM)
    for b in range(st_ref.shape[0]):
        qn = st_ref[b, 0, 0:1, :]
        kn = st_ref[b, 0, 1:2, :]
        for j in range(1, nt):
            qn = jnp.maximum(qn, st_ref[b, j, 0:1, :])
            kn = jnp.maximum(kn, st_ref[b, j, 1:2, :])
        spread = qn * (1.0 / q_scale) + kn * q_scale
        rows = []
        for blk in range(nt // ratio):
            limit = st_ref[b, blk * ratio, 2:3, :] + spread + SKIP_LOG2
            count = jnp.zeros((1, LANES), jnp.int32)
            prefix = jnp.ones((1, LANES), jnp.int32)
            for j in range(blk * ratio):
                prefix = prefix * (st_ref[b, j, 3:4, :] > limit).astype(jnp.int32)
                count = count + prefix
            rows.append(count)
        o_ref[b] = jnp.concatenate(rows, axis=0)


def _fox_plan(stats, ratio):
    bsz, nt, _, _ = stats.shape
    return pl.pallas_call(
        functools.partial(_plan_kernel, ratio=ratio),
        grid=(1,),
        in_specs=[pl.BlockSpec((bsz, nt, 8, LANES), lambda i: (0, 0, 0, 0))],
        out_specs=pl.BlockSpec((bsz, nt // ratio, LANES), lambda i: (0, 0, 0)),
        out_shape=jax.ShapeDtypeStruct((bsz, nt // ratio, LANES), jnp.int32),
        compiler_params=_params("arbitrary"),
        name="fox_plan",
    )(stats)


def _swa_kernel(q_ref, kp_ref, kc_ref, vtp_ref, vtc_ref, sink_ref, o_ref):
    i = pl.program_id(1)
    tq = q_ref.shape[1]
    nsub = tq // WINDOW
    ncol = A_Q_W // LANES
    lane_q = lax.broadcasted_iota(jnp.int32, (WINDOW, LANES), 1)
    low = lane_q < HEAD_DIM
    low_dim = lax.broadcasted_iota(jnp.int32, (LANES, WINDOW), 0) < HEAD_DIM
    k_loc = lax.broadcasted_iota(jnp.int32, (2 * WINDOW, WINDOW), 0)
    q_loc = lax.broadcasted_iota(jnp.int32, (2 * WINDOW, WINDOW), 1) + WINDOW
    rel = q_loc - k_loc
    band = (rel >= 0) & (rel < WINDOW)
    sink = sink_ref[...] * LOG2E
    ones = jnp.ones((ONES_ROWS, 2 * WINDOW), BF16)
    zero = jnp.zeros((WINDOW, LANES), BF16)

    def keys(j):
        if j == 0:
            return jnp.concatenate([kp_ref[0], kc_ref[0, :WINDOW, :]], axis=0)
        return kc_ref[0, (j - 1) * WINDOW:(j + 1) * WINDOW, :]

    def values_t(j):
        if j == 0:
            return jnp.concatenate([vtp_ref[0], vtc_ref[0, :, :WINDOW]], axis=1)
        return vtc_ref[0, :, (j - 1) * WINDOW:(j + 1) * WINDOW]

    def scores(j):
        q = q_ref[0, j * WINDOW:(j + 1) * WINDOW, :]
        parts = [jnp.where(low, q[:, c * LANES:(c + 1) * LANES], zero) for c in range(ncol)]
        parts += [jnp.where(low, zero, q[:, c * LANES:(c + 1) * LANES]) for c in range(ncol)]
        qs = jnp.concatenate(parts, axis=0)
        return lax.dot_general(keys(j), qs, (((1,), (1,)), ((), ())), preferred_element_type=F32)

    def softmax_pv(j, st):
        valid = band & ((k_loc >= WINDOW) | (i > 0)) if j == 0 else band
        ps, sink_terms = [], []
        for hh in range(A_Q_HEADS):
            cols = slice(hh * WINDOW, (hh + 1) * WINDOW)
            w = jnp.where(valid, st[:, cols], NEG_INF)
            m = jnp.maximum(jnp.max(w, axis=0, keepdims=True), sink[:, cols])
            ps.append(jnp.exp2(w - m).astype(BF16))
            sink_terms.append(jnp.exp2(sink[:, cols] - m))
        pt = jnp.concatenate(ps, axis=1)
        ot = jnp.dot(jnp.concatenate([values_t(j), ones], axis=0), pt, preferred_element_type=F32)
        for c in range(ncol):
            halves = []
            for hh in (c, ncol + c):
                cols = slice(hh * WINDOW, (hh + 1) * WINDOW)
                denom = ot[LANES:LANES + 1, cols] + sink_terms[hh]
                halves.append(ot[:LANES, cols] * (1.0 / denom))
            o_ref[0, j * WINDOW:(j + 1) * WINDOW, c * LANES:(c + 1) * LANES] = (
                jnp.transpose(jnp.where(low_dim, halves[0], halves[1])).astype(BF16))

    st = scores(0)
    for j in range(nsub):
        st_next = scores(j + 1) if j + 1 < nsub else None
        softmax_pv(j, st)
        st = st_next


def _swa(qa, ka, vat, sinks, tq):
    bsz, s, _ = qa.shape
    r = tq // WINDOW
    cur = lambda b, i: (b, i, 0)
    return pl.pallas_call(
        _swa_kernel,
        grid=(bsz, s // tq),
        in_specs=[pl.BlockSpec((1, tq, A_Q_W), cur),
                  pl.BlockSpec((1, WINDOW, A_KV_W), lambda b, i: (b, jnp.maximum(i * r - 1, 0), 0)),
                  pl.BlockSpec((1, tq, A_KV_W), cur),
                  pl.BlockSpec((1, A_KV_W, WINDOW), lambda b, i: (b, 0, jnp.maximum(i * r - 1, 0))),
                  pl.BlockSpec((1, A_KV_W, tq), lambda b, i: (b, 0, i)),
                  pl.BlockSpec((1, A_Q_HEADS * WINDOW), lambda b, i: (0, 0))],
        out_specs=pl.BlockSpec((1, tq, A_Q_W), cur),
        out_shape=jax.ShapeDtypeStruct((bsz, s, A_Q_W), BF16),
        compiler_params=_params("parallel", "parallel"),
        name="swa",
    )(qa, ka, ka, vat, vat, sinks)


def _fox_kernel(skip_ref, q_ref, k_ref, vt_ref, o_ref, *scratch, tq):
    nq = q_ref.shape[1] // tq

    def tile(qi, carry):
        _fox_tile(qi, tq, nq, skip_ref, q_ref, k_ref, vt_ref, o_ref, *scratch)
        return carry

    lax.fori_loop(0, nq, tile, 0)


def _fox_tile(qi, tq, nq, skip_ref, q_ref, k_ref, vt_ref, o_ref, m_ref, acc_ref, sta_ref, stb_ref):
    tk = vt_ref.shape[3]
    nchunk = tq // LANES
    q_row0 = pl.multiple_of(qi * tq, tq)
    assert tq == 2 * tk
    m_ref[...] = jnp.full(m_ref.shape, NEG_INF, F32)
    acc_ref[...] = jnp.zeros(acc_ref.shape, F32)
    ones = jnp.ones((ONES_ROWS, tk), BF16)

    def scores(kj, st_ref, c0=0, c1=nchunk):
        start = pl.multiple_of(kj * tk, tk)
        for e in range(2):
            group = slice(e * LANES, (e + 1) * LANES)
            st_ref[e, :, c0 * LANES:c1 * LANES] = lax.dot_general(
                k_ref[0, pl.ds(start, tk), group],
                q_ref[0, pl.ds(pl.multiple_of(q_row0 + c0 * LANES, LANES), (c1 - c0) * LANES), group],
                (((1,), (1,)), ((), ())), preferred_element_type=F32)

    def softmax_pv(kj, st_ref, first_query=None, c0=0, c1=nchunk):
        for e in range(2):
            vta = jnp.concatenate([vt_ref[0, kj, e * HEAD_DIM:(e + 1) * HEAD_DIM, :], ones], axis=0)
            ps, alphas = [], []
            for c in range(c0, c1):
                cols = slice(c * LANES, (c + 1) * LANES)
                off = None if first_query is None else first_query + c * LANES
                masked = off is not None and off < tk
                live = min(off + LANES, tk) if masked else tk
                z = st_ref[e, :live, cols]
                if masked:
                    causal = (lax.broadcasted_iota(jnp.int32, (live, LANES), 0)
                              <= lax.broadcasted_iota(jnp.int32, (live, LANES), 1) + off)
                    z = jnp.where(causal, z, NEG_INF)
                m_old = m_ref[e, :, cols]
                m_new = jnp.maximum(m_old, jnp.max(z, axis=0, keepdims=True))
                p = jnp.exp2(z - m_new).astype(BF16)
                if live < tk:
                    p = jnp.concatenate([p, jnp.zeros((tk - live, LANES), BF16)], axis=0)
                ps.append(p)
                alphas.append(jnp.exp2(m_old - m_new))
                m_ref[e, :, cols] = m_new
            pt = jnp.concatenate(ps, axis=1)
            alpha = jnp.concatenate(alphas, axis=1)
            span = slice(c0 * LANES, c1 * LANES)
            acc_ref[e, :, span] = alpha * acc_ref[e, :, span] + jnp.dot(vta, pt, preferred_element_type=F32)

    half = tk // LANES
    base = ((pl.program_id(0) * nq + qi) * 2 * B_HEADS + 2 * pl.program_id(1))
    lead = jnp.minimum(skip_ref[base], skip_ref[base + 1])
    first = jnp.minimum(jnp.minimum(skip_ref[base + B_HEADS], skip_ref[base + B_HEADS + 1]), 2 * qi)
    first = jnp.maximum(first, lead)

    second = half
    short = (qi >= 1) & (lead == 2 * qi - 1) & (first == 2 * qi)

    @pl.when(short)
    def _():
        scores(2 * qi - 1, stb_ref, c1=half)
        scores(2 * qi, sta_ref)
        scores(2 * qi + 1, stb_ref, c0=second)
        softmax_pv(2 * qi - 1, stb_ref, c1=half)
        softmax_pv(2 * qi, sta_ref, first_query=0, c0=second)
        softmax_pv(2 * qi, sta_ref, first_query=0, c1=half)
        softmax_pv(2 * qi + 1, stb_ref, first_query=-tk, c0=second)

    @pl.when(jnp.logical_not(short))
    def _():
        def early_half(u, carry):
            scores(u, stb_ref, c1=half)
            softmax_pv(u, stb_ref, c1=half)
            return carry

        lax.fori_loop(lead, first, early_half, 0)
        first_pair = (first + 1) // 2
        odd_start = first % 2 == 1

        @pl.when(odd_start)
        def _():
            scores(first, stb_ref)
            scores(first + 1, sta_ref)
            softmax_pv(first, stb_ref)

        @pl.when(jnp.logical_not(odd_start))
        def _():
            scores(first, sta_ref)

        def pair(t, carry):
            u = 2 * t
            scores(u + 1, stb_ref)
            softmax_pv(u, sta_ref)
            scores(u + 2, sta_ref)
            softmax_pv(u + 1, stb_ref)
            return carry

        lax.fori_loop(first_pair, qi, pair, 0)
        scores(2 * qi + 1, stb_ref, c0=second)
        softmax_pv(2 * qi, sta_ref, first_query=0)
        softmax_pv(2 * qi + 1, stb_ref, first_query=-tk, c0=second)

    out_t = jnp.concatenate(
        [acc_ref[e, :HEAD_DIM, :] * (1.0 / acc_ref[e, HEAD_DIM:HEAD_DIM + 1, :]) for e in range(2)], axis=0)
    o_ref[0, pl.ds(q_row0, tq), :] = jnp.transpose(out_t).astype(BF16)


def _fox(skip, qb, kb, vbt, tq):
    bsz, s, _ = qb.shape
    _, nk, _, tk = vbt.shape
    npair = B_HEADS // 2
    whole = pl.BlockSpec((1, s, 2 * LANES), lambda b, hp, skip: (b, 0, hp))
    return pl.pallas_call(
        functools.partial(_fox_kernel, tq=tq),
        grid_spec=pltpu.PrefetchScalarGridSpec(
            num_scalar_prefetch=1,
            grid=(bsz, npair),
            in_specs=[whole, whole,
                      pl.BlockSpec((1, nk, 2 * HEAD_DIM, tk), lambda b, hp, skip: (b, 0, hp, 0))],
            out_specs=pl.BlockSpec((1, s, 2 * HEAD_DIM), lambda b, hp, skip: (b, 0, hp)),
            scratch_shapes=[pltpu.VMEM((2, 1, tq), F32),
                            pltpu.VMEM((2, HEAD_DIM + ONES_ROWS, tq), F32),
                            pltpu.VMEM((2, tk, tq), F32),
                            pltpu.VMEM((2, tk, tq), F32)]),
        out_shape=jax.ShapeDtypeStruct((bsz, s, B_W), BF16),
        compiler_params=_params("parallel", "arbitrary"),
        name="fox",
    )(skip, qb, kb, vbt)


def _resident(shape):
    return pl.BlockSpec(shape, lambda b, i: (0,) * len(shape), pipeline_mode=pl.Buffered(1))


def _post_kernel(x_ref, oa_ref, ob_ref, ada_ref, gpre_ref, gpost_ref, wg_ref, wa_ref, wb_ref,
                 wo_ref, wfi_ref, wfo_ref, o_ref, wfi_out_ref, wfo_out_ref, *, sub):
    d = x_ref.shape[2]
    wfi_out_ref[...] = wfi_ref[...].astype(BF16)
    wfo_out_ref[...] = wfo_ref[...].astype(BF16)
    nt = (((1,), (1,)), ((), ()))
    merged = []
    for r in range(x_ref.shape[1] // sub):
        rows = slice(r * sub, (r + 1) * sub)
        h = _rms_mod(x_ref[0, rows, :], gpre_ref[...], ada_ref[0, 1:2, :], ada_ref[0, 0:1, :]).astype(BF16)
        a = jnp.dot(oa_ref[0, rows, :], wa_ref[...], preferred_element_type=F32)
        b = jnp.dot(ob_ref[0, rows, :], wb_ref[...], preferred_element_type=F32)
        ga = jax.nn.sigmoid(lax.dot_general(h, wg_ref[:d, :], nt, preferred_element_type=F32))
        gb = jax.nn.sigmoid(lax.dot_general(h, wg_ref[d:, :], nt, preferred_element_type=F32))
        merged.append((ga * a + gb * b).astype(BF16))
    for r in range(x_ref.shape[1] // sub):
        rows = slice(r * sub, (r + 1) * sub)
        y = jnp.dot(merged[r], wo_ref[...], preferred_element_type=F32)
        o_ref[0, rows, :] = x_ref[0, rows, :] + ada_ref[0, 2:3, :] * _rms(y, gpost_ref[...])


def _post(x, oa, ob, ada, g_pre, g_post, wg, wa, wb, wo, w_ffn_in, w_ffn_out, tm, sub):
    bsz, s, d = x.shape
    nt = s // tm
    steps = bsz * nt
    n_in = steps
    n_out = math.gcd(steps, w_ffn_out.shape[0] // BF16_ROWS)
    tok = lambda w: pl.BlockSpec((1, tm, w), lambda b, i: (b, i, 0))
    const = lambda shape: pl.BlockSpec(shape, lambda b, i: (0,) * len(shape))
    slab_in = pl.BlockSpec((d // n_in, w_ffn_in.shape[1]), lambda b, i: (b * nt + i, 0))
    slab_out = pl.BlockSpec((w_ffn_out.shape[0] // n_out, d), lambda b, i: ((b * nt + i) * n_out // steps, 0))
    return pl.pallas_call(
        functools.partial(_post_kernel, sub=sub),
        grid=(bsz, nt),
        in_specs=[tok(d), tok(A_Q_W), tok(B_W),
                  pl.BlockSpec((1, N_ADA, d), lambda b, i: (b, 0, 0)),
                  const((1, d)), const((1, d)),
                  _resident(wg.shape), _resident(wa.shape), _resident(wb.shape), _resident(wo.shape),
                  slab_in, slab_out],
        out_specs=[tok(d), slab_in, slab_out],
        out_shape=[jax.ShapeDtypeStruct((bsz, s, d), F32),
                   jax.ShapeDtypeStruct(w_ffn_in.shape, BF16),
                   jax.ShapeDtypeStruct(w_ffn_out.shape, BF16)],
        compiler_params=_params("arbitrary", "arbitrary"),
        name="post",
    )(x, oa, ob, ada, g_pre, g_post, wg, wa, wb, wo, w_ffn_in, w_ffn_out)


def _ffn_kernel(x_ref, ada_ref, gpre_ref, gpost_ref, wi_ref, wo_ref, o_ref, *, sub, chunks):
    d_ff = wo_ref.shape[0]
    for r in range(x_ref.shape[1] // sub):
        rows = slice(r * sub, (r + 1) * sub)
        x = x_ref[0, rows, :]
        h = _rms_mod(x, gpre_ref[...], ada_ref[0, 4:5, :], ada_ref[0, 3:4, :]).astype(BF16)
        acts = []
        lo = 0
        for width in chunks:
            g = jnp.dot(h, wi_ref[:, lo:lo + width], preferred_element_type=F32)
            u = jnp.dot(h, wi_ref[:, d_ff + lo:d_ff + lo + width], preferred_element_type=F32)
            acts.append((g * jax.nn.sigmoid(g) * u).astype(BF16))
            lo += width
        y = jnp.zeros(x.shape, F32)
        lo = 0
        for width, act in zip(chunks, acts):
            y = y + jnp.dot(act, wo_ref[lo:lo + width, :], preferred_element_type=F32)
            lo += width
        o_ref[0, rows, :] = x + ada_ref[0, 5:6, :] * _rms(y, gpost_ref[...])


def _ffn(x, ada, g_pre, g_post, wi, wo, tm, sub, chunks):
    bsz, s, d = x.shape
    tok = pl.BlockSpec((1, tm, d), lambda b, i: (b, i, 0))
    const = lambda shape: pl.BlockSpec(shape, lambda b, i: (0,) * len(shape))
    return pl.pallas_call(
        functools.partial(_ffn_kernel, sub=sub, chunks=chunks),
        grid=(bsz, s // tm),
        in_specs=[tok, pl.BlockSpec((1, N_ADA, d), lambda b, i: (b, 0, 0)),
                  const((1, d)), const((1, d)), _resident(wi.shape), _resident(wo.shape)],
        out_specs=tok,
        out_shape=jax.ShapeDtypeStruct((bsz, s, d), F32),
        compiler_params=_params("parallel", "parallel"),
        name="ffn",
    )(x, ada, g_pre, g_post, wi, wo)


def _pair_heads(w, axis):
    shape = w.shape
    pre, post = shape[:axis], shape[axis + 1:]
    w = w.reshape(pre + (A_KV_HEADS, A_GROUP, HEAD_DIM) + post)
    w = jnp.swapaxes(w, axis, axis + 1)
    return w.reshape(shape)


def kernel(x, c, positions, w_ada, b_ada, g_pre_mix, g_post_mix, w_in, b_f, sinks, w_branch_a,
           w_branch_b, w_out, g_pre_ffn, g_post_ffn, w_ffn_in, w_ffn_out):
    bsz, s, d = x.shape
    depth = w_ada.shape[0]
    tm = min(1024, s)
    sub = min(512, s)
    rep = LANES // ROPE_FREQS
    assert (s // rep) % sub == 0
    half = jnp.arange(0, HEAD_DIM, 2, dtype=F32) / HEAD_DIM
    inv_freq = jnp.tile(1.0 / (ROPE_THETA ** half), rep).reshape(1, LANES)
    pos_c = jnp.swapaxes(positions.astype(F32).reshape(bsz, rep, s // rep), 1, 2)
    pos_c = jnp.repeat(pos_c, ROPE_FREQS, axis=2)
    fine = sub // 2
    for l in range(depth):
        w_t = jnp.swapaxes(w_in[l], 0, 1)
        o_va, o_qb, o_vb = A_Q_W + A_KV_W, A_Q_W + 2 * A_KV_W, A_Q_W + 2 * A_KV_W + 2 * B_W
        w_f = jnp.pad(w_t[QKV_W:QKV_W + B_HEADS], ((0, LANES - B_HEADS), (0, 0)))
        w_qk = jnp.concatenate([_pair_heads(w_t[:A_Q_W], 0), w_t[o_qb:o_vb], w_t[A_Q_W:o_va], w_f],
                               axis=0).astype(BF16)
        w_vt = jnp.concatenate([w_t[o_va:o_qb], w_t[o_vb:QKV_W]], axis=0).astype(BF16)
        w_g = w_t[QKV_W + B_HEADS:].astype(BF16)
        bf_row = jnp.pad(b_f[l], (0, LANES - B_HEADS)).reshape(1, LANES)
        sink_row = jnp.repeat(sinks[l], WINDOW).reshape(1, A_Q_HEADS * WINDOW)
        w_a = _pair_heads(w_branch_a[l], 0).astype(BF16)
        w_b = w_branch_b[l].astype(BF16)
        w_o = w_out[l].astype(BF16)
        row = lambda g: g[l].reshape(1, d)

        if l == 0:
            ada, cos, sin = _ada_rope(c, w_ada[l], b_ada[l], pos_c, inv_freq)
        else:
            ada = _ada(c, w_ada[l], b_ada[l])
        ada = ada.reshape(bsz, N_ADA, d)
        qa, ka, vat, qb, kb, vbt, stats = _in_proj(x, cos, sin, ada, row(g_pre_mix), w_qk, w_vt, bf_row,
                                                   tm, sub)
        o_a = _swa(qa, ka, vat, sink_row, min(2 * tm, s))
        skip = _fox_plan(stats, 1)[:, :, :B_HEADS].reshape(-1)
        o_b = _fox(skip, qb, kb, vbt, 2 * sub)
        x, w_fi, w_fo = _post(x, o_a, o_b, ada, row(g_pre_mix), row(g_post_mix), w_g, w_a, w_b, w_o,
                              w_ffn_in[l], w_ffn_out[l], tm, fine)
        n_tiles = w_fo.shape[0] // MXU_DEPTH
        chunks = ((n_tiles + 1) // 2 * MXU_DEPTH, n_tiles // 2 * MXU_DEPTH)
        x = _ffn(x, ada, row(g_pre_ffn), row(g_post_ffn), w_fi, w_fo, tm, fine, chunks)
    return x
```

```python
import functools
import math

import jax
import jax.numpy as jnp
import numpy as np
from jax import lax
from jax.experimental import pallas as pl
from jax.experimental.pallas import tpu as pltpu

HEAD_DIM = 64
WINDOW = 128
A_Q_HEADS = 8
A_KV_HEADS = 2
A_GROUP = A_Q_HEADS // A_KV_HEADS
B_HEADS = 8
N_ADA = 6
ROPE_THETA = 10000.0
RMS_EPS = 1e-6
LANES = 128
MXU_DEPTH = 256
ROPE_FREQS = HEAD_DIM // 2
BF16_ROWS = 16
VMEM_LIMIT = 56 * 1024 * 1024

A_Q_W = A_Q_HEADS * HEAD_DIM
A_KV_W = A_KV_HEADS * HEAD_DIM
B_W = B_HEADS * HEAD_DIM
QKV_W = A_Q_W + 2 * A_KV_W + 3 * B_W
BF16 = jnp.bfloat16
F32 = jnp.float32
NEG_INF = float("-inf")
LOG2E = math.log2(math.e)
ONES_ROWS = 16
SKIP_LOG2 = 48.0
NORM_MARGIN = 1.02
NORM_ROWS = 256


def _params(*sem, flags=None):
    return pltpu.CompilerParams(dimension_semantics=sem, vmem_limit_bytes=VMEM_LIMIT, flags=flags)


def _rms_mod(x, g, scale, shift):
    ms = jnp.mean(x * x, axis=-1, keepdims=True)
    return (x * lax.rsqrt(ms + RMS_EPS)) * (g * (1.0 + scale)) + shift


def _rms(y, g):
    ms = jnp.mean(y * y, axis=-1, keepdims=True)
    return y * lax.rsqrt(ms + RMS_EPS) * g


def _ada_kernel(c_ref, w_ref, b_ref, o_ref):
    o_ref[...] = jnp.dot(c_ref[...], w_ref[...], preferred_element_type=F32) + b_ref[...]


def _ada(c, w, b):
    bsz, d = c.shape
    n = w.shape[1]
    tn = n // 4
    return pl.pallas_call(
        _ada_kernel,
        grid=(n // tn,),
        in_specs=[pl.BlockSpec((bsz, d), lambda j: (0, 0)),
                  pl.BlockSpec((d, tn), lambda j: (0, j)),
                  pl.BlockSpec((1, tn), lambda j: (0, j))],
        out_specs=pl.BlockSpec((bsz, tn), lambda j: (0, j)),
        out_shape=jax.ShapeDtypeStruct((bsz, n), F32),
        compiler_params=_params("arbitrary"),
        name="ada",
    )(c, w, b.reshape(1, n))


def _ada_rope_kernel(c_ref, w_ref, b_ref, pos_ref, invf_ref, ada_ref, cos_ref, sin_ref):
    ada_ref[...] = jnp.dot(c_ref[...], w_ref[...], preferred_element_type=F32) + b_ref[...]
    ang = pos_ref[0] * invf_ref[...]
    cos_ref[0] = jnp.cos(ang)
    sin_ref[0] = jnp.sin(ang)


def _ada_rope(c, w, b, pos_c, inv_freq):
    bsz, d = c.shape
    n = w.shape[1]
    tn = n // bsz
    assert tn % LANES == 0
    rows = pos_c.shape[1]
    tab = pl.BlockSpec((1, rows, LANES), lambda j: (j, 0, 0))
    tab_shape = jax.ShapeDtypeStruct((bsz, rows, LANES), F32)
    return pl.pallas_call(
        _ada_rope_kernel,
        grid=(bsz,),
        in_specs=[pl.BlockSpec((bsz, d), lambda j: (0, 0)),
                  pl.BlockSpec((d, tn), lambda j: (0, j)),
                  pl.BlockSpec((1, tn), lambda j: (0, j)),
                  tab, pl.BlockSpec((1, LANES), lambda j: (0, 0))],
        out_specs=[pl.BlockSpec((bsz, tn), lambda j: (0, j)), tab, tab],
        out_shape=[jax.ShapeDtypeStruct((bsz, n), F32), tab_shape, tab_shape],
        compiler_params=_params("arbitrary"),
        name="ada_rope",
    )(c, w, b.reshape(1, n), pos_c, inv_freq)


AUG_PARTS = 3


def _aug_base(head):
    return head * LANES + (HEAD_DIM if head % 2 == 0 else 0)


def _aug_placement():
    p = np.zeros((LANES, B_HEADS * LANES), np.float32)
    for head in range(B_HEADS):
        for part in range(AUG_PARTS):
            p[part * B_HEADS + head, _aug_base(head) + part] = 1.0
            p[part * B_HEADS + head, _aug_base(head) + AUG_PARTS + part] = -1.0
    return jnp.asarray(p, BF16)


def _inproj_kernel(x_ref, cos_ref, sin_ref, ada_ref, g_ref, w_ref, wvt_ref, bf_ref, place_ref, hsum_ref,
                   qa_ref, ka_ref, vat_ref, qb_ref, kb_ref, vbt_ref, stat_ref, carry_ref):
    i = pl.program_id(1)
    sub = vbt_ref.shape[3]
    lane = lax.broadcasted_iota(jnp.int32, (sub, LANES), 1)
    row = lax.broadcasted_iota(jnp.int32, (sub, LANES), 0)
    first_half = (lane % HEAD_DIM) < (HEAD_DIM // 2)
    low = lane < HEAD_DIM
    q_scale = LOG2E / math.sqrt(HEAD_DIM)

    @pl.when(i == 0)
    def _():
        carry_ref[...] = jnp.zeros(carry_ref.shape, F32)

    carry = carry_ref[0:1, :]
    pending = []
    for r in range(x_ref.shape[1] // sub):
        rows = slice(r * sub, (r + 1) * sub)
        h = _rms_mod(x_ref[0, rows, :], g_ref[...], ada_ref[0, 1:2, :], ada_ref[0, 0:1, :]).astype(BF16)

        def proj(lo, width):
            return lax.dot_general(h, w_ref[lo:lo + width, :], (((1,), (1,)), ((), ())),
                                   preferred_element_type=F32)

        def proj_t(lo, width):
            return lax.dot_general(wvt_ref[lo:lo + width, :], h, (((1,), (1,)), ((), ())),
                                   preferred_element_type=F32)

        t0 = i * x_ref.shape[1] + r * sub
        quarter = t0 // cos_ref.shape[1]
        t_rows = pl.ds(pl.multiple_of(t0 - quarter * cos_ref.shape[1], sub), sub)
        shift = (LANES - quarter * ROPE_FREQS) % LANES

        def spread(tab):
            y = jnp.where(lane < ROPE_FREQS, pltpu.roll(tab, shift, axis=1), 0.0)
            y = y + pltpu.roll(y, ROPE_FREQS, axis=1)
            return y + pltpu.roll(y, 2 * ROPE_FREQS, axis=1)

        cos = spread(cos_ref[0, t_rows, :])
        sin = spread(sin_ref[0, t_rows, :])
        sin_signed = jnp.where(first_half, -sin, sin)

        def rope(t):
            partner = jnp.where(first_half,
                                pltpu.roll(t, LANES - HEAD_DIM // 2, axis=1),
                                pltpu.roll(t, HEAD_DIM // 2, axis=1))
            return t * cos + partner * sin_signed

        kf = proj(A_Q_W + 2 * B_W, A_KV_W + LANES)
        ka_ref[0, rows, :] = rope(kf[:, :A_KV_W]).astype(BF16)
        fl = kf[:, A_KV_W:] + bf_ref[...]
        c = jnp.minimum(fl, 0.0) - jnp.log1p(jnp.exp(-jnp.abs(fl)))
        d = 1
        while d < sub:
            c = c + jnp.where(row >= d, pltpu.roll(c, d, axis=0), 0.0)
            d *= 2
        c = c + carry
        carry = c[sub - 1:sub, :]
        c = c * LOG2E
        hi = c.astype(BF16).astype(F32)
        r1 = c - hi
        mid = r1.astype(BF16).astype(F32)
        lo = r1 - mid
        packed = jnp.where(lane < B_HEADS, hi,
                           jnp.where(lane < 2 * B_HEADS, pltpu.roll(mid, B_HEADS, axis=1),
                                     jnp.where(lane < 3 * B_HEADS, pltpu.roll(lo, 2 * B_HEADS, axis=1), 0.0)))

        qa = proj(0, A_Q_W)
        for j in range(A_Q_W // LANES):
            qa_ref[0, rows, j * LANES:(j + 1) * LANES] = (
                rope(qa[:, j * LANES:(j + 1) * LANES]) * q_scale).astype(BF16)
        vat_ref[0, :, rows] = proj_t(0, A_KV_W).astype(BF16)
        vbt_ref[0, r] = proj_t(A_KV_W, B_W).astype(BF16)
        qb = proj(A_Q_W, B_W) * q_scale
        kb = proj(A_Q_W + B_W, B_W)
        pending.append((rows, r, packed, qb, kb, c))

    for rows, r, packed, qb, kb, c in pending:
        aug = jnp.dot(packed.astype(BF16), place_ref[...], preferred_element_type=F32)

        def norm_bound(t):
            sq = t * t
            fold = sq[:NORM_ROWS]
            for g in range(1, sub // NORM_ROWS):
                fold = jnp.maximum(fold, sq[g * NORM_ROWS:(g + 1) * NORM_ROWS])
            per_head = jnp.dot(fold.astype(BF16), hsum_ref[...], preferred_element_type=F32)
            return jnp.max(per_head, axis=0, keepdims=True) * NORM_MARGIN

        stats = [norm_bound(qb), norm_bound(kb)]
        stats += [c[0:1, :], c[sub - 1:sub, :], jnp.zeros((4, LANES), F32)]
        stat_ref[0, r] = jnp.concatenate(stats, axis=0)
        for head in range(B_HEADS):
            base = _aug_base(head) % LANES
            own = low if head % 2 == 0 else jnp.logical_not(low)
            plus = (lane >= base) & (lane < base + AUG_PARTS)
            minus = (lane >= base + AUG_PARTS) & (lane < base + 2 * AUG_PARTS)
            a = aug[:, head * LANES:(head + 1) * LANES]
            src = slice(head // 2 * LANES, (head // 2 + 1) * LANES)
            group = slice(head * LANES, (head + 1) * LANES)
            qb_ref[0, rows, group] = jnp.where(own, qb[:, src],
                                               jnp.where(plus, a, jnp.where(minus, 1.0, 0.0))).astype(BF16)
            kb_ref[0, rows, group] = jnp.where(own, kb[:, src],
                                               jnp.where(minus, a, jnp.where(plus, 1.0, 0.0))).astype(BF16)
    carry_ref[0:1, :] = carry


def _in_proj(x, cos, sin, ada, g_pre, w_qk, w_vt, b_f, tm, sub):
    bsz, s, d = x.shape
    grid = (bsz, s // tm)
    tok = lambda w: pl.BlockSpec((1, tm, w), lambda b, i: (b, i, 0))
    const = lambda shape: pl.BlockSpec(shape, lambda b, i: (0,) * len(shape))
    act = lambda w: jax.ShapeDtypeStruct((bsz, s, w), BF16)
    place = _aug_placement()
    head_sum = jnp.asarray(np.kron(np.eye(B_HEADS, LANES), np.ones((HEAD_DIM, 1))), BF16)
    return pl.pallas_call(
        _inproj_kernel,
        grid=grid,
        in_specs=[tok(d), pl.BlockSpec((1,) + cos.shape[1:], lambda b, i: (b, 0, 0)),
                  pl.BlockSpec((1,) + sin.shape[1:], lambda b, i: (b, 0, 0)),
                  pl.BlockSpec((1, N_ADA, d), lambda b, i: (b, 0, 0)),
                  const((1, d)), _resident(w_qk.shape), _resident(w_vt.shape),
                  const((1, LANES)), const(place.shape), const(head_sum.shape)],
        out_specs=[tok(A_Q_W), tok(A_KV_W),
                   pl.BlockSpec((1, A_KV_W, tm), lambda b, i: (b, 0, i)),
                   tok(B_HEADS * LANES), tok(B_HEADS * LANES),
                   pl.BlockSpec((1, tm // sub, B_W, sub), lambda b, i: (b, i, 0, 0)),
                   pl.BlockSpec((1, tm // sub, 8, LANES), lambda b, i: (b, i, 0, 0))],
        out_shape=[act(A_Q_W), act(A_KV_W),
                   jax.ShapeDtypeStruct((bsz, A_KV_W, s), BF16),
                   act(B_HEADS * LANES), act(B_HEADS * LANES),
                   jax.ShapeDtypeStruct((bsz, s // sub, B_W, sub), BF16),
                   jax.ShapeDtypeStruct((bsz, s // sub, 8, LANES), F32)],
        scratch_shapes=[pltpu.VMEM((8, LANES), F32)],
        compiler_params=_params("parallel", "arbitrary"),
        name="in_proj",
    )(x, cos, sin, ada, g_pre, w_qk, w_vt, b_f, place, head_sum)


def _plan_kernel(st_ref, o_ref, *, ratio):
    nt = st_ref.shape[1]
    q_scale = LOG2E / math.sqrt(HEAD_DIM)
    for b in range(st_ref.shape[0]):
        qn = st_ref[b, 0, 0:1, :]
        kn = st_ref[b, 0, 1:2, :]
        for j in range(1, nt):
            qn = jnp.maximum(qn, st_ref[b, j, 0:1, :])
            kn = jnp.maximum(kn, st_ref[b, j, 1:2, :])
        spread = qn * (1.0 / q_scale) + kn * q_scale
        rows = []
        for blk in range(nt // ratio):
            limit = st_ref[b, blk * ratio, 2:3, :] + spread + SKIP_LOG2
            count = jnp.zeros((1, LANES), jnp.int32)
            prefix = jnp.ones((1, LANES), jnp.int32)
            for j in range(blk * ratio):
                prefix = prefix * (st_ref[b, j, 3:4, :] > limit).astype(jnp.int32)
                count = count + prefix
            rows.append(count)
        o_ref[b] = jnp.concatenate(rows, axis=0)


def _fox_plan(stats, ratio):
    bsz, nt, _, _ = stats.shape
    return pl.pallas_call(
        functools.partial(_plan_kernel, ratio=ratio),
        grid=(1,),
        in_specs=[pl.BlockSpec((bsz, nt, 8, LANES), lambda i: (0, 0, 0, 0))],
        out_specs=pl.BlockSpec((bsz, nt // ratio, LANES), lambda i: (0, 0, 0)),
        out_shape=jax.ShapeDtypeStruct((bsz, nt // ratio, LANES), jnp.int32),
        compiler_params=_params("arbitrary"),
        name="fox_plan",
    )(stats)


def _swa_kernel(q_ref, kp_ref, kc_ref, vtp_ref, vtc_ref, sink_ref, o_ref):
    i = pl.program_id(1)
    tq = q_ref.shape[1]
    nsub = tq // WINDOW
    ncol = A_Q_W // LANES
    lane_q = lax.broadcasted_iota(jnp.int32, (WINDOW, LANES), 1)
    low = lane_q < HEAD_DIM
    low_dim = lax.broadcasted_iota(jnp.int32, (LANES, WINDOW), 0) < HEAD_DIM
    from_prev = (lax.broadcasted_iota(jnp.int32, (WINDOW, WINDOW), 0)
                 > lax.broadcasted_iota(jnp.int32, (WINDOW, WINDOW), 1))
    sink = sink_ref[...] * LOG2E
    ones = jnp.ones((ONES_ROWS, 2 * WINDOW), BF16)
    zero = jnp.zeros((WINDOW, LANES), BF16)

    def keys(j):
        if j == 0:
            return jnp.concatenate([kp_ref[0], kc_ref[0, :WINDOW, :]], axis=0)
        return kc_ref[0, (j - 1) * WINDOW:(j + 1) * WINDOW, :]

    def values_t(j):
        if j == 0:
            return jnp.concatenate([vtp_ref[0], vtc_ref[0, :, :WINDOW]], axis=1)
        return vtc_ref[0, :, (j - 1) * WINDOW:(j + 1) * WINDOW]

    def scores(j):
        q = q_ref[0, j * WINDOW:(j + 1) * WINDOW, :]
        parts = [jnp.where(low, q[:, c * LANES:(c + 1) * LANES], zero) for c in range(ncol)]
        parts += [jnp.where(low, zero, q[:, c * LANES:(c + 1) * LANES]) for c in range(ncol)]
        qs = jnp.concatenate(parts, axis=0)
        return lax.dot_general(keys(j), qs, (((1,), (1,)), ((), ())), preferred_element_type=F32)

    def softmax_pv(j, st):
        ps, sink_terms = [], []
        for hh in range(A_Q_HEADS):
            cols = slice(hh * WINDOW, (hh + 1) * WINDOW)
            prev = st[:WINDOW, cols]
            if j == 0:
                prev = jnp.where(i > 0, prev, NEG_INF)
            w = jnp.where(from_prev, prev, st[WINDOW:, cols])
            m = jnp.maximum(jnp.max(w, axis=0, keepdims=True), sink[:, cols])
            p = jnp.exp2(w - m)
            ps.append(jnp.concatenate([jnp.where(from_prev, p, 0.0), jnp.where(from_prev, 0.0, p)],
                                      axis=0).astype(BF16))
            sink_terms.append(jnp.exp2(sink[:, cols] - m))
        pt = jnp.concatenate(ps, axis=1)
        ot = jnp.dot(jnp.concatenate([values_t(j), ones], axis=0), pt, preferred_element_type=F32)
        for c in range(ncol):
            halves = []
            for hh in (c, ncol + c):
                cols = slice(hh * WINDOW, (hh + 1) * WINDOW)
                denom = ot[LANES:LANES + 1, cols] + sink_terms[hh]
                halves.append(ot[:LANES, cols] * (1.0 / denom))
            o_ref[0, j * WINDOW:(j + 1) * WINDOW, c * LANES:(c + 1) * LANES] = (
                jnp.transpose(jnp.where(low_dim, halves[0], halves[1])).astype(BF16))

    st = scores(0)
    for j in range(nsub):
        st_next = scores(j + 1) if j + 1 < nsub else None
        softmax_pv(j, st)
        st = st_next


def _swa(qa, ka, vat, sinks, tq):
    bsz, s, _ = qa.shape
    r = tq // WINDOW
    cur = lambda b, i: (b, i, 0)
    return pl.pallas_call(
        _swa_kernel,
        grid=(bsz, s // tq),
        in_specs=[pl.BlockSpec((1, tq, A_Q_W), cur),
                  pl.BlockSpec((1, WINDOW, A_KV_W), lambda b, i: (b, jnp.maximum(i * r - 1, 0), 0)),
                  pl.BlockSpec((1, tq, A_KV_W), cur),
                  pl.BlockSpec((1, A_KV_W, WINDOW), lambda b, i: (b, 0, jnp.maximum(i * r - 1, 0))),
                  pl.BlockSpec((1, A_KV_W, tq), lambda b, i: (b, 0, i)),
                  pl.BlockSpec((1, A_Q_HEADS * WINDOW), lambda b, i: (0, 0))],
        out_specs=pl.BlockSpec((1, tq, A_Q_W), cur),
        out_shape=jax.ShapeDtypeStruct((bsz, s, A_Q_W), BF16),
        compiler_params=_params("parallel", "parallel"),
        name="swa",
    )(qa, ka, ka, vat, vat, sinks)


def _fox_kernel(skip_ref, q_ref, k_ref, vt_ref, o_ref, *scratch, tq):
    nq = q_ref.shape[1] // tq

    def tile(qi, carry):
        _fox_tile(qi, tq, nq, skip_ref, q_ref, k_ref, vt_ref, o_ref, *scratch)
        return carry

    lax.fori_loop(0, nq, tile, 0)


def _fox_tile(qi, tq, nq, skip_ref, q_ref, k_ref, vt_ref, o_ref, m_ref, acc_ref, sta_ref, stb_ref):
    tk = vt_ref.shape[3]
    nchunk = tq // LANES
    q_row0 = pl.multiple_of(qi * tq, tq)
    assert tq == 2 * tk
    m_ref[...] = jnp.full(m_ref.shape, NEG_INF, F32)
    acc_ref[...] = jnp.zeros(acc_ref.shape, F32)
    ones = jnp.ones((ONES_ROWS, tk), BF16)

    def scores(kj, st_ref, c0=0, c1=nchunk):
        start = pl.multiple_of(kj * tk, tk)
        for e in range(2):
            group = slice(e * LANES, (e + 1) * LANES)
            st_ref[e, :, c0 * LANES:c1 * LANES] = lax.dot_general(
                k_ref[0, pl.ds(start, tk), group],
                q_ref[0, pl.ds(pl.multiple_of(q_row0 + c0 * LANES, LANES), (c1 - c0) * LANES), group],
                (((1,), (1,)), ((), ())), preferred_element_type=F32)

    def softmax_pv(kj, st_ref, first_query=None, c0=0, c1=nchunk):
        for e in range(2):
            vta = jnp.concatenate([vt_ref[0, kj, e * HEAD_DIM:(e + 1) * HEAD_DIM, :], ones], axis=0)
            ps, alphas = [], []
            for c in range(c0, c1):
                cols = slice(c * LANES, (c + 1) * LANES)
                off = None if first_query is None else first_query + c * LANES
                masked = off is not None and off < tk
                live = min(off + LANES, tk) if masked else tk
                z = st_ref[e, :live, cols]
                if masked:
                    causal = (lax.broadcasted_iota(jnp.int32, (live, LANES), 0)
                              <= lax.broadcasted_iota(jnp.int32, (live, LANES), 1) + off)
                    z = jnp.where(causal, z, NEG_INF)
                m_old = m_ref[e, :, cols]
                m_new = jnp.maximum(m_old, jnp.max(z, axis=0, keepdims=True))
                p = jnp.exp2(z - m_new).astype(BF16)
                if live < tk:
                    p = jnp.concatenate([p, jnp.zeros((tk - live, LANES), BF16)], axis=0)
                ps.append(p)
                alphas.append(jnp.exp2(m_old - m_new))
                m_ref[e, :, cols] = m_new
            pt = jnp.concatenate(ps, axis=1)
            alpha = jnp.concatenate(alphas, axis=1)
            span = slice(c0 * LANES, c1 * LANES)
            acc_ref[e, :, span] = alpha * acc_ref[e, :, span] + jnp.dot(vta, pt, preferred_element_type=F32)

    half = tk // LANES
    base = ((pl.program_id(0) * nq + qi) * 2 * B_HEADS + 2 * pl.program_id(1))
    lead = jnp.minimum(skip_ref[base], skip_ref[base + 1])
    first = jnp.minimum(jnp.minimum(skip_ref[base + B_HEADS], skip_ref[base + B_HEADS + 1]), 2 * qi)
    first = jnp.maximum(first, lead)

    second = half
    short = (qi >= 1) & (lead == 2 * qi - 1) & (first == 2 * qi)

    @pl.when(short)
    def _():
        scores(2 * qi - 1, stb_ref, c1=half)
        scores(2 * qi, sta_ref)
        scores(2 * qi + 1, stb_ref, c0=second)
        softmax_pv(2 * qi - 1, stb_ref, c1=half)
        softmax_pv(2 * qi, sta_ref, first_query=0, c0=second)
        softmax_pv(2 * qi, sta_ref, first_query=0, c1=half)
        softmax_pv(2 * qi + 1, stb_ref, first_query=-tk, c0=second)

    @pl.when(jnp.logical_not(short))
    def _():
        def early_half(u, carry):
            scores(u, stb_ref, c1=half)
            softmax_pv(u, stb_ref, c1=half)
            return carry

        lax.fori_loop(lead, first, early_half, 0)
        first_pair = (first + 1) // 2
        odd_start = first % 2 == 1

        @pl.when(odd_start)
        def _():
            scores(first, stb_ref)
            scores(first + 1, sta_ref)
            softmax_pv(first, stb_ref)

        @pl.when(jnp.logical_not(odd_start))
        def _():
            scores(first, sta_ref)

        def pair(t, carry):
            u = 2 * t
            scores(u + 1, stb_ref)
            softmax_pv(u, sta_ref)
            scores(u + 2, sta_ref)
            softmax_pv(u + 1, stb_ref)
            return carry

        lax.fori_loop(first_pair, qi, pair, 0)
        scores(2 * qi + 1, stb_ref, c0=second)
        softmax_pv(2 * qi, sta_ref, first_query=0)
        softmax_pv(2 * qi + 1, stb_ref, first_query=-tk, c0=second)

    out_t = jnp.concatenate(
        [acc_ref[e, :HEAD_DIM, :] * (1.0 / acc_ref[e, HEAD_DIM:HEAD_DIM + 1, :]) for e in range(2)], axis=0)
    o_ref[0, pl.ds(q_row0, tq), :] = jnp.transpose(out_t).astype(BF16)


def _fox(skip, qb, kb, vbt, tq):
    bsz, s, _ = qb.shape
    _, nk, _, tk = vbt.shape
    npair = B_HEADS // 2
    whole = pl.BlockSpec((1, s, 2 * LANES), lambda b, hp, skip: (b, 0, hp))
    return pl.pallas_call(
        functools.partial(_fox_kernel, tq=tq),
        grid_spec=pltpu.PrefetchScalarGridSpec(
            num_scalar_prefetch=1,
            grid=(bsz, npair),
            in_specs=[whole, whole,
                      pl.BlockSpec((1, nk, 2 * HEAD_DIM, tk), lambda b, hp, skip: (b, 0, hp, 0))],
            out_specs=pl.BlockSpec((1, s, 2 * HEAD_DIM), lambda b, hp, skip: (b, 0, hp)),
            scratch_shapes=[pltpu.VMEM((2, 1, tq), F32),
                            pltpu.VMEM((2, HEAD_DIM + ONES_ROWS, tq), F32),
                            pltpu.VMEM((2, tk, tq), F32),
                            pltpu.VMEM((2, tk, tq), F32)]),
        out_shape=jax.ShapeDtypeStruct((bsz, s, B_W), BF16),
        compiler_params=_params("parallel", "arbitrary"),
        name="fox",
    )(skip, qb, kb, vbt)


def _resident(shape):
    return pl.BlockSpec(shape, lambda b, i: (0,) * len(shape), pipeline_mode=pl.Buffered(1))


def _post_kernel(x_ref, oa_ref, ob_ref, ada_ref, gpre_ref, gpost_ref, wg_ref, wa_ref, wb_ref,
                 wo_ref, wfi_ref, wfo_ref, o_ref, wfi_out_ref, wfo_out_ref, *, sub):
    d = x_ref.shape[2]
    wfi_out_ref[...] = wfi_ref[...].astype(BF16)
    wfo_out_ref[...] = wfo_ref[...].astype(BF16)
    nt = (((1,), (1,)), ((), ()))
    merged = []
    for r in range(x_ref.shape[1] // sub):
        rows = slice(r * sub, (r + 1) * sub)
        h = _rms_mod(x_ref[0, rows, :], gpre_ref[...], ada_ref[0, 1:2, :], ada_ref[0, 0:1, :]).astype(BF16)
        a = jnp.dot(oa_ref[0, rows, :], wa_ref[...], preferred_element_type=F32)
        b = jnp.dot(ob_ref[0, rows, :], wb_ref[...], preferred_element_type=F32)
        ga = jax.nn.sigmoid(lax.dot_general(h, wg_ref[:d, :], nt, preferred_element_type=F32))
        gb = jax.nn.sigmoid(lax.dot_general(h, wg_ref[d:, :], nt, preferred_element_type=F32))
        merged.append((ga * a + gb * b).astype(BF16))
    for r in range(x_ref.shape[1] // sub):
        rows = slice(r * sub, (r + 1) * sub)
        y = jnp.dot(merged[r], wo_ref[...], preferred_element_type=F32)
        o_ref[0, rows, :] = x_ref[0, rows, :] + ada_ref[0, 2:3, :] * _rms(y, gpost_ref[...])


def _post(x, oa, ob, ada, g_pre, g_post, wg, wa, wb, wo, w_ffn_in, w_ffn_out, tm, sub):
    bsz, s, d = x.shape
    nt = s // tm
    steps = bsz * nt
    n_in = steps
    n_out = math.gcd(steps, w_ffn_out.shape[0] // BF16_ROWS)
    tok = lambda w: pl.BlockSpec((1, tm, w), lambda b, i: (b, i, 0))
    const = lambda shape: pl.BlockSpec(shape, lambda b, i: (0,) * len(shape))
    slab_in = pl.BlockSpec((d // n_in, w_ffn_in.shape[1]), lambda b, i: (b * nt + i, 0))
    slab_out = pl.BlockSpec((w_ffn_out.shape[0] // n_out, d), lambda b, i: ((b * nt + i) * n_out // steps, 0))
    return pl.pallas_call(
        functools.partial(_post_kernel, sub=sub),
        grid=(bsz, nt),
        in_specs=[tok(d), tok(A_Q_W), tok(B_W),
                  pl.BlockSpec((1, N_ADA, d), lambda b, i: (b, 0, 0)),
                  const((1, d)), const((1, d)),
                  _resident(wg.shape), _resident(wa.shape), _resident(wb.shape), _resident(wo.shape),
                  slab_in, slab_out],
        out_specs=[tok(d), slab_in, slab_out],
        out_shape=[jax.ShapeDtypeStruct((bsz, s, d), F32),
                   jax.ShapeDtypeStruct(w_ffn_in.shape, BF16),
                   jax.ShapeDtypeStruct(w_ffn_out.shape, BF16)],
        compiler_params=_params("arbitrary", "arbitrary"),
        name="post",
    )(x, oa, ob, ada, g_pre, g_post, wg, wa, wb, wo, w_ffn_in, w_ffn_out)


def _ffn_kernel(x_ref, ada_ref, gpre_ref, gpost_ref, wi_ref, wo_ref, o_ref, *, sub, chunks):
    d_ff = wo_ref.shape[0]
    for r in range(x_ref.shape[1] // sub):
        rows = slice(r * sub, (r + 1) * sub)
        x = x_ref[0, rows, :]
        h = _rms_mod(x, gpre_ref[...], ada_ref[0, 4:5, :], ada_ref[0, 3:4, :]).astype(BF16)
        acts = []
        lo = 0
        for width in chunks:
            g = jnp.dot(h, wi_ref[:, lo:lo + width], preferred_element_type=F32)
            u = jnp.dot(h, wi_ref[:, d_ff + lo:d_ff + lo + width], preferred_element_type=F32)
            acts.append((g * jax.nn.sigmoid(g) * u).astype(BF16))
            lo += width
        y = jnp.zeros(x.shape, F32)
        lo = 0
        for width, act in zip(chunks, acts):
            y = y + jnp.dot(act, wo_ref[lo:lo + width, :], preferred_element_type=F32)
            lo += width
        o_ref[0, rows, :] = x + ada_ref[0, 5:6, :] * _rms(y, gpost_ref[...])


def _ffn(x, ada, g_pre, g_post, wi, wo, tm, sub, chunks):
    bsz, s, d = x.shape
    tok = pl.BlockSpec((1, tm, d), lambda b, i: (b, i, 0))
    const = lambda shape: pl.BlockSpec(shape, lambda b, i: (0,) * len(shape))
    return pl.pallas_call(
        functools.partial(_ffn_kernel, sub=sub, chunks=chunks),
        grid=(bsz, s // tm),
        in_specs=[tok, pl.BlockSpec((1, N_ADA, d), lambda b, i: (b, 0, 0)),
                  const((1, d)), const((1, d)), _resident(wi.shape), _resident(wo.shape)],
        out_specs=tok,
        out_shape=jax.ShapeDtypeStruct((bsz, s, d), F32),
        compiler_params=_params("parallel", "parallel"),
        name="ffn",
    )(x, ada, g_pre, g_post, wi, wo)


def _pair_heads(w, axis):
    shape = w.shape
    pre, post = shape[:axis], shape[axis + 1:]
    w = w.reshape(pre + (A_KV_HEADS, A_GROUP, HEAD_DIM) + post)
    w = jnp.swapaxes(w, axis, axis + 1)
    return w.reshape(shape)


def kernel(x, c, positions, w_ada, b_ada, g_pre_mix, g_post_mix, w_in, b_f, sinks, w_branch_a,
           w_branch_b, w_out, g_pre_ffn, g_post_ffn, w_ffn_in, w_ffn_out):
    bsz, s, d = x.shape
    depth = w_ada.shape[0]
    tm = min(1024, s)
    sub = min(512, s)
    rep = LANES // ROPE_FREQS
    assert (s // rep) % sub == 0
    half = jnp.arange(0, HEAD_DIM, 2, dtype=F32) / HEAD_DIM
    inv_freq = jnp.tile(1.0 / (ROPE_THETA ** half), rep).reshape(1, LANES)
    pos_c = jnp.swapaxes(positions.astype(F32).reshape(bsz, rep, s // rep), 1, 2)
    pos_c = jnp.repeat(pos_c, ROPE_FREQS, axis=2)
    fine = sub // 2
    for l in range(depth):
        w_t = jnp.swapaxes(w_in[l], 0, 1)
        o_va, o_qb, o_vb = A_Q_W + A_KV_W, A_Q_W + 2 * A_KV_W, A_Q_W + 2 * A_KV_W + 2 * B_W
        w_f = jnp.pad(w_t[QKV_W:QKV_W + B_HEADS], ((0, LANES - B_HEADS), (0, 0)))
        w_qk = jnp.concatenate([_pair_heads(w_t[:A_Q_W], 0), w_t[o_qb:o_vb], w_t[A_Q_W:o_va], w_f],
                               axis=0).astype(BF16)
        w_vt = jnp.concatenate([w_t[o_va:o_qb], w_t[o_vb:QKV_W]], axis=0).astype(BF16)
        w_g = w_t[QKV_W + B_HEADS:].astype(BF16)
        bf_row = jnp.pad(b_f[l], (0, LANES - B_HEADS)).reshape(1, LANES)
        sink_row = jnp.repeat(sinks[l], WINDOW).reshape(1, A_Q_HEADS * WINDOW)
        w_a = _pair_heads(w_branch_a[l], 0).astype(BF16)
        w_b = w_branch_b[l].astype(BF16)
        w_o = w_out[l].astype(BF16)
        row = lambda g: g[l].reshape(1, d)

        if l == 0:
            ada, cos, sin = _ada_rope(c, w_ada[l], b_ada[l], pos_c, inv_freq)
        else:
            ada = _ada(c, w_ada[l], b_ada[l])
        ada = ada.reshape(bsz, N_ADA, d)
        qa, ka, vat, qb, kb, vbt, stats = _in_proj(x, cos, sin, ada, row(g_pre_mix), w_qk, w_vt, bf_row,
                                                   tm, sub)
        o_a = _swa(qa, ka, vat, sink_row, min(2 * tm, s))
        skip = _fox_plan(stats, 1)[:, :, :B_HEADS].reshape(-1)
        o_b = _fox(skip, qb, kb, vbt, 2 * sub)
        x, w_fi, w_fo = _post(x, o_a, o_b, ada, row(g_pre_mix), row(g_post_mix), w_g, w_a, w_b, w_o,
                              w_ffn_in[l], w_ffn_out[l], tm, fine)
        n_tiles = w_fo.shape[0] // MXU_DEPTH
        chunks = ((n_tiles + 1) // 2 * MXU_DEPTH, n_tiles // 2 * MXU_DEPTH)
        x = _ffn(x, ada, row(g_pre_ffn), row(g_post_ffn), w_fi, w_fo, tm, fine, chunks)
    return x
```
